```python
import math
import jax
import jax.numpy as jnp
from jax import lax
import numpy as np

D_MODEL = 2048
BATCH = 4
SEQ = 8192
DEPTH = 4

N_MIXERS = 3
N_ATTN_LAYERS = (DEPTH + 2) // 3
N_RWKV_LAYERS = (DEPTH + 1) // 3
N_POOL_LAYERS = DEPTH // 3

HEAD_DIM = 64
N_HEADS = D_MODEL // HEAD_DIM
N_KV_HEADS = max(1, N_HEADS // 8)
GQA_GROUP = N_HEADS // N_KV_HEADS
WINDOW = 128
Q_DIM = N_HEADS * HEAD_DIM
KV_DIM = N_KV_HEADS * HEAD_DIM
QKV_DIM = Q_DIM + 2 * KV_DIM

N_BUCKETS = 32
MAX_DISTANCE = 128

RWKV_HEAD = 64
RWKV_HEADS = D_MODEL // RWKV_HEAD
DECAY_LORA = max(32, int(round(1.8 * D_MODEL ** 0.5 / 32)) * 32)
AAA_LORA = max(32, int(round(1.8 * D_MODEL ** 0.5 / 32)) * 32)
GATE_LORA = max(32, int(round(0.6 * D_MODEL ** 0.8 / 32)) * 32)
GN_EPS = 64e-5

POOL_WINDOWS = (2, 4, 8, 16)
POOL_GROUP = D_MODEL // len(POOL_WINDOWS)

N_GROUPS = 4
EXPERTS_PER_GROUP = 8
N_EXPERTS = N_GROUPS * EXPERTS_PER_GROUP
TOP_K = 2
D_EXPERT = D_MODEL // 4
MOE_BLOCK = 256

NORM_EPS = 1e-6

kernel_name = 'hybrid_swa_rwkv7_pool_hmoe'


def _rms_norm(x, gain):
    xf = x.astype(jnp.float32)
    return xf * lax.rsqrt(jnp.mean(xf * xf, axis=-1, keepdims=True) + NORM_EPS) * gain.astype(jnp.float32)


def _ada_rms_norm(x, gain, mod):
    shift, scale, gate = jnp.split(mod, 3, axis=-1)
    h = _rms_norm(x, gain) * (1.0 + scale[:, None, :]) + shift[:, None, :]
    return h.astype(x.dtype), gate[:, None, :]


def _t5_bucket(dist):
    max_exact = N_BUCKETS // 2
    n = jnp.maximum(dist, 0)
    nf = jnp.maximum(n, 1).astype(jnp.float32)
    large = max_exact + (jnp.log(nf / max_exact) / math.log(MAX_DISTANCE / max_exact)
                         * (N_BUCKETS - max_exact)).astype(jnp.int32)
    large = jnp.minimum(large, N_BUCKETS - 1)
    return jnp.where(n < max_exact, n, large)


def _band_bias(rel_bias):
    i = jnp.arange(WINDOW)[:, None]
    j = jnp.arange(2 * WINDOW)[None, :]
    b = rel_bias[_t5_bucket(WINDOW + i - j)]
    return jnp.transpose(b, (2, 0, 1)).reshape(N_KV_HEADS, GQA_GROUP, WINDOW, 2 * WINDOW).astype(jnp.float32)


def _sliding_window_attention(h, bias, w_in, w_o, q_gain, k_gain, sinks):
    B_, S_, _ = h.shape
    nb = S_ // WINDOW
    qkv = h @ w_in
    q = qkv[..., :Q_DIM].reshape(B_, S_, N_KV_HEADS, GQA_GROUP, HEAD_DIM)
    k = qkv[..., Q_DIM:Q_DIM + KV_DIM].reshape(B_, S_, N_KV_HEADS, HEAD_DIM)
    v = qkv[..., Q_DIM + KV_DIM:].reshape(B_, S_, N_KV_HEADS, HEAD_DIM)
    q = _rms_norm(q, q_gain).reshape(B_, nb, WINDOW, N_KV_HEADS, GQA_GROUP, HEAD_DIM)
    k = _rms_norm(k, k_gain)

    def band(t):
        tb = t.astype(jnp.float32).reshape(B_, nb, WINDOW, N_KV_HEADS, HEAD_DIM)
        prev = jnp.pad(tb, ((0, 0), (1, 0), (0, 0), (0, 0), (0, 0)))[:, :-1]
        return jnp.concatenate([prev, tb], axis=2)

    kb, vb = band(k), band(v)
    logits = jnp.einsum('bnqhgd,bnkhd->bnhgqk', q, kb) * (HEAD_DIM ** -0.5) + bias
    i = jnp.arange(WINDOW)[:, None]
    j = jnp.arange(2 * WINDOW)[None, :]
    in_band = (j > i) & (j <= i + WINDOW)
    blk = jnp.arange(nb)[:, None, None]
    valid = in_band[None] & ((blk > 0) | (j >= WINDOW)[None])
    logits = jnp.where(valid[None, :, None, None], logits, -jnp.inf)
    sink = sinks.astype(jnp.float32).reshape(N_KV_HEADS, GQA_GROUP)[None, None, :, :, None]
    m = jnp.maximum(logits.max(axis=-1), sink)
    p = jnp.exp(logits - m[..., None])
    p = p / (p.sum(axis=-1) + jnp.exp(sink - m))[..., None]
    o = jnp.einsum('bnhgqk,bnkhd->bnqhgd', p, vb)
    o = o.reshape(B_, S_, Q_DIM).astype(h.dtype)
    return o @ w_o


def _rwkv7_time_mix(h, mu, w_rkv, w0, w1, w2, a0, a1, a2, g1, g2, k_k, k_a, r_k, lnx_g, lnx_b, w_o):
    B_, S_, D_ = h.shape
    f32 = jnp.float32
    xx = jnp.pad(h, ((0, 0), (1, 0), (0, 0)))[:, :-1] - h
    x_rkv = h[:, :, None, :] + xx[:, :, None, :] * mu[:3]
    rkv = jnp.einsum('bsjd,jde->bsje', x_rkv, w_rkv)
    xw = h + xx * mu[3]
    xa = h + xx * mu[4]
    xg = h + xx * mu[5]
    w_log = -jax.nn.softplus(-(w0 + jnp.tanh(xw @ w1) @ w2).astype(f32)) - 0.5
    a = jax.nn.sigmoid((a0 + (xa @ a1) @ a2).astype(f32))
    g = jax.nn.sigmoid(xg @ g1) @ g2

    def heads(t):
        return t.astype(f32).reshape(B_, S_, RWKV_HEADS, RWKV_HEAD)

    r, k, v = heads(rkv[:, :, 0]), heads(rkv[:, :, 1]), heads(rkv[:, :, 2])
    decay = jnp.exp(-jnp.exp(heads(w_log)))
    a = heads(a)
    kk = k * k_k.astype(f32).reshape(RWKV_HEADS, RWKV_HEAD)
    kk = kk / jnp.maximum(jnp.sqrt(jnp.sum(kk * kk, axis=-1, keepdims=True)), 1e-12)
    k = k * (1.0 + (a - 1.0) * k_a.astype(f32).reshape(RWKV_HEADS, RWKV_HEAD))

    def step(state, inp):
        r_t, w_t, k_t, v_t, a_t, b_t = inp
        sa = jnp.einsum('bhvk,bhk->bhv', state, a_t)
        state = state * w_t[:, :, None, :] + sa[..., None] * b_t[:, :, None, :] + v_t[..., None] * k_t[:, :, None, :]
        return state, jnp.einsum('bhvk,bhk->bhv', state, r_t)

    state0 = jnp.zeros((B_, RWKV_HEADS, RWKV_HEAD, RWKV_HEAD), f32)
    xs = tuple(jnp.moveaxis(t, 1, 0) for t in (r, decay, k, v, -kk, kk * a))
    _, y = lax.scan(step, state0, xs)
    y = jnp.moveaxis(y, 0, 1)
    mean = y.mean(axis=-1, keepdims=True)
    var = jnp.mean((y - mean) ** 2, axis=-1, keepdims=True)
    y = (y - mean) * lax.rsqrt(var + GN_EPS) * lnx_g.astype(f32).reshape(RWKV_HEADS, RWKV_HEAD) \
        + lnx_b.astype(f32).reshape(RWKV_HEADS, RWKV_HEAD)
    y = y + jnp.sum(r * k * r_k.astype(f32), axis=-1, keepdims=True) * v
    y = y.reshape(B_, S_, D_) * g
    return y.astype(h.dtype) @ w_o


def _multiscale_pool(h, w_grp, scale):
    B_, S_, D_ = h.shape
    f32 = jnp.float32
    hf = h.astype(f32)
    csum = jnp.concatenate([jnp.zeros((B_, 1, D_), f32), jnp.cumsum(hf, axis=1)], axis=1)
    counts = jnp.arange(1, S_ + 1, dtype=f32)
    groups = []
    for gi, w in enumerate(POOL_WINDOWS):
        sl = slice(gi * POOL_GROUP, (gi + 1) * POOL_GROUP)
        c_g = csum[:, :, sl]
        lagged = jnp.concatenate([jnp.zeros((B_, w - 1, POOL_GROUP), f32), c_g[:, :S_ - w + 1]], axis=1)
        mean = (c_g[:, 1:] - lagged) / jnp.minimum(counts, float(w))[None, :, None]
        groups.append(mean - hf[:, :, sl])
    pooled = jnp.stack(groups, axis=2)
    mixed = jnp.einsum('bsgc,gce->bsge', pooled, w_grp.astype(f32)).reshape(B_, S_, D_)
    return (mixed * scale.astype(f32)).astype(h.dtype)


def _hier_moe(h, w_grp, b_grp, w_exp, b_exp, w_gate, w_up, w_down):
    B_, S_, D_ = h.shape
    f32 = jnp.float32
    t = h.reshape(-1, D_)
    T = t.shape[0]
    grp_logits = (t @ w_grp).astype(f32) + b_grp.astype(f32)
    grp_prob = jax.nn.softmax(grp_logits, axis=-1)
    grp_idx = jnp.argmax(grp_logits, axis=-1).astype(jnp.int32)
    grp_w = jnp.take_along_axis(grp_prob, grp_idx[:, None], axis=-1)
    exp_logits = ((t @ w_exp).astype(f32) + b_exp.astype(f32)).reshape(T, N_GROUPS, EXPERTS_PER_GROUP)
    exp_logits = jnp.take_along_axis(exp_logits, grp_idx[:, None, None], axis=1)[:, 0]
    top_p, top_i = lax.top_k(jax.nn.softmax(exp_logits, axis=-1), TOP_K)
    weights = grp_w * top_p / jnp.sum(top_p, axis=-1, keepdims=True)
    flat_ids = (grp_idx[:, None] * EXPERTS_PER_GROUP + top_i).reshape(-1).astype(jnp.int32)

    N = T * TOP_K
    order = jnp.argsort(flat_ids).astype(jnp.int32)
    sorted_ids = flat_ids[order]
    counts = jnp.bincount(flat_ids, length=N_EXPERTS).astype(jnp.int32)
    start = jnp.cumsum(counts) - counts
    padded = ((counts + MOE_BLOCK - 1) // MOE_BLOCK) * MOE_BLOCK
    seg_end = jnp.cumsum(padded).astype(jnp.int32)
    pad_start = seg_end - padded
    rank = jnp.arange(N, dtype=jnp.int32) - start[sorted_ids]
    dest_sorted = pad_start[sorted_ids] + rank
    n_blocks = -(-N // MOE_BLOCK) + N_EXPERTS
    P = n_blocks * MOE_BLOCK
    slot_tok = jnp.full((P,), T, jnp.int32).at[dest_sorted].set(order // TOP_K)
    t_pad = jnp.concatenate([t, jnp.zeros((1, D_), t.dtype)], axis=0)
    buf = t_pad[slot_tok].reshape(n_blocks, MOE_BLOCK, D_)
    block_e = jnp.clip(jnp.searchsorted(seg_end, jnp.arange(n_blocks, dtype=jnp.int32) * MOE_BLOCK, side='right'),
                       0, N_EXPERTS - 1)

    def expert_block(args):
        xb, e = args
        hid = jax.nn.silu(xb @ w_gate[e]) * (xb @ w_up[e])
        return hid @ w_down[e]

    out = lax.map(expert_block, (buf, block_e)).reshape(P, D_)
    dest = jnp.zeros((N,), jnp.int32).at[order].set(dest_sorted)
    y = jnp.einsum('tkd,tk->td', out[dest].reshape(T, TOP_K, D_).astype(f32), weights)
    return y.reshape(B_, S_, D_).astype(h.dtype)


def setup_inputs(seed: int = 0) -> dict:
    key = jax.random.key(seed)
    keys = list(jax.random.split(key, 40))
    f32 = jnp.float32
    D = D_MODEL

    def nrm(shape, s):
        return jax.random.normal(keys.pop(), shape, f32) * s

    def near_one(shape):
        return 1.0 + nrm(shape, 0.1)

    NA, NR, NP = N_ATTN_LAYERS, N_RWKV_LAYERS, N_POOL_LAYERS
    inputs = {
        'x': nrm((BATCH, SEQ, D), 1.0),
        'c': nrm((BATCH, D), 1.0),
        'norm_g': near_one((DEPTH, 2, D)),
        'ada_w': nrm((DEPTH, 2, D, 3 * D), 0.5 * D ** -0.5),
        'ada_b': nrm((DEPTH, 2, 3 * D), 0.02),
        'rel_bias': nrm((N_BUCKETS, N_HEADS), 0.5),
        'attn_w_in': nrm((NA, D, QKV_DIM), D ** -0.5),
        'attn_w_o': nrm((NA, Q_DIM, D), Q_DIM ** -0.5),
        'attn_q_gain': near_one((NA, HEAD_DIM)),
        'attn_k_gain': near_one((NA, HEAD_DIM)),
        'attn_sinks': nrm((NA, N_HEADS), 0.5),
        'rw_mu': jax.random.uniform(keys.pop(), (NR, 6, D), f32),
        'rw_w_rkv': nrm((NR, 3, D, D), D ** -0.5),
        'rw_w0': -0.5 + nrm((NR, D), 0.5),
        'rw_w1': nrm((NR, D, DECAY_LORA), D ** -0.5),
        'rw_w2': nrm((NR, DECAY_LORA, D), 0.5 * DECAY_LORA ** -0.5),
        'rw_a0': nrm((NR, D), 0.5),
        'rw_a1': nrm((NR, D, AAA_LORA), D ** -0.5),
        'rw_a2': nrm((NR, AAA_LORA, D), 0.5 * AAA_LORA ** -0.5),
        'rw_g1': nrm((NR, D, GATE_LORA), D ** -0.5),
        'rw_g2': nrm((NR, GATE_LORA, D), GATE_LORA ** -0.5),
        'rw_k_k': near_one((NR, D)),
        'rw_k_a': near_one((NR, D)),
        'rw_r_k': nrm((NR, RWKV_HEADS, RWKV_HEAD), 0.1),
        'rw_lnx_g': near_one((NR, D)),
        'rw_lnx_b': nrm((NR, D), 0.02),
        'rw_w_o': nrm((NR, D, D), D ** -0.5),
        'pool_w': nrm((NP, len(POOL_WINDOWS), POOL_GROUP, POOL_GROUP), POOL_GROUP ** -0.5),
        'pool_scale': near_one((NP, D)),
        'moe_w_grp': nrm((DEPTH, D, N_GROUPS), D ** -0.5),
        'moe_b_grp': nrm((DEPTH, N_GROUPS), 0.01),
        'moe_w_exp': nrm((DEPTH, D, N_EXPERTS), D ** -0.5),
        'moe_b_exp': nrm((DEPTH, N_EXPERTS), 0.01),
        'moe_w_gate': nrm((DEPTH, N_EXPERTS, D, D_EXPERT), D ** -0.5),
        'moe_w_up': nrm((DEPTH, N_EXPERTS, D, D_EXPERT), D ** -0.5),
        'moe_w_down': nrm((DEPTH, N_EXPERTS, D_EXPERT, D), D_EXPERT ** -0.5),
    }
    return inputs


def reference(x, c, norm_g, ada_w, ada_b, rel_bias, attn_w_in, attn_w_o, attn_q_gain, attn_k_gain, attn_sinks,
              rw_mu, rw_w_rkv, rw_w0, rw_w1, rw_w2, rw_a0, rw_a1, rw_a2, rw_g1, rw_g2, rw_k_k, rw_k_a, rw_r_k,
              rw_lnx_g, rw_lnx_b, rw_w_o, pool_w, pool_scale,
              moe_w_grp, moe_b_grp, moe_w_exp, moe_b_exp, moe_w_gate, moe_w_up, moe_w_down):
    c_act = jax.nn.silu(c)
    bias = _band_bias(rel_bias)
    for layer in range(DEPTH):
        kind, idx = layer % N_MIXERS, layer // N_MIXERS
        h, gate = _ada_rms_norm(x, norm_g[layer, 0], c_act @ ada_w[layer, 0] + ada_b[layer, 0])
        if kind == 0:
            y = _sliding_window_attention(h, bias, attn_w_in[idx], attn_w_o[idx], attn_q_gain[idx],
                                          attn_k_gain[idx], attn_sinks[idx])
        elif kind == 1:
            y = _rwkv7_time_mix(h, rw_mu[idx], rw_w_rkv[idx], rw_w0[idx], rw_w1[idx], rw_w2[idx],
                                rw_a0[idx], rw_a1[idx], rw_a2[idx], rw_g1[idx], rw_g2[idx],
                                rw_k_k[idx], rw_k_a[idx], rw_r_k[idx], rw_lnx_g[idx], rw_lnx_b[idx], rw_w_o[idx])
        else:
            y = _multiscale_pool(h, pool_w[idx], pool_scale[idx])
        x = x + gate * y
        h, gate = _ada_rms_norm(x, norm_g[layer, 1], c_act @ ada_w[layer, 1] + ada_b[layer, 1])
        x = x + gate * _hier_moe(h, moe_w_grp[layer], moe_b_grp[layer], moe_w_exp[layer], moe_b_exp[layer],
                                 moe_w_gate[layer], moe_w_up[layer], moe_w_down[layer])
    return x
```

```python
import functools
import math

import jax
import jax.numpy as jnp
import numpy as np
from jax import lax
from jax.experimental import pallas as pl
from jax.experimental.pallas import tpu as pltpu

f32 = jnp.float32
bf16 = jnp.bfloat16

D_MODEL = 2048
DEPTH = 4
N_MIXERS = 3
HEAD_DIM = 64
N_HEADS = D_MODEL // HEAD_DIM
N_KV_HEADS = 4
GQA_GROUP = N_HEADS // N_KV_HEADS
WINDOW = 128
Q_DIM = N_HEADS * HEAD_DIM
KV_DIM = N_KV_HEADS * HEAD_DIM
QKV_DIM = Q_DIM + 2 * KV_DIM
N_BUCKETS = 32
MAX_DISTANCE = 128
RWKV_HEAD = 64
GN_EPS = 64e-5
POOL_WINDOWS = (2, 4, 8, 16)
POOL_GROUP = D_MODEL // len(POOL_WINDOWS)
N_GROUPS = 4
EXPERTS_PER_GROUP = 8
N_EXPERTS = N_GROUPS * EXPERTS_PER_GROUP
TOP_K = 2
D_EXPERT = D_MODEL // 4
NORM_EPS = 1e-6

LANES = 128
SUBLANES = 8
VMEM_LIMIT = 56 * 1024 * 1024

MOE_ROWS = 256
WKV_CHUNK = 64
EXP_LANE0 = 32
NEG_BIG = -1e30


def _cparams(sem):
    return pltpu.CompilerParams(dimension_semantics=sem, vmem_limit_bytes=VMEM_LIMIT)


def _ada_norm(xf, gain, shift, scale):
    ms = jnp.mean(xf * xf, axis=-1, keepdims=True)
    return xf * lax.rsqrt(ms + NORM_EPS) * gain * (1.0 + scale) + shift


def _sigmoid(z):
    return 1.0 / (1.0 + jnp.exp(-z))


def _split3(x):
    hi = x.astype(bf16)
    r1 = x - hi.astype(f32)
    mid = r1.astype(bf16)
    lo = (r1 - mid.astype(f32)).astype(bf16)
    return hi, mid, lo


def _dot(a, b):
    return jnp.dot(a, b, preferred_element_type=f32)


def _dot_nt(a, b):
    return lax.dot_general(a, b, (((1,), (1,)), ((), ())), preferred_element_type=f32)


def _dot_tn(a, b):
    return lax.dot_general(a, b, (((0,), (0,)), ((), ())), preferred_element_type=f32)


def _ada_kernel(c_ref, w_ref, b_ref, o_ref):
    c = c_ref[...]
    ca = (c * _sigmoid(c)).astype(bf16)
    o_ref[0] = _dot(ca, w_ref[0].astype(bf16)) + b_ref[0]


def _ada_mods(c, ada_w, ada_b):
    B, D = c.shape
    n_mod = ada_w.shape[0] * ada_w.shape[1]
    N = ada_w.shape[-1]
    tn = 1024
    c_pad = jnp.zeros((SUBLANES, D), f32).at[:B].set(c)
    out = pl.pallas_call(
        _ada_kernel,
        out_shape=jax.ShapeDtypeStruct((n_mod, SUBLANES, N), f32),
        grid=(n_mod, N // tn),
        in_specs=[
            pl.BlockSpec((SUBLANES, D), lambda m, j: (0, 0)),
            pl.BlockSpec((1, D, tn), lambda m, j: (m, 0, j)),
            pl.BlockSpec((1, 1, tn), lambda m, j: (m, 0, j)),
        ],
        out_specs=pl.BlockSpec((1, SUBLANES, tn), lambda m, j: (m, 0, j)),
        compiler_params=_cparams(("parallel", "parallel")),
        name="ada_mods",
    )(c_pad, ada_w.reshape(n_mod, D, N), ada_b.reshape(n_mod, 1, N))
    return out[:, :B].reshape(n_mod, B, 1, N)


def _norm_mm_kernel(*refs, tiles_per_batch, mix):
    if mix:
        x_ref, xh_ref, g_ref, sh_ref, sc_ref, mu_ref, w_ref, o_ref, lhs_ref = refs
    else:
        x_ref, g_ref, sh_ref, sc_ref, w_ref, o_ref, lhs_ref = refs
    i = pl.program_id(0)

    @pl.when(pl.program_id(1) == 0)
    def _():
        g, sh, sc = g_ref[...], sh_ref[0], sc_ref[0]
        h = _ada_norm(x_ref[...], g, sh, sc)
        if mix:
            h_last = _ada_norm(xh_ref[...], g, sh, sc)[SUBLANES - 1:SUBLANES]
            h_last = jnp.where(i % tiles_per_batch == 0, 0.0, h_last)
            row = lax.broadcasted_iota(jnp.int32, h.shape, 0)
            h_prev = jnp.where(row == 0, h_last, pltpu.roll(h, 1, axis=0))
            h = h + (h_prev - h) * mu_ref[...]
        lhs_ref[...] = h.astype(bf16)

    o_ref[...] = _dot(lhs_ref[...], w_ref[...]).astype(o_ref.dtype)


def _norm_mm(x, gain, mod, w, *, seq, mu=None, out_dtype=bf16, tm=512, tn=512):
    T, D = x.shape
    N = w.shape[1]
    tn = min(tn, N)
    tpb = seq // tm
    mix = mu is not None
    row_spec = pl.BlockSpec((tm, D), lambda i, j: (i, 0))
    vec_spec = pl.BlockSpec((1, D), lambda i, j: (0, 0))
    shift_spec = pl.BlockSpec((1, 1, D), lambda i, j: (i // tpb, 0, 0))
    scale_spec = pl.BlockSpec((1, 1, D), lambda i, j: (i // tpb, 0, 1))
    w_spec = pl.BlockSpec((D, tn), lambda i, j: (0, j))
    if mix:
        halo_spec = pl.BlockSpec((SUBLANES, D), lambda i, j: (jnp.maximum(i * (tm // SUBLANES) - 1, 0), 0))
        in_specs = [row_spec, halo_spec, vec_spec, shift_spec, scale_spec, vec_spec, w_spec]
        args = (x, x, gain, mod, mod, mu, w)
    else:
        in_specs = [row_spec, vec_spec, shift_spec, scale_spec, w_spec]
        args = (x, gain, mod, mod, w)
    return pl.pallas_call(
        functools.partial(_norm_mm_kernel, tiles_per_batch=tpb, mix=mix),
        out_shape=jax.ShapeDtypeStruct((T, N), out_dtype),
        grid=(T // tm, N // tn),
        in_specs=in_specs,
        out_specs=pl.BlockSpec((tm, tn), lambda i, j: (i, j)),
        scratch_shapes=[pltpu.VMEM((tm, D), bf16)],
        compiler_params=_cparams(("parallel", "arbitrary")),
        name="norm_mm_mix" if mix else "norm_mm",
    )(*args)


def _mm_res_kernel(a_ref, w_ref, x_ref, gate_ref, o_ref):
    o_ref[...] = x_ref[...] + gate_ref[0] * _dot(a_ref[...], w_ref[...])


def _mm_res(a, w, x, mod, *, seq, tm=1024, tn=512):
    T, K = a.shape
    N = w.shape[1]
    tpb = seq // tm
    gate_blk = 2 * (N // tn)
    return pl.pallas_call(
        _mm_res_kernel,
        out_shape=jax.ShapeDtypeStruct((T, N), f32),
        grid=(T // tm, N // tn),
        in_specs=[
            pl.BlockSpec((tm, K), lambda i, j: (i, 0)),
            pl.BlockSpec((K, tn), lambda i, j: (0, j)),
            pl.BlockSpec((tm, tn), lambda i, j: (i, j)),
            pl.BlockSpec((1, 1, tn), lambda i, j: (i // tpb, 0, gate_blk + j)),
        ],
        out_specs=pl.BlockSpec((tm, tn), lambda i, j: (i, j)),
        compiler_params=_cparams(("parallel", "parallel")),
        name="mm_res",
    )(a, w, x, mod)


def _attn_kernel(sink_ref, q_ref, kp_ref, kc_ref, vp_ref, vc_ref, bias_ref, qg_ref, kg_ref, o_ref):
    n = pl.program_id(1)
    W = WINDOW
    q_all = q_ref[...].astype(f32)
    k_band = jnp.concatenate([kp_ref[...], kc_ref[...]], axis=0).astype(f32)
    v_band = jnp.concatenate([vp_ref[...], vc_ref[...]], axis=0)
    qg = qg_ref[...]
    kg = kg_ref[...]
    col = lax.broadcasted_iota(jnp.int32, (GQA_GROUP * W, 2 * W), 1)
    key_ok = jnp.logical_or(n > 0, col >= W)
    grp = lax.broadcasted_iota(jnp.int32, (GQA_GROUP * W, 1), 0) // W
    for h in range(N_KV_HEADS):
        k_h = k_band[:, h * HEAD_DIM:(h + 1) * HEAD_DIM]
        k_h = k_h * lax.rsqrt(jnp.mean(k_h * k_h, axis=-1, keepdims=True) + NORM_EPS) * kg
        v_h = v_band[:, h * HEAD_DIM:(h + 1) * HEAD_DIM]
        q_h = jnp.concatenate(
            [q_all[:, (h * GQA_GROUP + g) * HEAD_DIM:(h * GQA_GROUP + g + 1) * HEAD_DIM] for g in range(GQA_GROUP)],
            axis=0)
        q_h = q_h * lax.rsqrt(jnp.mean(q_h * q_h, axis=-1, keepdims=True) + NORM_EPS) * qg
        logits = _dot_nt(q_h.astype(bf16), k_h.astype(bf16)) * (HEAD_DIM ** -0.5)
        logits = logits + bias_ref[h].reshape(GQA_GROUP * W, 2 * W)
        logits = jnp.where(key_ok, logits, NEG_BIG)
        sink = jnp.zeros((GQA_GROUP * W, 1), f32)
        for g in range(GQA_GROUP):
            sink = jnp.where(grp == g, sink_ref[h * GQA_GROUP + g], sink)
        m = jnp.maximum(jnp.max(logits, axis=-1, keepdims=True), sink)
        p = jnp.exp(logits - m)
        denom = jnp.sum(p, axis=-1, keepdims=True) + jnp.exp(sink - m)
        o_h = _dot(p.astype(bf16), v_h) / denom
        for g in range(GQA_GROUP):
            c0 = (h * GQA_GROUP + g) * HEAD_DIM
            o_ref[:, c0:c0 + HEAD_DIM] = o_h[g * W:(g + 1) * W].astype(o_ref.dtype)


def _band_bias_masked(rel_bias):
    max_exact = N_BUCKETS // 2
    i = jnp.arange(WINDOW)[:, None]
    j = jnp.arange(2 * WINDOW)[None, :]
    dist = WINDOW + i - j
    nn = jnp.maximum(dist, 0)
    nf = jnp.maximum(nn, 1).astype(f32)
    large = max_exact + (jnp.log(nf / max_exact) / math.log(MAX_DISTANCE / max_exact)
                         * (N_BUCKETS - max_exact)).astype(jnp.int32)
    large = jnp.minimum(large, N_BUCKETS - 1)
    bucket = jnp.where(nn < max_exact, nn, large)
    b = jnp.transpose(rel_bias[bucket], (2, 0, 1)).astype(f32)
    in_band = (j > i) & (j <= i + WINDOW)
    b = jnp.where(in_band[None], b, NEG_BIG)
    return b.reshape(N_KV_HEADS, GQA_GROUP, WINDOW, 2 * WINDOW)


def _attention(qkv, bias, q_gain, k_gain, sinks, *, batch, seq):
    T = qkv.shape[0]
    nb = seq // WINDOW
    kcol = Q_DIM // KV_DIM
    vcol = kcol + 1

    def cur(col):
        return lambda b, n, s: (b * nb + n, col)

    def prev(col):
        return lambda b, n, s: (b * nb + jnp.maximum(n - 1, 0), col)

    grid_spec = pltpu.PrefetchScalarGridSpec(
        num_scalar_prefetch=1,
        grid=(batch, nb),
        in_specs=[
            pl.BlockSpec((WINDOW, Q_DIM), lambda b, n, s: (b * nb + n, 0)),
            pl.BlockSpec((WINDOW, KV_DIM), prev(kcol)),
            pl.BlockSpec((WINDOW, KV_DIM), cur(kcol)),
            pl.BlockSpec((WINDOW, KV_DIM), prev(vcol)),
            pl.BlockSpec((WINDOW, KV_DIM), cur(vcol)),
            pl.BlockSpec((N_KV_HEADS, GQA_GROUP, WINDOW, 2 * WINDOW), lambda b, n, s: (0, 0, 0, 0)),
            pl.BlockSpec((1, HEAD_DIM), lambda b, n, s: (0, 0)),
            pl.BlockSpec((1, HEAD_DIM), lambda b, n, s: (0, 0)),
        ],
        out_specs=pl.BlockSpec((WINDOW, Q_DIM), lambda b, n, s: (b * nb + n, 0)),
    )
    return pl.pallas_call(
        _attn_kernel,
        out_shape=jax.ShapeDtypeStruct((T, Q_DIM), bf16),
        grid_spec=grid_spec,
        compiler_params=_cparams(("parallel", "parallel")),
        name="swa_attention",
    )(sinks, qkv, qkv, qkv, qkv, qkv, bias, q_gain.reshape(1, HEAD_DIM), k_gain.reshape(1, HEAD_DIM))


def _wkv_kernel(r_ref, k_ref, v_ref, lw_ref, la_ref, lg_ref, w0_ref, a0_ref, w2_ref, a2_ref, g2_ref,
                kk_ref, ka_ref, rk_ref, lng_ref, lnb_ref, o_ref,
                state_ref, sr, sk, sv, slogw, scum, sa, sg, so, *, n_pairs, unroll):
    L = WKV_CHUNK
    P2 = 2 * L

    @pl.when(pl.program_id(1) == 0)
    def _():
        state_ref[...] = jnp.zeros(state_ref.shape, f32)

    w_lin = w0_ref[...] + _dot(jnp.tanh(lw_ref[...]).astype(bf16), w2_ref[...])
    neg = -w_lin
    softplus = jnp.maximum(neg, 0.0) + jnp.log(1.0 + jnp.exp(-jnp.abs(neg)))
    logw = -jnp.exp(-softplus - 0.5)
    a_sig = _sigmoid(a0_ref[...] + _dot(la_ref[...].astype(bf16), a2_ref[...]))
    gate = _dot(_sigmoid(lg_ref[...]).astype(bf16), g2_ref[...])
    ti = lax.broadcasted_iota(jnp.int32, (L, L), 0)
    tj = lax.broadcasted_iota(jnp.int32, (L, L), 1)
    tri = jnp.where(ti >= tj, 1.0, 0.0).astype(bf16)
    hi, mid, lo = _split3(logw)
    cum = _dot(tri, hi) + _dot(tri, mid) + _dot(tri, lo)
    for p in range(n_pairs):
        cs = slice(p * LANES, (p + 1) * LANES)
        sr[p] = r_ref[:, cs].astype(f32)
        sk[p] = k_ref[:, cs].astype(f32)
        sv[p] = v_ref[:, cs].astype(f32)
        slogw[p] = logw[:, cs]
        scum[p] = cum[:, cs]
        sa[p] = a_sig[:, cs]
        sg[p] = gate[:, cs]

    lane = lax.broadcasted_iota(jnp.int32, (1, LANES), 1)
    m0 = lane < RWKV_HEAD
    ri = lax.broadcasted_iota(jnp.int32, (P2, P2), 0)
    ci = lax.broadcasted_iota(jnp.int32, (P2, P2), 1)
    same = (ri // L) == (ci // L)
    strict = jnp.logical_and(same, ri > ci)
    incl = jnp.logical_and(same, ri >= ci)
    eye = jnp.where(ri == ci, 1.0, 0.0)

    def seg_sum(x):
        s0 = jnp.sum(jnp.where(m0, x, 0.0), axis=-1, keepdims=True)
        s1 = jnp.sum(jnp.where(m0, 0.0, x), axis=-1, keepdims=True)
        return jnp.where(m0, s0, s1)

    def bd(x):
        return jnp.concatenate([jnp.where(m0, x, 0.0), jnp.where(m0, 0.0, x)], axis=0)

    def mm(a, b):
        return _dot(a.astype(bf16), b.astype(bf16))

    def pair_body(p, carry):
        r, k, v = sr[p], sk[p], sv[p]
        lw_, cm, a = slogw[p], scum[p], sa[p]
        kk = k * kk_ref[p]
        kk = kk / jnp.maximum(jnp.sqrt(seg_sum(kk * kk)), 1e-12)
        kmod = k * (1.0 + (a - 1.0) * ka_ref[p])
        e_pos = jnp.exp(cm)
        e_neg = jnp.exp(-cm)
        e_exc = jnp.exp(cm - lw_)
        r_t = bd(r * e_pos)
        a_t = bd(-kk * e_exc)
        b_t = bd(kk * a * e_neg)
        k_t = bd(kmod * e_neg)
        v_b = bd(v)
        G = _dot_nt(jnp.concatenate([a_t, r_t], axis=0).astype(bf16),
                    jnp.concatenate([b_t, k_t], axis=0).astype(bf16))
        A_ab = jnp.where(strict, G[:P2, :P2], 0.0)
        A_ak = jnp.where(strict, G[:P2, P2:], 0.0)
        A_rb = jnp.where(incl, G[P2:, :P2], 0.0)
        A_rk = jnp.where(incl, G[P2:, P2:], 0.0)
        X = eye + A_ab
        Pw = A_ab
        n_sq = int(math.log2(L)) - 1
        for _ in range(n_sq):
            Pw = mm(Pw, Pw)
            X = X + mm(X, Pw)
        AU = mm(X, jnp.concatenate([a_t, mm(A_ak, v_b)], axis=1))
        RY = mm(A_rb, AU)
        R_hat = r_t + RY[:, :LANES]
        Y_hat = RY[:, LANES:] + mm(A_rk, v_b)
        A_hat = AU[:, :LANES]
        U_hat = AU[:, LANES:]
        M_k = _dot_tn(A_hat.astype(bf16), b_t.astype(bf16))
        N_s = _dot_tn(jnp.concatenate([U_hat, v_b], axis=0).astype(bf16),
                      jnp.concatenate([b_t, k_t], axis=0).astype(bf16))
        S = state_ref[p]
        Y = _dot_nt(R_hat.astype(bf16), S.astype(bf16)) + Y_hat
        g_last = e_pos[L - 1:L, :]
        state_ref[p] = (S + mm(S, M_k) + N_s) * g_last
        y = Y[:L] + Y[L:]
        mean = seg_sum(y) * (1.0 / RWKV_HEAD)
        yc = y - mean
        var = seg_sum(yc * yc) * (1.0 / RWKV_HEAD)
        yn = yc * lax.rsqrt(var + GN_EPS) * lng_ref[p] + lnb_ref[p]
        yn = yn + seg_sum(r * kmod * rk_ref[p]) * v
        so[p] = (yn * sg[p]).astype(so.dtype)
        return carry

    lax.fori_loop(0, n_pairs, pair_body, 0, unroll=unroll)
    for p in range(n_pairs):
        o_ref[:, p * LANES:(p + 1) * LANES] = so[p]


def _wkv(r, k, v, lw, la, lg, w0, a0, w2, a2, g2, k_k, k_a, r_k, lnx_g, lnx_b, *, batch, seq, unroll=2):
    T, D = r.shape
    L = WKV_CHUNK
    n_pairs = D // LANES
    nc = seq // L

    def row(width):
        return pl.BlockSpec((L, width), lambda b, t: (b * nc + t, 0))

    def full(shape):
        return pl.BlockSpec(shape, lambda b, t: (0,) * len(shape))

    pair_vec = lambda a: a.reshape(n_pairs, 1, LANES).astype(f32)
    pv_spec = full((n_pairs, 1, LANES))
    pair_scr = pltpu.VMEM((n_pairs, L, LANES), f32)
    return pl.pallas_call(
        functools.partial(_wkv_kernel, n_pairs=n_pairs, unroll=unroll),
        out_shape=jax.ShapeDtypeStruct((T, D), bf16),
        grid=(batch, nc),
        in_specs=[row(D), row(D), row(D), row(lw.shape[1]), row(la.shape[1]), row(lg.shape[1]),
                  full((1, D)), full((1, D)), full(w2.shape), full(a2.shape), full(g2.shape),
                  pv_spec, pv_spec, pv_spec, pv_spec, pv_spec],
        out_specs=row(D),
        scratch_shapes=[pltpu.VMEM((n_pairs, 2 * RWKV_HEAD, 2 * RWKV_HEAD), f32)] + [pair_scr] * 7
                       + [pltpu.VMEM((n_pairs, L, LANES), bf16)],
        compiler_params=_cparams(("parallel", "arbitrary")),
        name="wkv7",
    )(r, k, v, lw, la, lg, w0.reshape(1, D), a0.reshape(1, D), w2, a2, g2,
      pair_vec(k_k), pair_vec(k_a), pair_vec(r_k), pair_vec(lnx_g), pair_vec(lnx_b))


def _pool_kernel(x_ref, xh_ref, g_ref, sh_ref, sc_ref, gate_ref, w_ref, ps_ref, o_ref, *, tiles_per_batch, tm):
    i = pl.program_id(0)
    halo = 2 * SUBLANES
    g, sh, sc = g_ref[...], sh_ref[0], sc_ref[0]
    x = x_ref[...]
    h = _ada_norm(x, g, sh, sc)
    hh = _ada_norm(xh_ref[...], g, sh, sc)
    hh = jnp.where(i % tiles_per_batch == 0, 0.0, hh)
    ext = jnp.concatenate([hh, h], axis=0)
    pos = (i % tiles_per_batch) * tm + lax.broadcasted_iota(jnp.int32, (tm, 1), 0)
    for gi, w in enumerate(POOL_WINDOWS):
        cs = slice(gi * POOL_GROUP, (gi + 1) * POOL_GROUP)
        s = ext[:, cs]
        d = 1
        while d < w:
            s = s + pltpu.roll(s, d, axis=0)
            d *= 2
        cnt = jnp.minimum(pos + 1, w).astype(f32)
        pooled = s[halo:] / cnt - h[:, cs]
        mixed = _dot(pooled.astype(bf16), w_ref[gi]) * ps_ref[:, cs]
        o_ref[:, cs] = x[:, cs] + gate_ref[0][:, cs] * mixed


def _pool_layer(x, gain, mod, pool_w, pool_scale, *, seq, tm=512):
    T, D = x.shape
    tpb = seq // tm
    halo = 2 * SUBLANES
    vec = pl.BlockSpec((1, D), lambda i: (0, 0))
    return pl.pallas_call(
        functools.partial(_pool_kernel, tiles_per_batch=tpb, tm=tm),
        out_shape=jax.ShapeDtypeStruct((T, D), f32),
        grid=(T // tm,),
        in_specs=[
            pl.BlockSpec((tm, D), lambda i: (i, 0)),
            pl.BlockSpec((halo, D), lambda i: (jnp.maximum(i * (tm // halo) - 1, 0), 0)),
            vec,
            pl.BlockSpec((1, 1, D), lambda i: (i // tpb, 0, 0)),
            pl.BlockSpec((1, 1, D), lambda i: (i // tpb, 0, 1)),
            pl.BlockSpec((1, 1, D), lambda i: (i // tpb, 0, 2)),
            pl.BlockSpec(pool_w.shape, lambda i: (0, 0, 0)),
            vec,
        ],
        out_specs=pl.BlockSpec((tm, D), lambda i: (i, 0)),
        compiler_params=_cparams(("parallel",)),
        name="pool_mixer",
    )(x, x, gain, mod, mod, mod, pool_w, pool_scale.reshape(1, D))


def _route_kernel(x_ref, g_ref, sh_ref, sc_ref, w_ref, b_ref, h_ref, route_ref):
    h = _ada_norm(x_ref[...], g_ref[...], sh_ref[0], sc_ref[0])
    h_ref[...] = h
    h_hi, h_mid, _ = _split3(h)
    w_hi, w_mid, _ = _split3(w_ref[...])
    logits = _dot(h_hi, w_hi) + (_dot(h_hi, w_mid) + _dot(h_mid, w_hi)) + b_ref[...]
    lane = lax.broadcasted_iota(jnp.int32, logits.shape, 1)
    big = jnp.int32(LANES)

    def masked_argmax(mask):
        mx = jnp.max(jnp.where(mask, logits, NEG_BIG), axis=-1, keepdims=True)
        idx = jnp.min(jnp.where(jnp.logical_and(mask, logits == mx), lane, big), axis=-1, keepdims=True)
        return mx, idx

    gmask = lane < N_GROUPS
    gmax, gidx = masked_argmax(gmask)
    grp_w = 1.0 / jnp.sum(jnp.where(gmask, jnp.exp(logits - gmax), 0.0), axis=-1, keepdims=True)
    e_lo = EXP_LANE0 + gidx * EXPERTS_PER_GROUP
    emask = jnp.logical_and(lane >= e_lo, lane < e_lo + EXPERTS_PER_GROUP)
    m1, i1 = masked_argmax(emask)
    m2, i2 = masked_argmax(jnp.logical_and(emask, lane != i1))
    e21 = jnp.exp(m2 - m1)
    w1 = grp_w / (1.0 + e21)
    w2 = grp_w * e21 / (1.0 + e21)
    out = jnp.where(lane == 0, (i1 - EXP_LANE0).astype(f32), 0.0)
    out = jnp.where(lane == 1, (i2 - EXP_LANE0).astype(f32), out)
    out = jnp.where(lane == 2, w1, out)
    out = jnp.where(lane == 3, w2, out)
    route_ref[...] = out


def _route(x, gain, mod, w_router, b_router, *, seq, tm=512):
    T, D = x.shape
    tpb = seq // tm
    vec = pl.BlockSpec((1, D), lambda i: (0, 0))
    return pl.pallas_call(
        _route_kernel,
        out_shape=(jax.ShapeDtypeStruct((T, D), f32), jax.ShapeDtypeStruct((T, LANES), f32)),
        grid=(T // tm,),
        in_specs=[
            pl.BlockSpec((tm, D), lambda i: (i, 0)),
            vec,
            pl.BlockSpec((1, 1, D), lambda i: (i // tpb, 0, 0)),
            pl.BlockSpec((1, 1, D), lambda i: (i // tpb, 0, 1)),
            pl.BlockSpec((D, LANES), lambda i: (0, 0)),
            pl.BlockSpec((1, LANES), lambda i: (0, 0)),
        ],
        out_specs=(pl.BlockSpec((tm, D), lambda i: (i, 0)), pl.BlockSpec((tm, LANES), lambda i: (i, 0))),
        compiler_params=_cparams(("parallel",)),
        name="moe_route",
    )(x, gain, mod, mod, w_router, b_router)


def _dispatch_kernel(dest_ref, h_hbm, buf_in, buf_hbm, sem, *, tm, n_steps):
    del buf_in
    i = pl.program_id(0)

    def issue(r, carry):
        t = i * tm + r
        for kk in range(TOP_K):
            d = dest_ref[t * TOP_K + kk]
            pltpu.make_async_copy(h_hbm.at[pl.ds(t, 1)], buf_hbm.at[pl.ds(d, 1)], sem).start()
        return carry

    lax.fori_loop(0, tm, issue, 0)

    def wait_step():
        pltpu.make_async_copy(h_hbm.at[pl.ds(0, tm * TOP_K)], buf_hbm.at[pl.ds(0, tm * TOP_K)], sem).wait()

    @pl.when(i > 0)
    def _():
        wait_step()

    @pl.when(i == n_steps - 1)
    def _():
        wait_step()


def _dispatch(h, dest, n_rows, *, tm=512):
    T, D = h.shape
    n_steps = T // tm
    buf0 = jnp.zeros((n_rows, D), h.dtype)
    grid_spec = pltpu.PrefetchScalarGridSpec(
        num_scalar_prefetch=1,
        grid=(n_steps,),
        in_specs=[pl.BlockSpec(memory_space=pl.ANY), pl.BlockSpec(memory_space=pl.ANY)],
        out_specs=pl.BlockSpec(memory_space=pl.ANY),
        scratch_shapes=[pltpu.SemaphoreType.DMA(())],
    )
    return pl.pallas_call(
        functools.partial(_dispatch_kernel, tm=tm, n_steps=n_steps),
        out_shape=jax.ShapeDtypeStruct((n_rows, D), h.dtype),
        grid_spec=grid_spec,
        input_output_aliases={2: 0},
        compiler_params=pltpu.CompilerParams(dimension_semantics=("arbitrary",), has_side_effects=True),
        name="moe_dispatch",
    )(dest, h, buf0)


def _expert_kernel(be_ref, nu_ref, x_ref, wg_ref, wu_ref, wd_ref, o_ref):
    used = pl.program_id(0) < nu_ref[0]

    @pl.when(used)
    def _():
        xb = x_ref[...].astype(bf16)
        gate = _dot(xb, wg_ref[0])
        up = _dot(xb, wu_ref[0])
        hid = (gate * _sigmoid(gate) * up).astype(bf16)
        o_ref[...] = _dot(hid, wd_ref[0])

    @pl.when(jnp.logical_not(used))
    def _():
        o_ref[...] = jnp.zeros(o_ref.shape, o_ref.dtype)


def _experts(buf, block_e, n_used, w_gate, w_up, w_down):
    P, D = buf.shape
    n_blocks = P // MOE_ROWS
    DE = w_gate.shape[-1]

    def rows(i, be, nu):
        return (jnp.minimum(i, nu[0] - 1), 0)

    def wsel(i, be, nu):
        return (be[jnp.minimum(i, nu[0] - 1)], 0, 0)

    grid_spec = pltpu.PrefetchScalarGridSpec(
        num_scalar_prefetch=2,
        grid=(n_blocks,),
        in_specs=[
            pl.BlockSpec((MOE_ROWS, D), rows),
            pl.BlockSpec((1, D, DE), wsel),
            pl.BlockSpec((1, D, DE), wsel),
            pl.BlockSpec((1, DE, D), wsel),
        ],
        out_specs=pl.BlockSpec((MOE_ROWS, D), lambda i, be, nu: (i, 0)),
    )
    return pl.pallas_call(
        _expert_kernel,
        out_shape=jax.ShapeDtypeStruct((P, D), f32),
        grid_spec=grid_spec,
        compiler_params=_cparams(("arbitrary",)),
        name="moe_experts",
    )(block_e, n_used, buf, w_gate, w_up, w_down)


def _combine_kernel(dest_ref, out_hbm, x_ref, route_ref, gate_ref, o_ref, rows, sems, *, tm, n_steps):
    i = pl.program_id(0)

    def issue(step, slot):
        def body(r, carry):
            t = step * tm + r
            for kk in range(TOP_K):
                d = dest_ref[t * TOP_K + kk]
                pltpu.make_async_copy(out_hbm.at[pl.ds(d, 1)], rows.at[slot, kk, pl.ds(r, 1)], sems.at[slot]).start()
            return carry
        lax.fori_loop(0, tm, body, 0)

    @pl.when(i == 0)
    def _():
        issue(0, 0)

    @pl.when(i + 1 < n_steps)
    def _():
        issue(i + 1, (i + 1) % 2)

    slot = i % 2
    for kk in range(TOP_K):
        pltpu.make_async_copy(out_hbm.at[pl.ds(0, tm)], rows.at[slot, kk], sems.at[slot]).wait()
    route = route_ref[...]
    y = route[:, 2:3] * rows[slot, 0] + route[:, 3:4] * rows[slot, 1]
    o_ref[...] = x_ref[...] + gate_ref[0] * y


def _combine(out_buf, dest, x, route, mod, *, seq, tm=256):
    T, D = x.shape
    n_steps = T // tm
    tpb = seq // tm
    grid_spec = pltpu.PrefetchScalarGridSpec(
        num_scalar_prefetch=1,
        grid=(n_steps,),
        in_specs=[
            pl.BlockSpec(memory_space=pl.ANY),
            pl.BlockSpec((tm, D), lambda i, d: (i, 0)),
            pl.BlockSpec((tm, LANES), lambda i, d: (i, 0)),
            pl.BlockSpec((1, 1, D), lambda i, d: (i // tpb, 0, 2)),
        ],
        out_specs=pl.BlockSpec((tm, D), lambda i, d: (i, 0)),
        scratch_shapes=[pltpu.VMEM((2, TOP_K, tm, D), f32), pltpu.SemaphoreType.DMA((2,))],
    )
    return pl.pallas_call(
        functools.partial(_combine_kernel, tm=tm, n_steps=n_steps),
        out_shape=jax.ShapeDtypeStruct((T, D), f32),
        grid_spec=grid_spec,
        compiler_params=_cparams(("arbitrary",)),
        name="moe_combine",
    )(dest, out_buf, x, route, mod)


def _dispatch_plan(ids, n_tokens):
    N = n_tokens * TOP_K
    flat_ids = ids.reshape(-1)
    order = jnp.argsort(flat_ids).astype(jnp.int32)
    sorted_ids = flat_ids[order]
    counts = jnp.bincount(flat_ids, length=N_EXPERTS).astype(jnp.int32)
    start = jnp.cumsum(counts) - counts
    padded = ((counts + MOE_ROWS - 1) // MOE_ROWS) * MOE_ROWS
    seg_end = jnp.cumsum(padded).astype(jnp.int32)
    pad_start = seg_end - padded
    rank = jnp.arange(N, dtype=jnp.int32) - start[sorted_ids]
    dest_sorted = pad_start[sorted_ids] + rank
    n_blocks = -(-N // MOE_ROWS) + N_EXPERTS
    dest = jnp.zeros((N,), jnp.int32).at[order].set(dest_sorted)
    block_e = jnp.clip(jnp.searchsorted(seg_end, jnp.arange(n_blocks, dtype=jnp.int32) * MOE_ROWS, side='right'),
                       0, N_EXPERTS - 1).astype(jnp.int32)
    n_used = (seg_end[-1:] // MOE_ROWS).astype(jnp.int32)
    return dest, block_e, n_used, n_blocks


def _moe_layer(x, gain, mod, w_router, b_router, w_gate, w_up, w_down, *, seq):
    T, D = x.shape
    h, route = _route(x, gain, mod, w_router, b_router, seq=seq)
    ids = route[:, :TOP_K].astype(jnp.int32)
    dest, block_e, n_used, n_blocks = _dispatch_plan(ids, T)
    buf = _dispatch(h, dest, n_blocks * MOE_ROWS)
    out_buf = _experts(buf, block_e, n_used, w_gate, w_up, w_down)
    return _combine(out_buf, dest, x, route, mod, seq=seq)


def _router_params(w_grp, b_grp, w_exp, b_exp):
    D = w_grp.shape[0]
    w = jnp.zeros((D, LANES), f32).at[:, :N_GROUPS].set(w_grp).at[:, EXP_LANE0:EXP_LANE0 + N_EXPERTS].set(w_exp)
    b = jnp.zeros((1, LANES), f32).at[0, :N_GROUPS].set(b_grp).at[0, EXP_LANE0:EXP_LANE0 + N_EXPERTS].set(b_exp)
    return w, b


def _pad_cols(w, n):
    return jnp.zeros((w.shape[0], n), w.dtype).at[:, :w.shape[1]].set(w)


def _pad_rows(w, n):
    return jnp.zeros((n, w.shape[1]), w.dtype).at[:w.shape[0]].set(w)


def kernel(x, c, norm_g, ada_w, ada_b, rel_bias, attn_w_in, attn_w_o, attn_q_gain, attn_k_gain, attn_sinks, rw_mu, rw_w_rkv, rw_w0, rw_w1, rw_w2, rw_a0, rw_a1, rw_a2, rw_g1, rw_g2, rw_k_k, rw_k_a, rw_r_k, rw_lnx_g, rw_lnx_b, rw_w_o, pool_w, pool_scale, moe_w_grp, moe_b_grp, moe_w_exp, moe_b_exp, moe_w_gate, moe_w_up, moe_w_down):
    B, S, D = x.shape
    T = B * S
    xt = x.reshape(T, D)
    mods = _ada_mods(c, ada_w, ada_b)
    bias = _band_bias_masked(rel_bias)
    for layer in range(DEPTH):
        kind, idx = layer % N_MIXERS, layer // N_MIXERS
        gain = norm_g[layer, 0].reshape(1, D)
        mod = mods[2 * layer]
        if kind == 0:
            qkv = _norm_mm(xt, gain, mod, attn_w_in[idx].astype(bf16), seq=S)
            o = _attention(qkv, bias, attn_q_gain[idx], attn_k_gain[idx], attn_sinks[idx], batch=B, seq=S)
            xt = _mm_res(o, attn_w_o[idx].astype(bf16), xt, mod, seq=S)
        elif kind == 1:
            mu = rw_mu[idx]
            proj = lambda j, w, dt: _norm_mm(xt, gain, mod, w.astype(bf16), seq=S, mu=mu[j].reshape(1, D),
                                             out_dtype=dt)
            r = proj(0, rw_w_rkv[idx, 0], bf16)
            k = proj(1, rw_w_rkv[idx, 1], bf16)
            v = proj(2, rw_w_rkv[idx, 2], bf16)
            lw = proj(3, _pad_cols(rw_w1[idx], LANES), f32)
            la = proj(4, _pad_cols(rw_a1[idx], LANES), f32)
            lg = proj(5, rw_g1[idx], f32)
            yg = _wkv(r, k, v, lw, la, lg, rw_w0[idx], rw_a0[idx],
                      _pad_rows(rw_w2[idx], LANES).astype(bf16), _pad_rows(rw_a2[idx], LANES).astype(bf16),
                      rw_g2[idx].astype(bf16), rw_k_k[idx], rw_k_a[idx], rw_r_k[idx], rw_lnx_g[idx], rw_lnx_b[idx],
                      batch=B, seq=S)
            xt = _mm_res(yg, rw_w_o[idx].astype(bf16), xt, mod, seq=S)
        else:
            xt = _pool_layer(xt, gain, mod, pool_w[idx].astype(bf16), pool_scale[idx], seq=S)
        w_router, b_router = _router_params(moe_w_grp[layer], moe_b_grp[layer], moe_w_exp[layer], moe_b_exp[layer])
        xt = _moe_layer(xt, norm_g[layer, 1].reshape(1, D), mods[2 * layer + 1], w_router, b_router,
                        moe_w_gate[layer].astype(bf16), moe_w_up[layer].astype(bf16), moe_w_down[layer].astype(bf16),
                        seq=S)
    return xt.reshape(B, S, D)
```

```python
import functools
import math

import jax
import jax.numpy as jnp
import numpy as np
from jax import lax
from jax.experimental import pallas as pl
from jax.experimental.pallas import tpu as pltpu

f32 = jnp.float32
bf16 = jnp.bfloat16

D_MODEL = 2048
DEPTH = 4
N_MIXERS = 3
HEAD_DIM = 64
N_HEADS = D_MODEL // HEAD_DIM
N_KV_HEADS = 4
GQA_GROUP = N_HEADS // N_KV_HEADS
WINDOW = 128
Q_DIM = N_HEADS * HEAD_DIM
KV_DIM = N_KV_HEADS * HEAD_DIM
QKV_DIM = Q_DIM + 2 * KV_DIM
N_BUCKETS = 32
MAX_DISTANCE = 128
RWKV_HEAD = 64
GN_EPS = 64e-5
POOL_WINDOWS = (2, 4, 8, 16)
POOL_GROUP = D_MODEL // len(POOL_WINDOWS)
N_GROUPS = 4
EXPERTS_PER_GROUP = 8
N_EXPERTS = N_GROUPS * EXPERTS_PER_GROUP
TOP_K = 2
D_EXPERT = D_MODEL // 4
NORM_EPS = 1e-6

LANES = 128
SUBLANES = 8
VMEM_LIMIT = 56 * 1024 * 1024

MOE_ROWS = 256
WKV_CHUNK = 64
EXP_LANE0 = 32
NEG_BIG = -1e30


def _cparams(sem):
    return pltpu.CompilerParams(dimension_semantics=sem, vmem_limit_bytes=VMEM_LIMIT)


def _ada_norm(xf, gain, shift, scale):
    ms = jnp.mean(xf * xf, axis=-1, keepdims=True)
    return xf * lax.rsqrt(ms + NORM_EPS) * gain * (1.0 + scale) + shift


def _sigmoid(z):
    return 1.0 / (1.0 + jnp.exp(-z))


def _split3(x):
    hi = x.astype(bf16)
    r1 = x - hi.astype(f32)
    mid = r1.astype(bf16)
    lo = (r1 - mid.astype(f32)).astype(bf16)
    return hi, mid, lo


def _dot(a, b):
    return jnp.dot(a, b, preferred_element_type=f32)


def _dot_nt(a, b):
    return lax.dot_general(a, b, (((1,), (1,)), ((), ())), preferred_element_type=f32)


def _dot_tn(a, b):
    return lax.dot_general(a, b, (((0,), (0,)), ((), ())), preferred_element_type=f32)


def _ada_kernel(c_ref, w_ref, b_ref, o_ref):
    c = c_ref[...]
    ca = (c * _sigmoid(c)).astype(bf16)
    o_ref[0] = _dot(ca, w_ref[0].astype(bf16)) + b_ref[0]


def _ada_mods(c, ada_w, ada_b):
    B, D = c.shape
    n_mod = ada_w.shape[0] * ada_w.shape[1]
    N = ada_w.shape[-1]
    tn = 1024
    c_pad = jnp.zeros((SUBLANES, D), f32).at[:B].set(c)
    out = pl.pallas_call(
        _ada_kernel,
        out_shape=jax.ShapeDtypeStruct((n_mod, SUBLANES, N), f32),
        grid=(n_mod, N // tn),
        in_specs=[
            pl.BlockSpec((SUBLANES, D), lambda m, j: (0, 0)),
            pl.BlockSpec((1, D, tn), lambda m, j: (m, 0, j)),
            pl.BlockSpec((1, 1, tn), lambda m, j: (m, 0, j)),
        ],
        out_specs=pl.BlockSpec((1, SUBLANES, tn), lambda m, j: (m, 0, j)),
        compiler_params=_cparams(("parallel", "parallel")),
        name="ada_mods",
    )(c_pad, ada_w.reshape(n_mod, D, N), ada_b.reshape(n_mod, 1, N))
    return out[:, :B].reshape(n_mod, B, 1, N)


def _norm_mm_kernel(x_ref, g_ref, sh_ref, sc_ref, w_ref, o_ref, lhs_ref):
    @pl.when(pl.program_id(1) == 0)
    def _():
        lhs_ref[...] = _ada_norm(x_ref[...], g_ref[...], sh_ref[0], sc_ref[0]).astype(bf16)

    o_ref[...] = _dot(lhs_ref[...], w_ref[...]).astype(o_ref.dtype)


def _norm_mm(x, gain, mod, w, *, seq, tm=512, tn=512):
    T, D = x.shape
    N = w.shape[1]
    tpb = seq // tm
    return pl.pallas_call(
        _norm_mm_kernel,
        out_shape=jax.ShapeDtypeStruct((T, N), bf16),
        grid=(T // tm, N // tn),
        in_specs=[
            pl.BlockSpec((tm, D), lambda i, j: (i, 0)),
            pl.BlockSpec((1, D), lambda i, j: (0, 0)),
            pl.BlockSpec((1, 1, D), lambda i, j: (i // tpb, 0, 0)),
            pl.BlockSpec((1, 1, D), lambda i, j: (i // tpb, 0, 1)),
            pl.BlockSpec((D, tn), lambda i, j: (0, j)),
        ],
        out_specs=pl.BlockSpec((tm, tn), lambda i, j: (i, j)),
        scratch_shapes=[pltpu.VMEM((tm, D), bf16)],
        compiler_params=_cparams(("parallel", "arbitrary")),
        name="norm_mm",
    )(x, gain, mod, mod, w)


RW_MIXES = 6
RW_LORA_TILE = 512


def _rwkv_proj_kernel(x_ref, xh_ref, g_ref, sh_ref, sc_ref, mu_ref, w_ref, o_ref, lhs_ref, *, tiles_per_batch, n_big,
                      chunks_per_mix):
    i = pl.program_id(0)
    j = pl.program_id(1)

    @pl.when(j == 0)
    def _():
        g, sh, sc = g_ref[...], sh_ref[0], sc_ref[0]
        h = _ada_norm(x_ref[...], g, sh, sc)
        h_last = _ada_norm(xh_ref[...], g, sh, sc)[SUBLANES - 1:SUBLANES]
        h_last = jnp.where(i % tiles_per_batch == 0, 0.0, h_last)
        row = lax.broadcasted_iota(jnp.int32, h.shape, 0)
        dh = jnp.where(row == 0, h_last, pltpu.roll(h, 1, axis=0)) - h
        for m in range(RW_MIXES):
            lhs_ref[m] = (h + dh * mu_ref[m:m + 1, :]).astype(bf16)

    @pl.when(j < n_big)
    def _():
        o_ref[...] = _dot(lhs_ref[j // chunks_per_mix], w_ref[...]).astype(o_ref.dtype)

    @pl.when(j == n_big)
    def _():
        q = LANES
        o_ref[...] = jnp.concatenate(
            [_dot(lhs_ref[3], w_ref[:, :q]), _dot(lhs_ref[4], w_ref[:, q:2 * q]), _dot(lhs_ref[5], w_ref[:, 2 * q:])],
            axis=1).astype(o_ref.dtype)


def _rwkv_proj(x, gain, mod, mu, w_all, *, seq, tm=512):
    T, D = x.shape
    tn = RW_LORA_TILE
    n_big = 3 * D // tn
    tpb = seq // tm
    vec_spec = pl.BlockSpec((1, D), lambda i, j: (0, 0))
    return pl.pallas_call(
        functools.partial(_rwkv_proj_kernel, tiles_per_batch=tpb, n_big=n_big, chunks_per_mix=D // tn),
        out_shape=jax.ShapeDtypeStruct((T, w_all.shape[1]), bf16),
        grid=(T // tm, n_big + 1),
        in_specs=[
            pl.BlockSpec((tm, D), lambda i, j: (i, 0)),
            pl.BlockSpec((SUBLANES, D), lambda i, j: (jnp.maximum(i * (tm // SUBLANES) - 1, 0), 0)),
            vec_spec,
            pl.BlockSpec((1, 1, D), lambda i, j: (i // tpb, 0, 0)),
            pl.BlockSpec((1, 1, D), lambda i, j: (i // tpb, 0, 1)),
            pl.BlockSpec((RW_MIXES, D), lambda i, j: (0, 0)),
            pl.BlockSpec((D, tn), lambda i, j: (0, j)),
        ],
        out_specs=pl.BlockSpec((tm, tn), lambda i, j: (i, j)),
        scratch_shapes=[pltpu.VMEM((RW_MIXES, tm, D), bf16)],
        compiler_params=_cparams(("parallel", "arbitrary")),
        name="rwkv_proj",
    )(x, x, gain, mod, mod, mu, w_all)


def _mm_res_kernel(a_ref, w_ref, x_ref, gate_ref, o_ref):
    o_ref[...] = x_ref[...] + gate_ref[0] * _dot(a_ref[...], w_ref[...])


def _mm_res(a, w, x, mod, *, seq, tm=1024, tn=512):
    T, K = a.shape
    N = w.shape[1]
    tpb = seq // tm
    gate_blk = 2 * (N // tn)
    return pl.pallas_call(
        _mm_res_kernel,
        out_shape=jax.ShapeDtypeStruct((T, N), f32),
        grid=(T // tm, N // tn),
        in_specs=[
            pl.BlockSpec((tm, K), lambda i, j: (i, 0)),
            pl.BlockSpec((K, tn), lambda i, j: (0, j)),
            pl.BlockSpec((tm, tn), lambda i, j: (i, j)),
            pl.BlockSpec((1, 1, tn), lambda i, j: (i // tpb, 0, gate_blk + j)),
        ],
        out_specs=pl.BlockSpec((tm, tn), lambda i, j: (i, j)),
        compiler_params=_cparams(("parallel", "parallel")),
        name="mm_res",
    )(a, w, x, mod)


def _attn_kernel(sink_ref, q_ref, kp_ref, kc_ref, vp_ref, vc_ref, bias_ref, qg_ref, kg_ref, o_ref):
    n = pl.program_id(1)
    W = WINDOW
    q_all = q_ref[...].astype(f32)
    k_band = jnp.concatenate([kp_ref[...], kc_ref[...]], axis=0).astype(f32)
    v_band = jnp.concatenate([vp_ref[...], vc_ref[...]], axis=0)
    qg = qg_ref[...]
    kg = kg_ref[...]
    col = lax.broadcasted_iota(jnp.int32, (GQA_GROUP * W, 2 * W), 1)
    key_ok = jnp.logical_or(n > 0, col >= W)
    grp = lax.broadcasted_iota(jnp.int32, (GQA_GROUP * W, 1), 0) // W
    for h in range(N_KV_HEADS):
        k_h = k_band[:, h * HEAD_DIM:(h + 1) * HEAD_DIM]
        k_h = k_h * lax.rsqrt(jnp.mean(k_h * k_h, axis=-1, keepdims=True) + NORM_EPS) * kg
        v_h = v_band[:, h * HEAD_DIM:(h + 1) * HEAD_DIM]
        q_h = jnp.concatenate(
            [q_all[:, (h * GQA_GROUP + g) * HEAD_DIM:(h * GQA_GROUP + g + 1) * HEAD_DIM] for g in range(GQA_GROUP)],
            axis=0)
        q_h = q_h * lax.rsqrt(jnp.mean(q_h * q_h, axis=-1, keepdims=True) + NORM_EPS) * qg
        logits = _dot_nt(q_h.astype(bf16), k_h.astype(bf16)) * (HEAD_DIM ** -0.5)
        logits = logits + bias_ref[h].reshape(GQA_GROUP * W, 2 * W)
        logits = jnp.where(key_ok, logits, NEG_BIG)
        sink = jnp.zeros((GQA_GROUP * W, 1), f32)
        for g in range(GQA_GROUP):
            sink = jnp.where(grp == g, sink_ref[h * GQA_GROUP + g], sink)
        m = jnp.maximum(jnp.max(logits, axis=-1, keepdims=True), sink)
        p = jnp.exp(logits - m)
        denom = jnp.sum(p, axis=-1, keepdims=True) + jnp.exp(sink - m)
        o_h = _dot(p.astype(bf16), v_h) / denom
        for g in range(GQA_GROUP):
            c0 = (h * GQA_GROUP + g) * HEAD_DIM
            o_ref[:, c0:c0 + HEAD_DIM] = o_h[g * W:(g + 1) * W].astype(o_ref.dtype)


def _band_bias_masked(rel_bias):
    max_exact = N_BUCKETS // 2
    i = jnp.arange(WINDOW)[:, None]
    j = jnp.arange(2 * WINDOW)[None, :]
    dist = WINDOW + i - j
    nn = jnp.maximum(dist, 0)
    nf = jnp.maximum(nn, 1).astype(f32)
    large = max_exact + (jnp.log(nf / max_exact) / math.log(MAX_DISTANCE / max_exact)
                         * (N_BUCKETS - max_exact)).astype(jnp.int32)
    large = jnp.minimum(large, N_BUCKETS - 1)
    bucket = jnp.where(nn < max_exact, nn, large)
    b = jnp.transpose(rel_bias[bucket], (2, 0, 1)).astype(f32)
    in_band = (j > i) & (j <= i + WINDOW)
    b = jnp.where(in_band[None], b, NEG_BIG)
    return b.reshape(N_KV_HEADS, GQA_GROUP, WINDOW, 2 * WINDOW)


def _attention(qkv, bias, q_gain, k_gain, sinks, *, batch, seq):
    T = qkv.shape[0]
    nb = seq // WINDOW
    kcol = Q_DIM // KV_DIM
    vcol = kcol + 1

    def cur(col):
        return lambda b, n, s: (b * nb + n, col)

    def prev(col):
        return lambda b, n, s: (b * nb + jnp.maximum(n - 1, 0), col)

    grid_spec = pltpu.PrefetchScalarGridSpec(
        num_scalar_prefetch=1,
        grid=(batch, nb),
        in_specs=[
            pl.BlockSpec((WINDOW, Q_DIM), lambda b, n, s: (b * nb + n, 0)),
            pl.BlockSpec((WINDOW, KV_DIM), prev(kcol)),
            pl.BlockSpec((WINDOW, KV_DIM), cur(kcol)),
            pl.BlockSpec((WINDOW, KV_DIM), prev(vcol)),
            pl.BlockSpec((WINDOW, KV_DIM), cur(vcol)),
            pl.BlockSpec((N_KV_HEADS, GQA_GROUP, WINDOW, 2 * WINDOW), lambda b, n, s: (0, 0, 0, 0)),
            pl.BlockSpec((1, HEAD_DIM), lambda b, n, s: (0, 0)),
            pl.BlockSpec((1, HEAD_DIM), lambda b, n, s: (0, 0)),
        ],
        out_specs=pl.BlockSpec((WINDOW, Q_DIM), lambda b, n, s: (b * nb + n, 0)),
    )
    return pl.pallas_call(
        _attn_kernel,
        out_shape=jax.ShapeDtypeStruct((T, Q_DIM), bf16),
        grid_spec=grid_spec,
        compiler_params=_cparams(("parallel", "parallel")),
        name="swa_attention",
    )(sinks, qkv, qkv, qkv, qkv, qkv, bias, q_gain.reshape(1, HEAD_DIM), k_gain.reshape(1, HEAD_DIM))


def _wkv_kernel(r_ref, k_ref, v_ref, lora_ref, w0_ref, a0_ref, w2_ref, a2_ref, g2_ref,
                kk_ref, ka_ref, rk_ref, lng_ref, lnb_ref, o_ref,
                state_ref, sr, sk, sv, slogw, scum, sa, sg, so, *, n_pairs, unroll):
    L = WKV_CHUNK
    P2 = 2 * L

    @pl.when(pl.program_id(1) == 0)
    def _():
        state_ref[...] = jnp.zeros(state_ref.shape, f32)

    lw = lora_ref[:, :LANES].astype(f32)
    la = lora_ref[:, LANES:2 * LANES]
    lg = lora_ref[:, 2 * LANES:].astype(f32)
    w_lin = w0_ref[...] + _dot(jnp.tanh(lw).astype(bf16), w2_ref[...])
    neg = -w_lin
    softplus = jnp.maximum(neg, 0.0) + jnp.log(1.0 + jnp.exp(-jnp.abs(neg)))
    logw = -jnp.exp(-softplus - 0.5)
    a_sig = _sigmoid(a0_ref[...] + _dot(la, a2_ref[...]))
    gate = _dot(_sigmoid(lg).astype(bf16), g2_ref[...])
    ti = lax.broadcasted_iota(jnp.int32, (L, L), 0)
    tj = lax.broadcasted_iota(jnp.int32, (L, L), 1)
    tri = jnp.where(ti >= tj, 1.0, 0.0).astype(bf16)
    hi, mid, lo = _split3(logw)
    cum = _dot(tri, hi) + _dot(tri, mid) + _dot(tri, lo)
    for p in range(n_pairs):
        cs = slice(p * LANES, (p + 1) * LANES)
        sr[p] = r_ref[:, cs].astype(f32)
        sk[p] = k_ref[:, cs].astype(f32)
        sv[p] = v_ref[:, cs].astype(f32)
        slogw[p] = logw[:, cs]
        scum[p] = cum[:, cs]
        sa[p] = a_sig[:, cs]
        sg[p] = gate[:, cs]

    lane = lax.broadcasted_iota(jnp.int32, (1, LANES), 1)
    m0 = lane < RWKV_HEAD
    ri = lax.broadcasted_iota(jnp.int32, (P2, P2), 0)
    ci = lax.broadcasted_iota(jnp.int32, (P2, P2), 1)
    same = (ri // L) == (ci // L)
    strict = jnp.logical_and(same, ri > ci)
    incl = jnp.logical_and(same, ri >= ci)
    eye = jnp.where(ri == ci, 1.0, 0.0)

    def seg_sum(x):
        s0 = jnp.sum(jnp.where(m0, x, 0.0), axis=-1, keepdims=True)
        s1 = jnp.sum(jnp.where(m0, 0.0, x), axis=-1, keepdims=True)
        return jnp.where(m0, s0, s1)

    def bd(x):
        return jnp.concatenate([jnp.where(m0, x, 0.0), jnp.where(m0, 0.0, x)], axis=0)

    def mm(a, b):
        return _dot(a.astype(bf16), b.astype(bf16))

    def pair_body(p, carry):
        r, k, v = sr[p], sk[p], sv[p]
        lw_, cm, a = slogw[p], scum[p], sa[p]
        kk = k * kk_ref[p]
        kk = kk / jnp.maximum(jnp.sqrt(seg_sum(kk * kk)), 1e-12)
        kmod = k * (1.0 + (a - 1.0) * ka_ref[p])
        e_pos = jnp.exp(cm)
        e_neg = jnp.exp(-cm)
        e_exc = jnp.exp(cm - lw_)
        r_t = bd(r * e_pos)
        a_t = bd(-kk * e_exc)
        b_t = bd(kk * a * e_neg)
        k_t = bd(kmod * e_neg)
        v_b = bd(v)
        G = _dot_nt(jnp.concatenate([a_t, r_t], axis=0).astype(bf16),
                    jnp.concatenate([b_t, k_t], axis=0).astype(bf16))
        A_ab = jnp.where(strict, G[:P2, :P2], 0.0)
        A_ak = jnp.where(strict, G[:P2, P2:], 0.0)
        A_rb = jnp.where(incl, G[P2:, :P2], 0.0)
        A_rk = jnp.where(incl, G[P2:, P2:], 0.0)
        X = eye + A_ab
        Pw = mm(A_ab, A_ab)
        for _ in range(int(math.log2(L)) - 2):
            Z = mm(jnp.concatenate([X, Pw], axis=0), Pw)
            X = X + Z[:P2]
            Pw = Z[P2:]
        X = X + mm(X, Pw)
        AU = mm(X, jnp.concatenate([a_t, mm(A_ak, v_b)], axis=1))
        RY = mm(A_rb, AU)
        R_hat = r_t + RY[:, :LANES]
        Y_hat = RY[:, LANES:] + mm(A_rk, v_b)
        A_hat = AU[:, :LANES]
        U_hat = AU[:, LANES:]
        M_k = _dot_tn(A_hat.astype(bf16), b_t.astype(bf16))
        N_s = _dot_tn(jnp.concatenate([U_hat, v_b], axis=0).astype(bf16),
                      jnp.concatenate([b_t, k_t], axis=0).astype(bf16))
        S = state_ref[p]
        Y = _dot_nt(R_hat.astype(bf16), S.astype(bf16)) + Y_hat
        g_last = e_pos[L - 1:L, :]
        state_ref[p] = (S + mm(S, M_k) + N_s) * g_last
        y = Y[:L] + Y[L:]
        mean = seg_sum(y) * (1.0 / RWKV_HEAD)
        yc = y - mean
        var = seg_sum(yc * yc) * (1.0 / RWKV_HEAD)
        yn = yc * lax.rsqrt(var + GN_EPS) * lng_ref[p] + lnb_ref[p]
        yn = yn + seg_sum(r * kmod * rk_ref[p]) * v
        so[p] = (yn * sg[p]).astype(so.dtype)
        return carry

    lax.fori_loop(0, n_pairs, pair_body, 0, unroll=unroll)
    for p in range(n_pairs):
        o_ref[:, p * LANES:(p + 1) * LANES] = so[p]


def _wkv(proj, w0, a0, w2, a2, g2, k_k, k_a, r_k, lnx_g, lnx_b, *, batch, seq, unroll=8):
    T = proj.shape[0]
    D = w0.shape[0]
    L = WKV_CHUNK
    n_pairs = D // LANES
    nc = seq // L

    def row(width, col):
        return pl.BlockSpec((L, width), lambda b, t: (b * nc + t, col))

    def full(shape):
        return pl.BlockSpec(shape, lambda b, t: (0,) * len(shape))

    pair_vec = lambda a: a.reshape(n_pairs, 1, LANES).astype(f32)
    pv_spec = full((n_pairs, 1, LANES))
    pair_scr = pltpu.VMEM((n_pairs, L, LANES), f32)
    return pl.pallas_call(
        functools.partial(_wkv_kernel, n_pairs=n_pairs, unroll=unroll),
        out_shape=jax.ShapeDtypeStruct((T, D), bf16),
        grid=(batch, nc),
        in_specs=[row(D, 0), row(D, 1), row(D, 2), row(RW_LORA_TILE, 3 * D // RW_LORA_TILE),
                  full((1, D)), full((1, D)), full(w2.shape), full(a2.shape), full(g2.shape),
                  pv_spec, pv_spec, pv_spec, pv_spec, pv_spec],
        out_specs=row(D, 0),
        scratch_shapes=[pltpu.VMEM((n_pairs, 2 * RWKV_HEAD, 2 * RWKV_HEAD), f32)] + [pair_scr] * 7
                       + [pltpu.VMEM((n_pairs, L, LANES), bf16)],
        compiler_params=_cparams(("parallel", "arbitrary")),
        name="wkv7",
    )(proj, proj, proj, proj, w0.reshape(1, D), a0.reshape(1, D), w2, a2, g2,
      pair_vec(k_k), pair_vec(k_a), pair_vec(r_k), pair_vec(lnx_g), pair_vec(lnx_b))


def _pool_kernel(x_ref, xh_ref, g_ref, sh_ref, sc_ref, gate_ref, w_ref, ps_ref, o_ref, *, tiles_per_batch, tm):
    i = pl.program_id(0)
    halo = 2 * SUBLANES
    g, sh, sc = g_ref[...], sh_ref[0], sc_ref[0]
    x = x_ref[...]
    h = _ada_norm(x, g, sh, sc)
    hh = _ada_norm(xh_ref[...], g, sh, sc)
    hh = jnp.where(i % tiles_per_batch == 0, 0.0, hh)
    ext = jnp.concatenate([hh, h], axis=0)
    pos = (i % tiles_per_batch) * tm + lax.broadcasted_iota(jnp.int32, (tm, 1), 0)
    for gi, w in enumerate(POOL_WINDOWS):
        cs = slice(gi * POOL_GROUP, (gi + 1) * POOL_GROUP)
        s = ext[:, cs]
        d = 1
        while d < w:
            s = s + pltpu.roll(s, d, axis=0)
            d *= 2
        cnt = jnp.minimum(pos + 1, w).astype(f32)
        pooled = s[halo:] / cnt - h[:, cs]
        mixed = _dot(pooled.astype(bf16), w_ref[gi]) * ps_ref[:, cs]
        o_ref[:, cs] = x[:, cs] + gate_ref[0][:, cs] * mixed


def _pool_layer(x, gain, mod, pool_w, pool_scale, *, seq, tm=512):
    T, D = x.shape
    tpb = seq // tm
    halo = 2 * SUBLANES
    vec = pl.BlockSpec((1, D), lambda i: (0, 0))
    return pl.pallas_call(
        functools.partial(_pool_kernel, tiles_per_batch=tpb, tm=tm),
        out_shape=jax.ShapeDtypeStruct((T, D), f32),
        grid=(T // tm,),
        in_specs=[
            pl.BlockSpec((tm, D), lambda i: (i, 0)),
            pl.BlockSpec((halo, D), lambda i: (jnp.maximum(i * (tm // halo) - 1, 0), 0)),
            vec,
            pl.BlockSpec((1, 1, D), lambda i: (i // tpb, 0, 0)),
            pl.BlockSpec((1, 1, D), lambda i: (i // tpb, 0, 1)),
            pl.BlockSpec((1, 1, D), lambda i: (i // tpb, 0, 2)),
            pl.BlockSpec(pool_w.shape, lambda i: (0, 0, 0)),
            vec,
        ],
        out_specs=pl.BlockSpec((tm, D), lambda i: (i, 0)),
        compiler_params=_cparams(("parallel",)),
        name="pool_mixer",
    )(x, x, gain, mod, mod, mod, pool_w, pool_scale.reshape(1, D))


def _route_kernel(x_ref, g_ref, sh_ref, sc_ref, w_ref, b_ref, h_ref, route_ref):
    h = _ada_norm(x_ref[...], g_ref[...], sh_ref[0], sc_ref[0])
    h_ref[...] = h
    h_hi, h_mid, _ = _split3(h)
    w_hi, w_mid, _ = _split3(w_ref[...])
    logits = _dot(h_hi, w_hi) + (_dot(h_hi, w_mid) + _dot(h_mid, w_hi)) + b_ref[...]
    lane = lax.broadcasted_iota(jnp.int32, logits.shape, 1)
    big = jnp.int32(LANES)

    def masked_argmax(mask):
        mx = jnp.max(jnp.where(mask, logits, NEG_BIG), axis=-1, keepdims=True)
        idx = jnp.min(jnp.where(jnp.logical_and(mask, logits == mx), lane, big), axis=-1, keepdims=True)
        return mx, idx

    gmask = lane < N_GROUPS
    gmax, gidx = masked_argmax(gmask)
    grp_w = 1.0 / jnp.sum(jnp.where(gmask, jnp.exp(logits - gmax), 0.0), axis=-1, keepdims=True)
    e_lo = EXP_LANE0 + gidx * EXPERTS_PER_GROUP
    emask = jnp.logical_and(lane >= e_lo, lane < e_lo + EXPERTS_PER_GROUP)
    m1, i1 = masked_argmax(emask)
    m2, i2 = masked_argmax(jnp.logical_and(emask, lane != i1))
    e21 = jnp.exp(m2 - m1)
    w1 = grp_w / (1.0 + e21)
    w2 = grp_w * e21 / (1.0 + e21)
    out = jnp.where(lane == 0, (i1 - EXP_LANE0).astype(f32), 0.0)
    out = jnp.where(lane == 1, (i2 - EXP_LANE0).astype(f32), out)
    out = jnp.where(lane == 2, w1, out)
    out = jnp.where(lane == 3, w2, out)
    route_ref[...] = out


def _route(x, gain, mod, w_router, b_router, *, seq, tm=512):
    T, D = x.shape
    tpb = seq // tm
    vec = pl.BlockSpec((1, D), lambda i: (0, 0))
    return pl.pallas_call(
        _route_kernel,
        out_shape=(jax.ShapeDtypeStruct((T, D), f32), jax.ShapeDtypeStruct((T, LANES), f32)),
        grid=(T // tm,),
        in_specs=[
            pl.BlockSpec((tm, D), lambda i: (i, 0)),
            vec,
            pl.BlockSpec((1, 1, D), lambda i: (i // tpb, 0, 0)),
            pl.BlockSpec((1, 1, D), lambda i: (i // tpb, 0, 1)),
            pl.BlockSpec((D, LANES), lambda i: (0, 0)),
            pl.BlockSpec((1, LANES), lambda i: (0, 0)),
        ],
        out_specs=(pl.BlockSpec((tm, D), lambda i: (i, 0)), pl.BlockSpec((tm, LANES), lambda i: (i, 0))),
        compiler_params=_cparams(("parallel",)),
        name="moe_route",
    )(x, gain, mod, mod, w_router, b_router)


def _expert_kernel(be_ref, nu_ref, tok_ref, h_hbm, wg_ref, wu_ref, wd_ref, o_ref, xbuf, sems, wg_s, wu_s, wd_s):
    i = pl.program_id(0)
    n_used = nu_ref[0]

    def issue(blk, slot):
        def body(r, carry):
            tok = tok_ref[blk * MOE_ROWS + r]
            pltpu.make_async_copy(h_hbm.at[pl.ds(tok, 1)], xbuf.at[slot, pl.ds(r, 1)], sems.at[slot]).start()
            return carry
        lax.fori_loop(0, MOE_ROWS, body, 0, unroll=8)

    @pl.when(i == 0)
    def _():
        issue(0, 0)

    @pl.when(i + 1 < n_used)
    def _():
        issue(i + 1, (i + 1) % 2)

    used = i < n_used

    @pl.when(used)
    def _():
        @pl.when(jnp.logical_or(i == 0, be_ref[i] != be_ref[jnp.maximum(i - 1, 0)]))
        def _():
            wg_s[...] = wg_ref[0].astype(bf16)
            wu_s[...] = wu_ref[0].astype(bf16)
            wd_s[...] = wd_ref[0].astype(bf16)

        slot = i % 2
        pltpu.make_async_copy(h_hbm.at[pl.ds(0, MOE_ROWS)], xbuf.at[slot], sems.at[slot]).wait()
        xb = xbuf[slot].astype(bf16)
        gate = _dot(xb, wg_s[...])
        up = _dot(xb, wu_s[...])
        hid = (gate * _sigmoid(gate) * up).astype(bf16)
        o_ref[...] = _dot(hid, wd_s[...])

    @pl.when(jnp.logical_not(used))
    def _():
        o_ref[...] = jnp.zeros(o_ref.shape, o_ref.dtype)


def _experts(h, slot_tok, block_e, n_used, w_gate, w_up, w_down):
    T, D = h.shape
    n_blocks = block_e.shape[0]
    DE = w_gate.shape[-1]

    def wsel(i, be, nu, tok):
        return (be[jnp.minimum(i, nu[0] - 1)], 0, 0)

    grid_spec = pltpu.PrefetchScalarGridSpec(
        num_scalar_prefetch=3,
        grid=(n_blocks,),
        in_specs=[
            pl.BlockSpec(memory_space=pl.ANY),
            pl.BlockSpec((1, D, DE), wsel),
            pl.BlockSpec((1, D, DE), wsel),
            pl.BlockSpec((1, DE, D), wsel),
        ],
        out_specs=pl.BlockSpec((MOE_ROWS, D), lambda i, be, nu, tok: (i, 0)),
        scratch_shapes=[pltpu.VMEM((2, MOE_ROWS, D), f32), pltpu.SemaphoreType.DMA((2,)),
                        pltpu.VMEM((D, DE), bf16), pltpu.VMEM((D, DE), bf16), pltpu.VMEM((DE, D), bf16)],
    )
    return pl.pallas_call(
        _expert_kernel,
        out_shape=jax.ShapeDtypeStruct((n_blocks * MOE_ROWS, D), f32),
        grid_spec=grid_spec,
        compiler_params=_cparams(("arbitrary",)),
        name="moe_experts",
    )(block_e, n_used, slot_tok, h, w_gate, w_up, w_down)


def _combine_kernel(dest_ref, out_hbm, x_ref, route_ref, gate_ref, o_ref, rows, sems, *, tm, n_steps):
    i = pl.program_id(0)

    def issue(step, slot):
        def body(r, carry):
            t = step * tm + r
            for kk in range(TOP_K):
                d = dest_ref[t * TOP_K + kk]
                pltpu.make_async_copy(out_hbm.at[pl.ds(d, 1)], rows.at[slot, kk, pl.ds(r, 1)], sems.at[slot]).start()
            return carry
        lax.fori_loop(0, tm, body, 0, unroll=4)

    @pl.when(i == 0)
    def _():
        issue(0, 0)

    @pl.when(i + 1 < n_steps)
    def _():
        issue(i + 1, (i + 1) % 2)

    slot = i % 2
    for kk in range(TOP_K):
        pltpu.make_async_copy(out_hbm.at[pl.ds(0, tm)], rows.at[slot, kk], sems.at[slot]).wait()
    route = route_ref[...]
    y = route[:, 2:3] * rows[slot, 0] + route[:, 3:4] * rows[slot, 1]
    o_ref[...] = x_ref[...] + gate_ref[0] * y


def _combine(out_buf, dest, x, route, mod, *, seq, tm=256):
    T, D = x.shape
    n_steps = T // tm
    tpb = seq // tm
    grid_spec = pltpu.PrefetchScalarGridSpec(
        num_scalar_prefetch=1,
        grid=(n_steps,),
        in_specs=[
            pl.BlockSpec(memory_space=pl.ANY),
            pl.BlockSpec((tm, D), lambda i, d: (i, 0)),
            pl.BlockSpec((tm, LANES), lambda i, d: (i, 0)),
            pl.BlockSpec((1, 1, D), lambda i, d: (i // tpb, 0, 2)),
        ],
        out_specs=pl.BlockSpec((tm, D), lambda i, d: (i, 0)),
        scratch_shapes=[pltpu.VMEM((2, TOP_K, tm, D), f32), pltpu.SemaphoreType.DMA((2,))],
    )
    return pl.pallas_call(
        functools.partial(_combine_kernel, tm=tm, n_steps=n_steps),
        out_shape=jax.ShapeDtypeStruct((T, D), f32),
        grid_spec=grid_spec,
        compiler_params=_cparams(("arbitrary",)),
        name="moe_combine",
    )(dest, out_buf, x, route, mod)


def _dispatch_plan(ids, n_tokens):
    N = n_tokens * TOP_K
    i32 = jnp.int32
    flat_ids = ids.reshape(-1)
    order = jnp.argsort(flat_ids).astype(i32)
    inv_order = jnp.argsort(order).astype(i32)
    experts = jnp.arange(N_EXPERTS, dtype=i32)
    counts = jnp.sum((flat_ids[:, None] == experts[None, :]).astype(i32), axis=0)
    start = jnp.cumsum(counts) - counts
    padded = ((counts + MOE_ROWS - 1) // MOE_ROWS) * MOE_ROWS
    seg_end = jnp.cumsum(padded).astype(i32)
    pad_start = seg_end - padded
    dest = (pad_start - start)[flat_ids] + inv_order
    n_blocks = -(-N // MOE_ROWS) + N_EXPERTS
    blk_row0 = jnp.arange(n_blocks, dtype=i32) * MOE_ROWS
    block_e = jnp.minimum(jnp.sum((blk_row0[:, None] >= seg_end[None, :]).astype(i32), axis=1), N_EXPERTS - 1)
    n_used = (seg_end[-1:] // MOE_ROWS).astype(i32)
    row = jnp.arange(n_blocks * MOE_ROWS, dtype=i32)
    row_e = jnp.repeat(block_e, MOE_ROWS)
    j = row - pad_start[row_e]
    src = order[jnp.clip(start[row_e] + j, 0, N - 1)] // TOP_K
    slot_tok = jnp.where(j < counts[row_e], src, 0).astype(i32)
    return dest.astype(i32), slot_tok, block_e.astype(i32), n_used


def _moe_layer(x, gain, mod, w_router, b_router, w_gate, w_up, w_down, *, seq):
    T, D = x.shape
    h, route = _route(x, gain, mod, w_router, b_router, seq=seq)
    ids = route[:, :TOP_K].astype(jnp.int32)
    dest, slot_tok, block_e, n_used = _dispatch_plan(ids, T)
    out_buf = _experts(h, slot_tok, block_e, n_used, w_gate, w_up, w_down)
    return _combine(out_buf, dest, x, route, mod, seq=seq)


def _router_params(w_grp, b_grp, w_exp, b_exp):
    D = w_grp.shape[0]
    w = jnp.zeros((D, LANES), f32).at[:, :N_GROUPS].set(w_grp).at[:, EXP_LANE0:EXP_LANE0 + N_EXPERTS].set(w_exp)
    b = jnp.zeros((1, LANES), f32).at[0, :N_GROUPS].set(b_grp).at[0, EXP_LANE0:EXP_LANE0 + N_EXPERTS].set(b_exp)
    return w, b


def _pad_cols(w, n):
    return jnp.zeros((w.shape[0], n), w.dtype).at[:, :w.shape[1]].set(w)


def _pad_rows(w, n):
    return jnp.zeros((n, w.shape[1]), w.dtype).at[:w.shape[0]].set(w)


def kernel(x, c, norm_g, ada_w, ada_b, rel_bias, attn_w_in, attn_w_o, attn_q_gain, attn_k_gain, attn_sinks, rw_mu, rw_w_rkv, rw_w0, rw_w1, rw_w2, rw_a0, rw_a1, rw_a2, rw_g1, rw_g2, rw_k_k, rw_k_a, rw_r_k, rw_lnx_g, rw_lnx_b, rw_w_o, pool_w, pool_scale, moe_w_grp, moe_b_grp, moe_w_exp, moe_b_exp, moe_w_gate, moe_w_up, moe_w_down):
    B, S, D = x.shape
    T = B * S
    xt = x.reshape(T, D)
    mods = _ada_mods(c, ada_w, ada_b)
    bias = _band_bias_masked(rel_bias)
    for layer in range(DEPTH):
        kind, idx = layer % N_MIXERS, layer // N_MIXERS
        gain = norm_g[layer, 0].reshape(1, D)
        mod = mods[2 * layer]
        if kind == 0:
            qkv = _norm_mm(xt, gain, mod, attn_w_in[idx].astype(bf16), seq=S)
            o = _attention(qkv, bias, attn_q_gain[idx], attn_k_gain[idx], attn_sinks[idx], batch=B, seq=S)
            xt = _mm_res(o, attn_w_o[idx].astype(bf16), xt, mod, seq=S)
        elif kind == 1:
            w_all = jnp.concatenate(
                [rw_w_rkv[idx, 0], rw_w_rkv[idx, 1], rw_w_rkv[idx, 2], _pad_cols(rw_w1[idx], LANES),
                 _pad_cols(rw_a1[idx], LANES), rw_g1[idx]], axis=1).astype(bf16)
            proj = _rwkv_proj(xt, gain, mod, rw_mu[idx], w_all, seq=S)
            yg = _wkv(proj, rw_w0[idx], rw_a0[idx],
                      _pad_rows(rw_w2[idx], LANES).astype(bf16), _pad_rows(rw_a2[idx], LANES).astype(bf16),
                      rw_g2[idx].astype(bf16), rw_k_k[idx], rw_k_a[idx], rw_r_k[idx], rw_lnx_g[idx], rw_lnx_b[idx],
                      batch=B, seq=S)
            xt = _mm_res(yg, rw_w_o[idx].astype(bf16), xt, mod, seq=S)
        else:
            xt = _pool_layer(xt, gain, mod, pool_w[idx].astype(bf16), pool_scale[idx], seq=S)
        w_router, b_router = _router_params(moe_w_grp[layer], moe_b_grp[layer], moe_w_exp[layer], moe_b_exp[layer])
        xt = _moe_layer(xt, norm_g[layer, 1].reshape(1, D), mods[2 * layer + 1], w_router, b_router,
                        moe_w_gate[layer], moe_w_up[layer], moe_w_down[layer], seq=S)
    return xt.reshape(B, S, D)
```

```python
import functools
import math

import jax
import jax.numpy as jnp
import numpy as np
from jax import lax
from jax.experimental import pallas as pl
from jax.experimental.pallas import tpu as pltpu

f32 = jnp.float32
bf16 = jnp.bfloat16

D_MODEL = 2048
DEPTH = 4
N_MIXERS = 3
HEAD_DIM = 64
N_HEADS = D_MODEL // HEAD_DIM
N_KV_HEADS = 4
GQA_GROUP = N_HEADS // N_KV_HEADS
WINDOW = 128
Q_DIM = N_HEADS * HEAD_DIM
KV_DIM = N_KV_HEADS * HEAD_DIM
QKV_DIM = Q_DIM + 2 * KV_DIM
N_BUCKETS = 32
MAX_DISTANCE = 128
RWKV_HEAD = 64
GN_EPS = 64e-5
POOL_WINDOWS = (2, 4, 8, 16)
POOL_GROUP = D_MODEL // len(POOL_WINDOWS)
N_GROUPS = 4
EXPERTS_PER_GROUP = 8
N_EXPERTS = N_GROUPS * EXPERTS_PER_GROUP
TOP_K = 2
D_EXPERT = D_MODEL // 4
NORM_EPS = 1e-6

LANES = 128
SUBLANES = 8
VMEM_LIMIT = 56 * 1024 * 1024

MOE_ROWS = 256
WKV_CHUNK = 64
EXP_LANE0 = 32
NEG_BIG = -1e30


def _cparams(sem):
    return pltpu.CompilerParams(dimension_semantics=sem, vmem_limit_bytes=VMEM_LIMIT)


def _ada_norm(xf, gain, shift, scale):
    ms = jnp.mean(xf * xf, axis=-1, keepdims=True)
    return xf * lax.rsqrt(ms + NORM_EPS) * gain * (1.0 + scale) + shift


def _sigmoid(z):
    return 1.0 / (1.0 + jnp.exp(-z))


def _split3(x):
    hi = x.astype(bf16)
    r1 = x - hi.astype(f32)
    mid = r1.astype(bf16)
    lo = (r1 - mid.astype(f32)).astype(bf16)
    return hi, mid, lo


def _dot(a, b):
    return jnp.dot(a, b, preferred_element_type=f32)


def _dot_nt(a, b):
    return lax.dot_general(a, b, (((1,), (1,)), ((), ())), preferred_element_type=f32)


def _dot_tn(a, b):
    return lax.dot_general(a, b, (((0,), (0,)), ((), ())), preferred_element_type=f32)


def _ada_kernel(c_ref, w_ref, b_ref, o_ref):
    c = c_ref[...]
    ca = (c * _sigmoid(c)).astype(bf16)
    o_ref[0] = _dot(ca, w_ref[0].astype(bf16)) + b_ref[0]


def _ada_mods(c, ada_w, ada_b):
    B, D = c.shape
    n_mod = ada_w.shape[0] * ada_w.shape[1]
    N = ada_w.shape[-1]
    tn = 1024
    c_pad = jnp.zeros((SUBLANES, D), f32).at[:B].set(c)
    out = pl.pallas_call(
        _ada_kernel,
        out_shape=jax.ShapeDtypeStruct((n_mod, SUBLANES, N), f32),
        grid=(n_mod, N // tn),
        in_specs=[
            pl.BlockSpec((SUBLANES, D), lambda m, j: (0, 0)),
            pl.BlockSpec((1, D, tn), lambda m, j: (m, 0, j)),
            pl.BlockSpec((1, 1, tn), lambda m, j: (m, 0, j)),
        ],
        out_specs=pl.BlockSpec((1, SUBLANES, tn), lambda m, j: (m, 0, j)),
        compiler_params=_cparams(("parallel", "parallel")),
        name="ada_mods",
    )(c_pad, ada_w.reshape(n_mod, D, N), ada_b.reshape(n_mod, 1, N))
    return out[:, :B].reshape(n_mod, B, 1, N)


def _norm_mm_kernel(x_ref, g_ref, sh_ref, sc_ref, w_ref, o_ref, lhs_ref):
    @pl.when(pl.program_id(1) == 0)
    def _():
        lhs_ref[...] = _ada_norm(x_ref[...], g_ref[...], sh_ref[0], sc_ref[0]).astype(bf16)

    o_ref[...] = _dot(lhs_ref[...], w_ref[...]).astype(o_ref.dtype)


def _norm_mm(x, gain, mod, w, *, seq, tm=512, tn=1280):
    T, D = x.shape
    N = w.shape[1]
    tpb = seq // tm
    return pl.pallas_call(
        _norm_mm_kernel,
        out_shape=jax.ShapeDtypeStruct((T, N), bf16),
        grid=(T // tm, N // tn),
        in_specs=[
            pl.BlockSpec((tm, D), lambda i, j: (i, 0)),
            pl.BlockSpec((1, D), lambda i, j: (0, 0)),
            pl.BlockSpec((1, 1, D), lambda i, j: (i // tpb, 0, 0)),
            pl.BlockSpec((1, 1, D), lambda i, j: (i // tpb, 0, 1)),
            pl.BlockSpec((D, tn), lambda i, j: (0, j)),
        ],
        out_specs=pl.BlockSpec((tm, tn), lambda i, j: (i, j)),
        scratch_shapes=[pltpu.VMEM((tm, D), bf16)],
        compiler_params=_cparams(("parallel", "arbitrary")),
        name="norm_mm",
    )(x, gain, mod, mod, w)


RW_MIXES = 6
RW_LORA_TILE = 512
RW_PROJ_TILE = 1024


def _rwkv_proj_kernel(x_ref, xh_ref, g_ref, sh_ref, sc_ref, mu_ref, w_ref, o_ref, lhs_ref, *, tiles_per_batch, n_big,
                      chunks_per_mix):
    i = pl.program_id(0)
    j = pl.program_id(1)

    @pl.when(j == 0)
    def _():
        g, sh, sc = g_ref[...], sh_ref[0], sc_ref[0]
        h = _ada_norm(x_ref[...], g, sh, sc)
        h_last = _ada_norm(xh_ref[...], g, sh, sc)[SUBLANES - 1:SUBLANES]
        h_last = jnp.where(i % tiles_per_batch == 0, 0.0, h_last)
        row = lax.broadcasted_iota(jnp.int32, h.shape, 0)
        dh = jnp.where(row == 0, h_last, pltpu.roll(h, 1, axis=0)) - h
        for m in range(RW_MIXES):
            lhs_ref[m] = (h + dh * mu_ref[m:m + 1, :]).astype(bf16)

    @pl.when(j < n_big)
    def _():
        o_ref[...] = _dot(lhs_ref[j // chunks_per_mix], w_ref[...]).astype(o_ref.dtype)

    @pl.when(j == n_big)
    def _():
        q = LANES
        o_ref[:, :RW_LORA_TILE] = jnp.concatenate(
            [_dot(lhs_ref[3], w_ref[:, :q]), _dot(lhs_ref[4], w_ref[:, q:2 * q]),
             _dot(lhs_ref[5], w_ref[:, 2 * q:RW_LORA_TILE])], axis=1).astype(o_ref.dtype)
        o_ref[:, RW_LORA_TILE:] = jnp.zeros((o_ref.shape[0], o_ref.shape[1] - RW_LORA_TILE), o_ref.dtype)


def _rwkv_proj(x, gain, mod, mu, w_all, *, seq, tm=512, tn=RW_PROJ_TILE):
    T, D = x.shape
    n_big = 3 * D // tn
    tpb = seq // tm
    vec_spec = pl.BlockSpec((1, D), lambda i, j: (0, 0))
    return pl.pallas_call(
        functools.partial(_rwkv_proj_kernel, tiles_per_batch=tpb, n_big=n_big, chunks_per_mix=D // tn),
        out_shape=jax.ShapeDtypeStruct((T, w_all.shape[1]), bf16),
        grid=(T // tm, n_big + 1),
        in_specs=[
            pl.BlockSpec((tm, D), lambda i, j: (i, 0)),
            pl.BlockSpec((SUBLANES, D), lambda i, j: (jnp.maximum(i * (tm // SUBLANES) - 1, 0), 0)),
            vec_spec,
            pl.BlockSpec((1, 1, D), lambda i, j: (i // tpb, 0, 0)),
            pl.BlockSpec((1, 1, D), lambda i, j: (i // tpb, 0, 1)),
            pl.BlockSpec((RW_MIXES, D), lambda i, j: (0, 0)),
            pl.BlockSpec((D, tn), lambda i, j: (0, j)),
        ],
        out_specs=pl.BlockSpec((tm, tn), lambda i, j: (i, j)),
        scratch_shapes=[pltpu.VMEM((RW_MIXES, tm, D), bf16)],
        compiler_params=_cparams(("parallel", "arbitrary")),
        name="rwkv_proj",
    )(x, x, gain, mod, mod, mu, w_all)


def _mm_res_kernel(a_ref, w_ref, x_ref, gate_ref, o_ref):
    o_ref[...] = x_ref[...] + gate_ref[0] * _dot(a_ref[...], w_ref[...])


def _mm_res(a, w, x, mod, *, seq, tm=1024, tn=1024):
    T, K = a.shape
    N = w.shape[1]
    tpb = seq // tm
    gate_blk = 2 * (N // tn)
    return pl.pallas_call(
        _mm_res_kernel,
        out_shape=jax.ShapeDtypeStruct((T, N), f32),
        grid=(T // tm, N // tn),
        in_specs=[
            pl.BlockSpec((tm, K), lambda i, j: (i, 0)),
            pl.BlockSpec((K, tn), lambda i, j: (0, j)),
            pl.BlockSpec((tm, tn), lambda i, j: (i, j)),
            pl.BlockSpec((1, 1, tn), lambda i, j: (i // tpb, 0, gate_blk + j)),
        ],
        out_specs=pl.BlockSpec((tm, tn), lambda i, j: (i, j)),
        compiler_params=_cparams(("parallel", "parallel")),
        name="mm_res",
    )(a, w, x, mod)


def _attn_kernel(sink_ref, q_ref, kp_ref, kc_ref, vp_ref, vc_ref, bias_ref, qg_ref, kg_ref, o_ref):
    n = pl.program_id(1)
    W = WINDOW
    GW = GQA_GROUP * W
    q_all = q_ref[...].astype(f32)
    k_band = jnp.concatenate([kp_ref[...], kc_ref[...]], axis=0).astype(f32)
    v_band = jnp.concatenate([vp_ref[...], vc_ref[...]], axis=0)
    qg = qg_ref[...] * (HEAD_DIM ** -0.5)
    kg = kg_ref[...]
    col = lax.broadcasted_iota(jnp.int32, (GW, 2 * W), 1)
    key_ok = jnp.logical_or(n > 0, col >= W)
    grp = lax.broadcasted_iota(jnp.int32, (GW, 1), 0) // W
    for h in range(N_KV_HEADS):
        k_h = k_band[:, h * HEAD_DIM:(h + 1) * HEAD_DIM]
        k_h = k_h * lax.rsqrt(jnp.mean(k_h * k_h, axis=-1, keepdims=True) + NORM_EPS) * kg
        v_h = v_band[:, h * HEAD_DIM:(h + 1) * HEAD_DIM]
        q_h = jnp.concatenate(
            [q_all[:, (h * GQA_GROUP + g) * HEAD_DIM:(h * GQA_GROUP + g + 1) * HEAD_DIM] for g in range(GQA_GROUP)],
            axis=0)
        q_h = q_h * lax.rsqrt(jnp.mean(q_h * q_h, axis=-1, keepdims=True) + NORM_EPS) * qg
        logits = _dot_nt(q_h.astype(bf16), k_h.astype(bf16))
        logits = logits + bias_ref[h].reshape(GW, 2 * W)
        logits = jnp.where(key_ok, logits, NEG_BIG)
        sink = jnp.zeros((GW, 1), f32)
        for g in range(GQA_GROUP):
            sink = jnp.where(grp == g, sink_ref[h * GQA_GROUP + g], sink)
        m = jnp.maximum(jnp.max(logits, axis=-1, keepdims=True), sink)
        p = jnp.exp(logits - m)
        denom = jnp.sum(p, axis=-1, keepdims=True) + jnp.exp(sink - m)
        o_h = _dot(p.astype(bf16), v_h) / denom
        for g in range(GQA_GROUP):
            c0 = (h * GQA_GROUP + g) * HEAD_DIM
            o_ref[:, c0:c0 + HEAD_DIM] = o_h[g * W:(g + 1) * W].astype(o_ref.dtype)


def _band_bias_masked(rel_bias):
    max_exact = N_BUCKETS // 2
    i = jnp.arange(WINDOW)[:, None]
    j = jnp.arange(2 * WINDOW)[None, :]
    dist = WINDOW + i - j
    nn = jnp.maximum(dist, 0)
    nf = jnp.maximum(nn, 1).astype(f32)
    large = max_exact + (jnp.log(nf / max_exact) / math.log(MAX_DISTANCE / max_exact)
                         * (N_BUCKETS - max_exact)).astype(jnp.int32)
    large = jnp.minimum(large, N_BUCKETS - 1)
    bucket = jnp.where(nn < max_exact, nn, large)
    b = jnp.transpose(rel_bias[bucket], (2, 0, 1)).astype(f32)
    in_band = (j > i) & (j <= i + WINDOW)
    b = jnp.where(in_band[None], b, NEG_BIG)
    return b.reshape(N_KV_HEADS, GQA_GROUP, WINDOW, 2 * WINDOW)


def _attention(qkv, bias, q_gain, k_gain, sinks, *, batch, seq):
    T = qkv.shape[0]
    nb = seq // WINDOW
    kcol = Q_DIM // KV_DIM
    vcol = kcol + 1

    def cur(col):
        return lambda b, n, s: (b * nb + n, col)

    def prev(col):
        return lambda b, n, s: (b * nb + jnp.maximum(n - 1, 0), col)

    grid_spec = pltpu.PrefetchScalarGridSpec(
        num_scalar_prefetch=1,
        grid=(batch, nb),
        in_specs=[
            pl.BlockSpec((WINDOW, Q_DIM), lambda b, n, s: (b * nb + n, 0)),
            pl.BlockSpec((WINDOW, KV_DIM), prev(kcol)),
            pl.BlockSpec((WINDOW, KV_DIM), cur(kcol)),
            pl.BlockSpec((WINDOW, KV_DIM), prev(vcol)),
            pl.BlockSpec((WINDOW, KV_DIM), cur(vcol)),
            pl.BlockSpec((N_KV_HEADS, GQA_GROUP, WINDOW, 2 * WINDOW), lambda b, n, s: (0, 0, 0, 0)),
            pl.BlockSpec((1, HEAD_DIM), lambda b, n, s: (0, 0)),
            pl.BlockSpec((1, HEAD_DIM), lambda b, n, s: (0, 0)),
        ],
        out_specs=pl.BlockSpec((WINDOW, Q_DIM), lambda b, n, s: (b * nb + n, 0)),
    )
    return pl.pallas_call(
        _attn_kernel,
        out_shape=jax.ShapeDtypeStruct((T, Q_DIM), bf16),
        grid_spec=grid_spec,
        compiler_params=_cparams(("parallel", "parallel")),
        name="swa_attention",
    )(sinks, qkv, qkv, qkv, qkv, qkv, bias, q_gain.reshape(1, HEAD_DIM), k_gain.reshape(1, HEAD_DIM))


def _wkv_kernel(r_ref, k_ref, v_ref, lora_ref, w0_ref, a0_ref, w2_ref, a2_ref, g2_ref,
                kk_ref, ka_ref, rk_ref, lng_ref, lnb_ref, o_ref,
                state_ref, w_s, c_s, a_s, g_s, ar_s, bk_s, vb_s, aab_s, aak_s, arb_s, ark_s, x_s, p_s, xb_s,
                aav_s, au_s, rh_s, yh_s, mk_s, ns_s, *, n_pairs):
    L = WKV_CHUNK
    P2 = 2 * L
    pairs = range(n_pairs)

    @pl.when(pl.program_id(1) == 0)
    def _():
        state_ref[...] = jnp.zeros(state_ref.shape, f32)

    lw = lora_ref[:, :LANES].astype(f32)
    la = lora_ref[:, LANES:2 * LANES]
    lg = lora_ref[:, 2 * LANES:].astype(f32)
    w_lin = w0_ref[...] + _dot(jnp.tanh(lw).astype(bf16), w2_ref[...])
    neg = -w_lin
    softplus = jnp.maximum(neg, 0.0) + jnp.log(1.0 + jnp.exp(-jnp.abs(neg)))
    logw = -jnp.exp(-softplus - 0.5)
    w_s[...] = logw
    a_s[...] = _sigmoid(a0_ref[...] + _dot(la, a2_ref[...]))
    g_s[...] = _dot(_sigmoid(lg).astype(bf16), g2_ref[...])
    ti = lax.broadcasted_iota(jnp.int32, (L, L), 0)
    tj = lax.broadcasted_iota(jnp.int32, (L, L), 1)
    tri = jnp.where(ti >= tj, 1.0, 0.0).astype(bf16)
    hi, mid, lo = _split3(logw)
    c_s[...] = _dot(tri, hi) + _dot(tri, mid) + _dot(tri, lo)

    lane = lax.broadcasted_iota(jnp.int32, (1, LANES), 1)
    m0 = lane < RWKV_HEAD
    ri = lax.broadcasted_iota(jnp.int32, (P2, P2), 0)
    ci = lax.broadcasted_iota(jnp.int32, (P2, P2), 1)
    same = (ri // L) == (ci // L)
    strict = jnp.logical_and(same, ri > ci)
    incl = jnp.logical_and(same, ri >= ci)
    eye = jnp.where(ri == ci, 1.0, 0.0)

    def seg_sum(x):
        s0 = jnp.sum(jnp.where(m0, x, 0.0), axis=-1, keepdims=True)
        s1 = jnp.sum(jnp.where(m0, 0.0, x), axis=-1, keepdims=True)
        return jnp.where(m0, s0, s1)

    def bd(x):
        return jnp.concatenate([jnp.where(m0, x, 0.0), jnp.where(m0, 0.0, x)], axis=0)

    def cols(p):
        return slice(p * LANES, (p + 1) * LANES)

    def k_mod(p):
        return k_ref[:, cols(p)].astype(f32) * (1.0 + (a_s[:, cols(p)] - 1.0) * ka_ref[p])

    for p in pairs:
        cs = cols(p)
        r = r_ref[:, cs].astype(f32)
        k = k_ref[:, cs].astype(f32)
        a = a_s[:, cs]
        cm = c_s[:, cs]
        kk = k * kk_ref[p]
        kk = kk / jnp.maximum(jnp.sqrt(seg_sum(kk * kk)), 1e-12)
        e_pos = jnp.exp(cm)
        e_neg = jnp.exp(-cm)
        e_exc = jnp.exp(cm - w_s[:, cs])
        ar_s[p, :P2] = bd(-kk * e_exc).astype(bf16)
        ar_s[p, P2:] = bd(r * e_pos).astype(bf16)
        bk_s[p, :P2] = bd(kk * a * e_neg).astype(bf16)
        bk_s[p, P2:] = bd(k_mod(p) * e_neg).astype(bf16)
        vb_s[p] = bd(v_ref[:, cs].astype(f32)).astype(bf16)

    for p in pairs:
        G = _dot_nt(ar_s[p], bk_s[p])
        A_ab = jnp.where(strict, G[:P2, :P2], 0.0)
        aab_s[p] = A_ab.astype(bf16)
        x_s[p] = eye + A_ab
        aak_s[p] = jnp.where(strict, G[:P2, P2:], 0.0).astype(bf16)
        arb_s[p] = jnp.where(incl, G[P2:, :P2], 0.0).astype(bf16)
        ark_s[p] = jnp.where(incl, G[P2:, P2:], 0.0).astype(bf16)

    for p in pairs:
        p_s[p] = _dot(aab_s[p], aab_s[p]).astype(bf16)
        aav_s[p, :, :LANES] = ar_s[p, :P2]
        aav_s[p, :, LANES:] = _dot(aak_s[p], vb_s[p]).astype(bf16)
    for _ in range(int(math.log2(L)) - 2):
        for p in pairs:
            pw = p_s[p]
            Z = _dot(jnp.concatenate([x_s[p].astype(bf16), pw], axis=0), pw)
            x_s[p] = x_s[p] + Z[:P2]
            p_s[p] = Z[P2:].astype(bf16)
    for p in pairs:
        X = x_s[p]
        xb_s[p] = (X + _dot(X.astype(bf16), p_s[p])).astype(bf16)
    for p in pairs:
        au_s[p] = _dot(xb_s[p], aav_s[p]).astype(bf16)

    for p in pairs:
        RY = _dot(arb_s[p], au_s[p])
        rh_s[p] = (ar_s[p, P2:].astype(f32) + RY[:, :LANES]).astype(bf16)
        yh_s[p] = RY[:, LANES:] + _dot(ark_s[p], vb_s[p])
        mk_s[p] = _dot_tn(au_s[p, :, :LANES], bk_s[p, :P2]).astype(bf16)
        uv = jnp.concatenate([au_s[p, :, LANES:], vb_s[p]], axis=0)
        ns_s[p] = _dot_tn(uv, bk_s[p])

    for p in pairs:
        cs = cols(p)
        S = state_ref[p]
        Sb = S.astype(bf16)
        Y = _dot_nt(rh_s[p], Sb) + yh_s[p]
        g_last = jnp.exp(c_s[L - 1:L, cs])
        state_ref[p] = (S + _dot(Sb, mk_s[p]) + ns_s[p]) * g_last
        y = Y[:L] + Y[L:]
        mean = seg_sum(y) * (1.0 / RWKV_HEAD)
        yc = y - mean
        var = seg_sum(yc * yc) * (1.0 / RWKV_HEAD)
        yn = yc * lax.rsqrt(var + GN_EPS) * lng_ref[p] + lnb_ref[p]
        r = r_ref[:, cs].astype(f32)
        v = v_ref[:, cs].astype(f32)
        yn = yn + seg_sum(r * k_mod(p) * rk_ref[p]) * v
        o_ref[:, cs] = (yn * g_s[:, cs]).astype(o_ref.dtype)


def _wkv(proj, w0, a0, w2, a2, g2, k_k, k_a, r_k, lnx_g, lnx_b, *, batch, seq):
    T = proj.shape[0]
    D = w0.shape[0]
    L = WKV_CHUNK
    P2 = 2 * L
    n_pairs = D // LANES
    nc = seq // L

    def row(width, col):
        return pl.BlockSpec((L, width), lambda b, t: (b * nc + t, col))

    def full(shape):
        return pl.BlockSpec(shape, lambda b, t: (0,) * len(shape))

    pair_vec = lambda a: a.reshape(n_pairs, 1, LANES).astype(f32)
    pv_spec = full((n_pairs, 1, LANES))
    wide = pltpu.VMEM((L, D), f32)
    sq = lambda dt: pltpu.VMEM((n_pairs, P2, P2), dt)
    tall = pltpu.VMEM((n_pairs, 2 * P2, LANES), bf16)
    wide2 = pltpu.VMEM((n_pairs, P2, 2 * LANES), bf16)
    return pl.pallas_call(
        functools.partial(_wkv_kernel, n_pairs=n_pairs),
        out_shape=jax.ShapeDtypeStruct((T, D), bf16),
        grid=(batch, nc),
        in_specs=[row(D, 0), row(D, 1), row(D, 2), row(RW_LORA_TILE, 3 * D // RW_LORA_TILE),
                  full((1, D)), full((1, D)), full(w2.shape), full(a2.shape), full(g2.shape),
                  pv_spec, pv_spec, pv_spec, pv_spec, pv_spec],
        out_specs=row(D, 0),
        scratch_shapes=[sq(f32), wide, wide, wide, wide, tall, tall, sq(bf16), sq(bf16), sq(bf16), sq(bf16), sq(bf16),
                        sq(f32), sq(bf16), sq(bf16), wide2, wide2, sq(bf16), sq(f32), sq(bf16), sq(f32)],
        compiler_params=_cparams(("parallel", "arbitrary")),
        name="wkv7",
    )(proj, proj, proj, proj, w0.reshape(1, D), a0.reshape(1, D), w2, a2, g2,
      pair_vec(k_k), pair_vec(k_a), pair_vec(r_k), pair_vec(lnx_g), pair_vec(lnx_b))


def _pool_kernel(x_ref, xh_ref, g_ref, sh_ref, sc_ref, gate_ref, w_ref, ps_ref, o_ref, *, tiles_per_batch, tm):
    i = pl.program_id(0)
    halo = 2 * SUBLANES
    g, sh, sc = g_ref[...], sh_ref[0], sc_ref[0]
    x = x_ref[...]
    h = _ada_norm(x, g, sh, sc)
    hh = _ada_norm(xh_ref[...], g, sh, sc)
    hh = jnp.where(i % tiles_per_batch == 0, 0.0, hh)
    ext = jnp.concatenate([hh, h], axis=0)
    pos = (i % tiles_per_batch) * tm + lax.broadcasted_iota(jnp.int32, (tm, 1), 0)
    for gi, w in enumerate(POOL_WINDOWS):
        cs = slice(gi * POOL_GROUP, (gi + 1) * POOL_GROUP)
        s = ext[:, cs]
        d = 1
        while d < w:
            s = s + pltpu.roll(s, d, axis=0)
            d *= 2
        cnt = jnp.minimum(pos + 1, w).astype(f32)
        pooled = s[halo:] / cnt - h[:, cs]
        mixed = _dot(pooled.astype(bf16), w_ref[gi]) * ps_ref[:, cs]
        o_ref[:, cs] = x[:, cs] + gate_ref[0][:, cs] * mixed


def _pool_layer(x, gain, mod, pool_w, pool_scale, *, seq, tm=512):
    T, D = x.shape
    tpb = seq // tm
    halo = 2 * SUBLANES
    vec = pl.BlockSpec((1, D), lambda i: (0, 0))
    return pl.pallas_call(
        functools.partial(_pool_kernel, tiles_per_batch=tpb, tm=tm),
        out_shape=jax.ShapeDtypeStruct((T, D), f32),
        grid=(T // tm,),
        in_specs=[
            pl.BlockSpec((tm, D), lambda i: (i, 0)),
            pl.BlockSpec((halo, D), lambda i: (jnp.maximum(i * (tm // halo) - 1, 0), 0)),
            vec,
            pl.BlockSpec((1, 1, D), lambda i: (i // tpb, 0, 0)),
            pl.BlockSpec((1, 1, D), lambda i: (i // tpb, 0, 1)),
            pl.BlockSpec((1, 1, D), lambda i: (i // tpb, 0, 2)),
            pl.BlockSpec(pool_w.shape, lambda i: (0, 0, 0)),
            vec,
        ],
        out_specs=pl.BlockSpec((tm, D), lambda i: (i, 0)),
        compiler_params=_cparams(("parallel",)),
        name="pool_mixer",
    )(x, x, gain, mod, mod, mod, pool_w, pool_scale.reshape(1, D))


def _route_kernel(x_ref, g_ref, sh_ref, sc_ref, w_ref, b_ref, h_ref, route_ref):
    h = _ada_norm(x_ref[...], g_ref[...], sh_ref[0], sc_ref[0])
    h_ref[...] = h
    h_hi, h_mid, _ = _split3(h)
    w_hi, w_mid, _ = _split3(w_ref[...])
    logits = _dot(h_hi, w_hi) + (_dot(h_hi, w_mid) + _dot(h_mid, w_hi)) + b_ref[...]
    lane = lax.broadcasted_iota(jnp.int32, logits.shape, 1)
    big = jnp.int32(LANES)

    def masked_argmax(mask):
        mx = jnp.max(jnp.where(mask, logits, NEG_BIG), axis=-1, keepdims=True)
        idx = jnp.min(jnp.where(jnp.logical_and(mask, logits == mx), lane, big), axis=-1, keepdims=True)
        return mx, idx

    gmask = lane < N_GROUPS
    gmax, gidx = masked_argmax(gmask)
    grp_w = 1.0 / jnp.sum(jnp.where(gmask, jnp.exp(logits - gmax), 0.0), axis=-1, keepdims=True)
    e_lo = EXP_LANE0 + gidx * EXPERTS_PER_GROUP
    emask = jnp.logical_and(lane >= e_lo, lane < e_lo + EXPERTS_PER_GROUP)
    m1, i1 = masked_argmax(emask)
    m2, i2 = masked_argmax(jnp.logical_and(emask, lane != i1))
    e21 = jnp.exp(m2 - m1)
    w1 = grp_w / (1.0 + e21)
    w2 = grp_w * e21 / (1.0 + e21)
    out = jnp.where(lane == 0, (i1 - EXP_LANE0).astype(f32), 0.0)
    out = jnp.where(lane == 1, (i2 - EXP_LANE0).astype(f32), out)
    out = jnp.where(lane == 2, w1, out)
    out = jnp.where(lane == 3, w2, out)
    route_ref[...] = out


def _route(x, gain, mod, w_router, b_router, *, seq, tm=512):
    T, D = x.shape
    tpb = seq // tm
    vec = pl.BlockSpec((1, D), lambda i: (0, 0))
    return pl.pallas_call(
        _route_kernel,
        out_shape=(jax.ShapeDtypeStruct((T, D), f32), jax.ShapeDtypeStruct((T, LANES), f32)),
        grid=(T // tm,),
        in_specs=[
            pl.BlockSpec((tm, D), lambda i: (i, 0)),
            vec,
            pl.BlockSpec((1, 1, D), lambda i: (i // tpb, 0, 0)),
            pl.BlockSpec((1, 1, D), lambda i: (i // tpb, 0, 1)),
            pl.BlockSpec((D, LANES), lambda i: (0, 0)),
            pl.BlockSpec((1, LANES), lambda i: (0, 0)),
        ],
        out_specs=(pl.BlockSpec((tm, D), lambda i: (i, 0)), pl.BlockSpec((tm, LANES), lambda i: (i, 0))),
        compiler_params=_cparams(("parallel",)),
        name="moe_route",
    )(x, gain, mod, mod, w_router, b_router)


def _expert_kernel(be_ref, nu_ref, tok_ref, h_hbm, wg_ref, wu_ref, wd_ref, o_ref, xbuf, sems, xb_s, wg_s, wu_s, wd_s,
                   *, n_blocks):
    i = pl.program_id(0)
    slot = i % 2
    nxt = jnp.minimum(i + 1, n_blocks - 1)

    def row_copy(blk, slot_, r):
        tok = tok_ref[blk * MOE_ROWS + r]
        return pltpu.make_async_copy(h_hbm.at[pl.ds(tok, 1)], xbuf.at[slot_, pl.ds(r, 1)], sems.at[slot_])

    def issue_loop(blk, slot_):
        def body(r, carry):
            row_copy(blk, slot_, r).start()
            return carry
        lax.fori_loop(0, MOE_ROWS, body, 0, unroll=8)

    def wait_block(slot_):
        pltpu.make_async_copy(h_hbm.at[pl.ds(0, MOE_ROWS)], xbuf.at[slot_], sems.at[slot_]).wait()

    @pl.when(i == 0)
    def _():
        issue_loop(0, 0)

    wait_block(slot)
    used = i < nu_ref[0]

    @pl.when(used)
    def _():
        @pl.when(jnp.logical_or(i == 0, be_ref[i] != be_ref[jnp.maximum(i - 1, 0)]))
        def _():
            wg_s[...] = wg_ref[0, 0].astype(bf16)
            wu_s[...] = wu_ref[0, 0].astype(bf16)
            wd_s[...] = wd_ref[0, 0].astype(bf16)

        xb_s[...] = xbuf[slot].astype(bf16)
        for r in range(MOE_ROWS):
            row_copy(nxt, 1 - slot, r).start()
        xb = xb_s[...]
        gate = _dot(xb, wg_s[...])
        up = _dot(xb, wu_s[...])
        hid = (gate * _sigmoid(gate) * up).astype(bf16)
        o_ref[...] = _dot(hid, wd_s[...])

    @pl.when(jnp.logical_not(used))
    def _():
        issue_loop(nxt, 1 - slot)
        o_ref[...] = jnp.zeros(o_ref.shape, o_ref.dtype)

    @pl.when(i == n_blocks - 1)
    def _():
        wait_block(1 - slot)


def _experts(h, slot_tok, block_e, n_used, w_gate, w_up, w_down, layer):
    T, D = h.shape
    n_blocks = block_e.shape[0]
    DE = w_gate.shape[-1]

    def wsel(i, be, nu, tok):
        return (layer, be[jnp.minimum(i, nu[0] - 1)], 0, 0)

    grid_spec = pltpu.PrefetchScalarGridSpec(
        num_scalar_prefetch=3,
        grid=(n_blocks,),
        in_specs=[
            pl.BlockSpec(memory_space=pl.ANY),
            pl.BlockSpec((1, 1, D, DE), wsel),
            pl.BlockSpec((1, 1, D, DE), wsel),
            pl.BlockSpec((1, 1, DE, D), wsel),
        ],
        out_specs=pl.BlockSpec((MOE_ROWS, D), lambda i, be, nu, tok: (i, 0)),
        scratch_shapes=[pltpu.VMEM((2, MOE_ROWS, D), f32), pltpu.SemaphoreType.DMA((2,)),
                        pltpu.VMEM((MOE_ROWS, D), bf16),
                        pltpu.VMEM((D, DE), bf16), pltpu.VMEM((D, DE), bf16), pltpu.VMEM((DE, D), bf16)],
    )
    return pl.pallas_call(
        functools.partial(_expert_kernel, n_blocks=n_blocks),
        out_shape=jax.ShapeDtypeStruct((n_blocks * MOE_ROWS, D), f32),
        grid_spec=grid_spec,
        compiler_params=_cparams(("arbitrary",)),
        name="moe_experts",
    )(block_e, n_used, slot_tok, h, w_gate, w_up, w_down)


def _combine_kernel(dest_ref, out_hbm, x_ref, route_ref, gate_ref, o_ref, rows, sems, *, tm, n_steps):
    i = pl.program_id(0)

    def issue(step, slot):
        def body(r, carry):
            t = step * tm + r
            for kk in range(TOP_K):
                d = dest_ref[t * TOP_K + kk]
                pltpu.make_async_copy(out_hbm.at[pl.ds(d, 1)], rows.at[slot, kk, pl.ds(r, 1)], sems.at[slot]).start()
            return carry
        lax.fori_loop(0, tm, body, 0, unroll=4)

    @pl.when(i == 0)
    def _():
        issue(0, 0)

    @pl.when(i + 1 < n_steps)
    def _():
        issue(i + 1, (i + 1) % 2)

    slot = i % 2
    for kk in range(TOP_K):
        pltpu.make_async_copy(out_hbm.at[pl.ds(0, tm)], rows.at[slot, kk], sems.at[slot]).wait()
    route = route_ref[...]
    y = route[:, 2:3] * rows[slot, 0] + route[:, 3:4] * rows[slot, 1]
    o_ref[...] = x_ref[...] + gate_ref[0] * y


def _combine(out_buf, dest, x, route, mod, *, seq, tm=256):
    T, D = x.shape
    n_steps = T // tm
    tpb = seq // tm
    grid_spec = pltpu.PrefetchScalarGridSpec(
        num_scalar_prefetch=1,
        grid=(n_steps,),
        in_specs=[
            pl.BlockSpec(memory_space=pl.ANY),
            pl.BlockSpec((tm, D), lambda i, d: (i, 0)),
            pl.BlockSpec((tm, LANES), lambda i, d: (i, 0)),
            pl.BlockSpec((1, 1, D), lambda i, d: (i // tpb, 0, 2)),
        ],
        out_specs=pl.BlockSpec((tm, D), lambda i, d: (i, 0)),
        scratch_shapes=[pltpu.VMEM((2, TOP_K, tm, D), f32), pltpu.SemaphoreType.DMA((2,))],
    )
    return pl.pallas_call(
        functools.partial(_combine_kernel, tm=tm, n_steps=n_steps),
        out_shape=jax.ShapeDtypeStruct((T, D), f32),
        grid_spec=grid_spec,
        compiler_params=_cparams(("arbitrary",)),
        name="moe_combine",
    )(dest, out_buf, x, route, mod)


def _dispatch_plan(ids, n_tokens):
    N = n_tokens * TOP_K
    i32 = jnp.int32
    flat_ids = ids.reshape(-1)
    order = jnp.argsort(flat_ids).astype(i32)
    inv_order = jnp.argsort(order).astype(i32)
    experts = jnp.arange(N_EXPERTS, dtype=i32)
    counts = jnp.sum((flat_ids[:, None] == experts[None, :]).astype(i32), axis=0)
    start = jnp.cumsum(counts) - counts
    padded = ((counts + MOE_ROWS - 1) // MOE_ROWS) * MOE_ROWS
    seg_end = jnp.cumsum(padded).astype(i32)
    pad_start = seg_end - padded
    dest = (pad_start - start)[flat_ids] + inv_order
    n_blocks = -(-N // MOE_ROWS) + N_EXPERTS
    blk_row0 = jnp.arange(n_blocks, dtype=i32) * MOE_ROWS
    block_e = jnp.minimum(jnp.sum((blk_row0[:, None] >= seg_end[None, :]).astype(i32), axis=1), N_EXPERTS - 1)
    n_used = (seg_end[-1:] // MOE_ROWS).astype(i32)
    row = jnp.arange(n_blocks * MOE_ROWS, dtype=i32)
    row_e = jnp.repeat(block_e, MOE_ROWS)
    j = row - pad_start[row_e]
    src = order[jnp.clip(start[row_e] + j, 0, N - 1)] // TOP_K
    slot_tok = jnp.where(j < counts[row_e], src, 0).astype(i32)
    return dest.astype(i32), slot_tok, block_e.astype(i32), n_used


def _moe_layer(x, gain, mod, w_router, b_router, w_gate, w_up, w_down, layer, *, seq):
    T, D = x.shape
    h, route = _route(x, gain, mod, w_router, b_router, seq=seq)
    ids = route[:, :TOP_K].astype(jnp.int32)
    dest, slot_tok, block_e, n_used = _dispatch_plan(ids, T)
    out_buf = _experts(h, slot_tok, block_e, n_used, w_gate, w_up, w_down, layer)
    return _combine(out_buf, dest, x, route, mod, seq=seq)


def _router_params(w_grp, b_grp, w_exp, b_exp):
    D = w_grp.shape[0]
    w = jnp.zeros((D, LANES), f32).at[:, :N_GROUPS].set(w_grp).at[:, EXP_LANE0:EXP_LANE0 + N_EXPERTS].set(w_exp)
    b = jnp.zeros((1, LANES), f32).at[0, :N_GROUPS].set(b_grp).at[0, EXP_LANE0:EXP_LANE0 + N_EXPERTS].set(b_exp)
    return w, b


def _pad_cols(w, n):
    return jnp.zeros((w.shape[0], n), w.dtype).at[:, :w.shape[1]].set(w)


def _pad_rows(w, n):
    return jnp.zeros((n, w.shape[1]), w.dtype).at[:w.shape[0]].set(w)


def kernel(x, c, norm_g, ada_w, ada_b, rel_bias, attn_w_in, attn_w_o, attn_q_gain, attn_k_gain, attn_sinks, rw_mu, rw_w_rkv, rw_w0, rw_w1, rw_w2, rw_a0, rw_a1, rw_a2, rw_g1, rw_g2, rw_k_k, rw_k_a, rw_r_k, rw_lnx_g, rw_lnx_b, rw_w_o, pool_w, pool_scale, moe_w_grp, moe_b_grp, moe_w_exp, moe_b_exp, moe_w_gate, moe_w_up, moe_w_down):
    B, S, D = x.shape
    T = B * S
    xt = x.reshape(T, D)
    mods = _ada_mods(c, ada_w, ada_b)
    bias = _band_bias_masked(rel_bias)
    for layer in range(DEPTH):
        kind, idx = layer % N_MIXERS, layer // N_MIXERS
        gain = norm_g[layer, 0].reshape(1, D)
        mod = mods[2 * layer]
        if kind == 0:
            qkv = _norm_mm(xt, gain, mod, attn_w_in[idx].astype(bf16), seq=S)
            o = _attention(qkv, bias, attn_q_gain[idx], attn_k_gain[idx], attn_sinks[idx], batch=B, seq=S)
            xt = _mm_res(o, attn_w_o[idx].astype(bf16), xt, mod, seq=S)
        elif kind == 1:
            w_all = jnp.concatenate(
                [rw_w_rkv[idx, 0], rw_w_rkv[idx, 1], rw_w_rkv[idx, 2], _pad_cols(rw_w1[idx], LANES),
                 _pad_cols(rw_a1[idx], LANES), _pad_cols(rw_g1[idx], RW_PROJ_TILE - 2 * LANES)], axis=1).astype(bf16)
            proj = _rwkv_proj(xt, gain, mod, rw_mu[idx], w_all, seq=S)
            yg = _wkv(proj, rw_w0[idx], rw_a0[idx],
                      _pad_rows(rw_w2[idx], LANES).astype(bf16), _pad_rows(rw_a2[idx], LANES).astype(bf16),
                      rw_g2[idx].astype(bf16), rw_k_k[idx], rw_k_a[idx], rw_r_k[idx], rw_lnx_g[idx], rw_lnx_b[idx],
                      batch=B, seq=S)
            xt = _mm_res(yg, rw_w_o[idx].astype(bf16), xt, mod, seq=S)
        else:
            xt = _pool_layer(xt, gain, mod, pool_w[idx].astype(bf16), pool_scale[idx], seq=S)
        w_router, b_router = _router_params(moe_w_grp[layer], moe_b_grp[layer], moe_w_exp[layer], moe_b_exp[layer])
        xt = _moe_layer(xt, norm_g[layer, 1].reshape(1, D), mods[2 * layer + 1], w_router, b_router,
                        moe_w_gate, moe_w_up, moe_w_down, layer, seq=S)
    return xt.reshape(B, S, D)
```

```python
import functools
import math

import jax
import jax.numpy as jnp
import numpy as np
from jax import lax
from jax.experimental import pallas as pl
from jax.experimental.pallas import tpu as pltpu

f32 = jnp.float32
bf16 = jnp.bfloat16

D_MODEL = 2048
DEPTH = 4
N_MIXERS = 3
HEAD_DIM = 64
N_HEADS = D_MODEL // HEAD_DIM
N_KV_HEADS = 4
GQA_GROUP = N_HEADS // N_KV_HEADS
WINDOW = 128
Q_DIM = N_HEADS * HEAD_DIM
KV_DIM = N_KV_HEADS * HEAD_DIM
QKV_DIM = Q_DIM + 2 * KV_DIM
N_BUCKETS = 32
MAX_DISTANCE = 128
RWKV_HEAD = 64
GN_EPS = 64e-5
POOL_WINDOWS = (2, 4, 8, 16)
POOL_GROUP = D_MODEL // len(POOL_WINDOWS)
N_GROUPS = 4
EXPERTS_PER_GROUP = 8
N_EXPERTS = N_GROUPS * EXPERTS_PER_GROUP
TOP_K = 2
D_EXPERT = D_MODEL // 4
NORM_EPS = 1e-6

LANES = 128
SUBLANES = 8
VMEM_LIMIT = 56 * 1024 * 1024

MOE_ROWS = 256
WKV_CHUNK = 64
EXP_LANE0 = 32
NEG_BIG = -1e30


def _cparams(sem):
    return pltpu.CompilerParams(dimension_semantics=sem, vmem_limit_bytes=VMEM_LIMIT)


def _ada_norm(xf, gain, shift, scale):
    ms = jnp.mean(xf * xf, axis=-1, keepdims=True)
    return xf * lax.rsqrt(ms + NORM_EPS) * gain * (1.0 + scale) + shift


def _sigmoid(z):
    return 1.0 / (1.0 + jnp.exp(-z))


def _split3(x):
    hi = x.astype(bf16)
    r1 = x - hi.astype(f32)
    mid = r1.astype(bf16)
    lo = (r1 - mid.astype(f32)).astype(bf16)
    return hi, mid, lo


def _pack_halves(x):
    n = x.shape[1] // 2
    lo = pltpu.bitcast(x[:, :n].astype(bf16).astype(f32), jnp.uint32)
    hi = pltpu.bitcast(x[:, n:].astype(bf16).astype(f32), jnp.uint32)
    return (lo >> 16) | hi


def _unpack_halves(u):
    return pltpu.bitcast(u << 16, f32), pltpu.bitcast(u & jnp.uint32(0xFFFF0000), f32)


def _dot(a, b):
    return jnp.dot(a, b, preferred_element_type=f32)


def _dot_nt(a, b):
    return lax.dot_general(a, b, (((1,), (1,)), ((), ())), preferred_element_type=f32)


def _dot_tn(a, b):
    return lax.dot_general(a, b, (((0,), (0,)), ((), ())), preferred_element_type=f32)


def _ada_kernel(c_ref, w_ref, b_ref, o_ref):
    c = c_ref[...]
    ca = (c * _sigmoid(c)).astype(bf16)
    o_ref[0] = _dot(ca, w_ref[0].astype(bf16)) + b_ref[0]


def _ada_mods(c, ada_w, ada_b):
    B, D = c.shape
    n_mod = ada_w.shape[0] * ada_w.shape[1]
    N = ada_w.shape[-1]
    tn = 1024
    c_pad = jnp.zeros((SUBLANES, D), f32).at[:B].set(c)
    out = pl.pallas_call(
        _ada_kernel,
        out_shape=jax.ShapeDtypeStruct((n_mod, SUBLANES, N), f32),
        grid=(n_mod, N // tn),
        in_specs=[
            pl.BlockSpec((SUBLANES, D), lambda m, j: (0, 0)),
            pl.BlockSpec((1, D, tn), lambda m, j: (m, 0, j)),
            pl.BlockSpec((1, 1, tn), lambda m, j: (m, 0, j)),
        ],
        out_specs=pl.BlockSpec((1, SUBLANES, tn), lambda m, j: (m, 0, j)),
        compiler_params=_cparams(("parallel", "parallel")),
        name="ada_mods",
    )(c_pad, ada_w.reshape(n_mod, D, N), ada_b.reshape(n_mod, 1, N))
    return out[:, :B].reshape(n_mod, B, 1, N)


def _norm_mm_kernel(x_ref, g_ref, sh_ref, sc_ref, w_ref, o_ref, lhs_ref):
    @pl.when(pl.program_id(1) == 0)
    def _():
        lhs_ref[...] = _ada_norm(x_ref[...], g_ref[...], sh_ref[0], sc_ref[0]).astype(bf16)

    o_ref[...] = _dot(lhs_ref[...], w_ref[...]).astype(o_ref.dtype)


def _norm_mm(x, gain, mod, w, *, seq, tm=512, tn=1280):
    T, D = x.shape
    N = w.shape[1]
    tpb = seq // tm
    return pl.pallas_call(
        _norm_mm_kernel,
        out_shape=jax.ShapeDtypeStruct((T, N), bf16),
        grid=(T // tm, N // tn),
        in_specs=[
            pl.BlockSpec((tm, D), lambda i, j: (i, 0)),
            pl.BlockSpec((1, D), lambda i, j: (0, 0)),
            pl.BlockSpec((1, 1, D), lambda i, j: (i // tpb, 0, 0)),
            pl.BlockSpec((1, 1, D), lambda i, j: (i // tpb, 0, 1)),
            pl.BlockSpec((D, tn), lambda i, j: (0, j)),
        ],
        out_specs=pl.BlockSpec((tm, tn), lambda i, j: (i, j)),
        scratch_shapes=[pltpu.VMEM((tm, D), bf16)],
        compiler_params=_cparams(("parallel", "arbitrary")),
        name="norm_mm",
    )(x, gain, mod, mod, w)


RW_MIXES = 6
RW_LORA_TILE = 512
RW_PROJ_TILE = 1024


def _rwkv_proj_kernel(x_ref, xh_ref, g_ref, sh_ref, sc_ref, mu_ref, w_ref, o_ref, lhs_ref, *, tiles_per_batch, n_big,
                      chunks_per_mix):
    i = pl.program_id(0)
    j = pl.program_id(1)

    @pl.when(j == 0)
    def _():
        g, sh, sc = g_ref[...], sh_ref[0], sc_ref[0]
        h = _ada_norm(x_ref[...], g, sh, sc)
        h_last = _ada_norm(xh_ref[...], g, sh, sc)[SUBLANES - 1:SUBLANES]
        h_last = jnp.where(i % tiles_per_batch == 0, 0.0, h_last)
        row = lax.broadcasted_iota(jnp.int32, h.shape, 0)
        dh = jnp.where(row == 0, h_last, pltpu.roll(h, 1, axis=0)) - h
        for m in range(RW_MIXES):
            lhs_ref[m] = (h + dh * mu_ref[m:m + 1, :]).astype(bf16)

    @pl.when(j < n_big)
    def _():
        o_ref[...] = _dot(lhs_ref[j // chunks_per_mix], w_ref[...]).astype(o_ref.dtype)

    @pl.when(j == n_big)
    def _():
        q = LANES
        o_ref[:, :RW_LORA_TILE] = jnp.concatenate(
            [_dot(lhs_ref[3], w_ref[:, :q]), _dot(lhs_ref[4], w_ref[:, q:2 * q]),
             _dot(lhs_ref[5], w_ref[:, 2 * q:RW_LORA_TILE])], axis=1).astype(o_ref.dtype)
        o_ref[:, RW_LORA_TILE:] = jnp.zeros((o_ref.shape[0], o_ref.shape[1] - RW_LORA_TILE), o_ref.dtype)


def _rwkv_proj(x, gain, mod, mu, w_all, *, seq, tm=512, tn=RW_PROJ_TILE):
    T, D = x.shape
    n_big = 3 * D // tn
    tpb = seq // tm
    vec_spec = pl.BlockSpec((1, D), lambda i, j: (0, 0))
    return pl.pallas_call(
        functools.partial(_rwkv_proj_kernel, tiles_per_batch=tpb, n_big=n_big, chunks_per_mix=D // tn),
        out_shape=jax.ShapeDtypeStruct((T, w_all.shape[1]), bf16),
        grid=(T // tm, n_big + 1),
        in_specs=[
            pl.BlockSpec((tm, D), lambda i, j: (i, 0)),
            pl.BlockSpec((SUBLANES, D), lambda i, j: (jnp.maximum(i * (tm // SUBLANES) - 1, 0), 0)),
            vec_spec,
            pl.BlockSpec((1, 1, D), lambda i, j: (i // tpb, 0, 0)),
            pl.BlockSpec((1, 1, D), lambda i, j: (i // tpb, 0, 1)),
            pl.BlockSpec((RW_MIXES, D), lambda i, j: (0, 0)),
            pl.BlockSpec((D, tn), lambda i, j: (0, j)),
        ],
        out_specs=pl.BlockSpec((tm, tn), lambda i, j: (i, j)),
        scratch_shapes=[pltpu.VMEM((RW_MIXES, tm, D), bf16)],
        compiler_params=_cparams(("parallel", "arbitrary")),
        name="rwkv_proj",
    )(x, x, gain, mod, mod, mu, w_all)


def _mm_res_kernel(a_ref, w_ref, x_ref, gate_ref, o_ref):
    o_ref[...] = x_ref[...] + gate_ref[0] * _dot(a_ref[...], w_ref[...])


def _mm_res(a, w, x, mod, *, seq, tm=1024, tn=1024):
    T, K = a.shape
    N = w.shape[1]
    tpb = seq // tm
    gate_blk = 2 * (N // tn)
    return pl.pallas_call(
        _mm_res_kernel,
        out_shape=jax.ShapeDtypeStruct((T, N), f32),
        grid=(T // tm, N // tn),
        in_specs=[
            pl.BlockSpec((tm, K), lambda i, j: (i, 0)),
            pl.BlockSpec((K, tn), lambda i, j: (0, j)),
            pl.BlockSpec((tm, tn), lambda i, j: (i, j)),
            pl.BlockSpec((1, 1, tn), lambda i, j: (i // tpb, 0, gate_blk + j)),
        ],
        out_specs=pl.BlockSpec((tm, tn), lambda i, j: (i, j)),
        compiler_params=_cparams(("parallel", "parallel")),
        name="mm_res",
    )(a, w, x, mod)


def _attn_kernel(sink_ref, q_ref, kp_ref, kc_ref, vp_ref, vc_ref, bias_ref, qg_ref, kg_ref, o_ref):
    n = pl.program_id(1)
    W = WINDOW
    GW = GQA_GROUP * W
    heads_per_tile = LANES // HEAD_DIM
    n_qt = Q_DIM // LANES
    qt_per_kv = n_qt // N_KV_HEADS
    lane = lax.broadcasted_iota(jnp.int32, (1, LANES), 1)
    m0 = lane < HEAD_DIM
    ri = lax.broadcasted_iota(jnp.int32, (LANES, LANES), 0)
    ci = lax.broadcasted_iota(jnp.int32, (LANES, LANES), 1)
    head_ones = jnp.where((ri // HEAD_DIM) == (ci // HEAD_DIM), 1.0, 0.0).astype(bf16)

    def inv_rms(t):
        tt = t * t
        hi = tt.astype(bf16)
        lo = (tt - hi.astype(f32)).astype(bf16)
        ssq = _dot(hi, head_ones) + _dot(lo, head_ones)
        return lax.rsqrt(ssq * (1.0 / HEAD_DIM) + NORM_EPS)

    q_stack = jnp.concatenate([q_ref[:, t * LANES:(t + 1) * LANES] for t in range(n_qt)], axis=0).astype(f32)
    q_stack = q_stack * inv_rms(q_stack) * qg_ref[...]
    k_band = jnp.concatenate([kp_ref[...], kc_ref[...]], axis=0).astype(f32)
    v_band = jnp.concatenate([vp_ref[...], vc_ref[...]], axis=0).astype(f32)
    col = lax.broadcasted_iota(jnp.int32, (GW, 2 * W), 1)
    key_ok = jnp.logical_or(n > 0, col >= W)
    grp = lax.broadcasted_iota(jnp.int32, (GW, 1), 0) // W
    ones_kv = jnp.ones((2 * W, LANES), bf16)
    for kt in range(KV_DIM // LANES):
        k_tile = k_band[:, kt * LANES:(kt + 1) * LANES]
        k_tile = k_tile * inv_rms(k_tile) * kg_ref[...]
        k_roll = pltpu.roll(k_tile, HEAD_DIM, axis=1)
        v_tile = v_band[:, kt * LANES:(kt + 1) * LANES]
        v_roll = pltpu.roll(v_tile, HEAD_DIM, axis=1)
        for side in range(heads_per_tile):
            h = kt * heads_per_tile + side
            first = m0 if side == 0 else jnp.logical_not(m0)
            k_dup = jnp.where(first, k_tile, k_roll).astype(bf16)
            v_dup = jnp.where(first, v_tile, v_roll).astype(bf16)
            lhs = []
            for j in range(qt_per_kv):
                q_t = q_stack[(h * qt_per_kv + j) * W:(h * qt_per_kv + j + 1) * W]
                lhs += [jnp.where(m0, q_t, 0.0), jnp.where(m0, 0.0, q_t)]
            lhs = jnp.concatenate(lhs, axis=0).astype(bf16)
            logits = _dot_nt(lhs, k_dup)
            logits = logits + bias_ref[h].reshape(GW, 2 * W)
            logits = jnp.where(key_ok, logits, NEG_BIG)
            sink = jnp.zeros((GW, 1), f32)
            for g in range(GQA_GROUP):
                sink = jnp.where(grp == g, sink_ref[h * GQA_GROUP + g], sink)
            m = jnp.maximum(jnp.max(logits, axis=-1, keepdims=True), sink)
            p = jnp.exp(logits - m).astype(bf16)
            denom = _dot(p, ones_kv) + jnp.exp(sink - m)
            o_full = _dot(p, v_dup) / denom
            for j in range(qt_per_kv):
                t = h * qt_per_kv + j
                o_ref[:, t * LANES:(t + 1) * LANES] = jnp.where(
                    m0, o_full[2 * j * W:(2 * j + 1) * W], o_full[(2 * j + 1) * W:(2 * j + 2) * W]).astype(o_ref.dtype)


def _band_bias_masked(rel_bias):
    max_exact = N_BUCKETS // 2
    i = jnp.arange(WINDOW)[:, None]
    j = jnp.arange(2 * WINDOW)[None, :]
    dist = WINDOW + i - j
    nn = jnp.maximum(dist, 0)
    nf = jnp.maximum(nn, 1).astype(f32)
    large = max_exact + (jnp.log(nf / max_exact) / math.log(MAX_DISTANCE / max_exact)
                         * (N_BUCKETS - max_exact)).astype(jnp.int32)
    large = jnp.minimum(large, N_BUCKETS - 1)
    bucket = jnp.where(nn < max_exact, nn, large)
    b = jnp.transpose(rel_bias[bucket], (2, 0, 1)).astype(f32)
    in_band = (j > i) & (j <= i + WINDOW)
    b = jnp.where(in_band[None], b, NEG_BIG)
    return b.reshape(N_KV_HEADS, GQA_GROUP, WINDOW, 2 * WINDOW)


def _attention(qkv, bias, q_gain, k_gain, sinks, *, batch, seq):
    T = qkv.shape[0]
    nb = seq // WINDOW
    kcol = Q_DIM // KV_DIM
    vcol = kcol + 1

    def cur(col):
        return lambda b, n, s: (b * nb + n, col)

    def prev(col):
        return lambda b, n, s: (b * nb + jnp.maximum(n - 1, 0), col)

    def tile_gain(g):
        return jnp.tile(g.astype(f32), LANES // HEAD_DIM).reshape(1, LANES)

    grid_spec = pltpu.PrefetchScalarGridSpec(
        num_scalar_prefetch=1,
        grid=(batch, nb),
        in_specs=[
            pl.BlockSpec((WINDOW, Q_DIM), lambda b, n, s: (b * nb + n, 0)),
            pl.BlockSpec((WINDOW, KV_DIM), prev(kcol)),
            pl.BlockSpec((WINDOW, KV_DIM), cur(kcol)),
            pl.BlockSpec((WINDOW, KV_DIM), prev(vcol)),
            pl.BlockSpec((WINDOW, KV_DIM), cur(vcol)),
            pl.BlockSpec((N_KV_HEADS, GQA_GROUP, WINDOW, 2 * WINDOW), lambda b, n, s: (0, 0, 0, 0)),
            pl.BlockSpec((1, LANES), lambda b, n, s: (0, 0)),
            pl.BlockSpec((1, LANES), lambda b, n, s: (0, 0)),
        ],
        out_specs=pl.BlockSpec((WINDOW, Q_DIM), lambda b, n, s: (b * nb + n, 0)),
    )
    return pl.pallas_call(
        _attn_kernel,
        out_shape=jax.ShapeDtypeStruct((T, Q_DIM), bf16),
        grid_spec=grid_spec,
        compiler_params=_cparams(("parallel", "parallel")),
        name="swa_attention",
    )(sinks, qkv, qkv, qkv, qkv, qkv, bias, tile_gain(q_gain * (HEAD_DIM ** -0.5)), tile_gain(k_gain))


def _wkv_kernel(r_ref, k_ref, v_ref, lora_ref, w0_ref, a0_ref, w2_ref, a2_ref, g2_ref,
                kk_ref, ka_ref, rk_ref, lng_ref, lnb_ref, o_ref,
                state_ref, w_s, c_s, a_s, g_s, ar_s, bk_s, vb_s, aab_s, aak_s, arb_s, ark_s, x_s, p_s, xb_s,
                aav_s, au_s, rh_s, yh_s, mk_s, ns_s, *, n_pairs):
    L = WKV_CHUNK
    P2 = 2 * L
    pairs = range(n_pairs)

    @pl.when(pl.program_id(1) == 0)
    def _():
        state_ref[...] = jnp.zeros(state_ref.shape, f32)

    lw = lora_ref[:, :LANES].astype(f32)
    la = lora_ref[:, LANES:2 * LANES]
    lg = lora_ref[:, 2 * LANES:].astype(f32)
    w_lin = w0_ref[...] + _dot(jnp.tanh(lw).astype(bf16), w2_ref[...])
    neg = -w_lin
    softplus = jnp.maximum(neg, 0.0) + jnp.log(1.0 + jnp.exp(-jnp.abs(neg)))
    logw = -jnp.exp(-softplus - 0.5)
    w_s[...] = logw
    a_s[...] = _sigmoid(a0_ref[...] + _dot(la, a2_ref[...]))
    g_s[...] = _dot(_sigmoid(lg).astype(bf16), g2_ref[...])
    ti = lax.broadcasted_iota(jnp.int32, (L, L), 0)
    tj = lax.broadcasted_iota(jnp.int32, (L, L), 1)
    tri = jnp.where(ti >= tj, 1.0, 0.0).astype(bf16)
    hi, mid, lo = _split3(logw)
    c_s[...] = _dot(tri, hi) + _dot(tri, mid) + _dot(tri, lo)

    lane = lax.broadcasted_iota(jnp.int32, (1, LANES), 1)
    m0 = lane < RWKV_HEAD
    ri = lax.broadcasted_iota(jnp.int32, (P2, P2), 0)
    ci = lax.broadcasted_iota(jnp.int32, (P2, P2), 1)
    same = (ri // L) == (ci // L)
    strict = jnp.logical_and(same, ri > ci)
    incl = jnp.logical_and(same, ri >= ci)
    eye = jnp.where(ri == ci, 1.0, 0.0)

    def seg_sum(x):
        s0 = jnp.sum(jnp.where(m0, x, 0.0), axis=-1, keepdims=True)
        s1 = jnp.sum(jnp.where(m0, 0.0, x), axis=-1, keepdims=True)
        return jnp.where(m0, s0, s1)

    def bd(x):
        return jnp.concatenate([jnp.where(m0, x, 0.0), jnp.where(m0, 0.0, x)], axis=0)

    def cols(p):
        return slice(p * LANES, (p + 1) * LANES)

    def k_mod(p):
        return k_ref[:, cols(p)].astype(f32) * (1.0 + (a_s[:, cols(p)] - 1.0) * ka_ref[p])

    for p in pairs:
        cs = cols(p)
        r = r_ref[:, cs].astype(f32)
        k = k_ref[:, cs].astype(f32)
        a = a_s[:, cs]
        cm = c_s[:, cs]
        kk = k * kk_ref[p]
        kk = kk / jnp.maximum(jnp.sqrt(seg_sum(kk * kk)), 1e-12)
        e_pos = jnp.exp(cm)
        e_neg = jnp.exp(-cm)
        e_exc = jnp.exp(cm - w_s[:, cs])
        ar_s[p, :P2] = bd(-kk * e_exc).astype(bf16)
        ar_s[p, P2:] = bd(r * e_pos).astype(bf16)
        bk_s[p, :P2] = bd(kk * a * e_neg).astype(bf16)
        bk_s[p, P2:] = bd(k_mod(p) * e_neg).astype(bf16)
        vb_s[p] = bd(v_ref[:, cs].astype(f32)).astype(bf16)

    for p in pairs:
        G = _dot_nt(ar_s[p], bk_s[p])
        A_ab = jnp.where(strict, G[:P2, :P2], 0.0)
        aab_s[p] = A_ab.astype(bf16)
        x_s[p] = eye + A_ab
        aak_s[p] = jnp.where(strict, G[:P2, P2:], 0.0).astype(bf16)
        arb_s[p] = jnp.where(incl, G[P2:, :P2], 0.0).astype(bf16)
        ark_s[p] = jnp.where(incl, G[P2:, P2:], 0.0).astype(bf16)

    for p in pairs:
        p_s[p] = _dot(aab_s[p], aab_s[p]).astype(bf16)
        aav_s[p, :, :LANES] = ar_s[p, :P2]
        aav_s[p, :, LANES:] = _dot(aak_s[p], vb_s[p]).astype(bf16)
    for _ in range(int(math.log2(L)) - 2):
        for p in pairs:
            pw = p_s[p]
            Z = _dot(jnp.concatenate([x_s[p].astype(bf16), pw], axis=0), pw)
            x_s[p] = x_s[p] + Z[:P2]
            p_s[p] = Z[P2:].astype(bf16)
    for p in pairs:
        X = x_s[p]
        xb_s[p] = (X + _dot(X.astype(bf16), p_s[p])).astype(bf16)
    for p in pairs:
        au_s[p] = _dot(xb_s[p], aav_s[p]).astype(bf16)

    for p in pairs:
        RY = _dot(arb_s[p], au_s[p])
        rh_s[p] = (ar_s[p, P2:].astype(f32) + RY[:, :LANES]).astype(bf16)
        yh_s[p] = RY[:, LANES:] + _dot(ark_s[p], vb_s[p])
        mk_s[p] = _dot_tn(au_s[p, :, :LANES], bk_s[p, :P2]).astype(bf16)
        uv = jnp.concatenate([au_s[p, :, LANES:], vb_s[p]], axis=0)
        ns_s[p] = _dot_tn(uv, bk_s[p])

    for p in pairs:
        cs = cols(p)
        S = state_ref[p]
        Sb = S.astype(bf16)
        Y = _dot_nt(rh_s[p], Sb) + yh_s[p]
        g_last = jnp.exp(c_s[L - 1:L, cs])
        state_ref[p] = (S + _dot(Sb, mk_s[p]) + ns_s[p]) * g_last
        y = Y[:L] + Y[L:]
        mean = seg_sum(y) * (1.0 / RWKV_HEAD)
        yc = y - mean
        var = seg_sum(yc * yc) * (1.0 / RWKV_HEAD)
        yn = yc * lax.rsqrt(var + GN_EPS) * lng_ref[p] + lnb_ref[p]
        r = r_ref[:, cs].astype(f32)
        v = v_ref[:, cs].astype(f32)
        yn = yn + seg_sum(r * k_mod(p) * rk_ref[p]) * v
        o_ref[:, cs] = (yn * g_s[:, cs]).astype(o_ref.dtype)


def _wkv(proj, w0, a0, w2, a2, g2, k_k, k_a, r_k, lnx_g, lnx_b, *, batch, seq):
    T = proj.shape[0]
    D = w0.shape[0]
    L = WKV_CHUNK
    P2 = 2 * L
    n_pairs = D // LANES
    nc = seq // L

    def row(width, col):
        return pl.BlockSpec((L, width), lambda b, t: (b * nc + t, col))

    def full(shape):
        return pl.BlockSpec(shape, lambda b, t: (0,) * len(shape))

    pair_vec = lambda a: a.reshape(n_pairs, 1, LANES).astype(f32)
    pv_spec = full((n_pairs, 1, LANES))
    wide = pltpu.VMEM((L, D), f32)
    sq = lambda dt: pltpu.VMEM((n_pairs, P2, P2), dt)
    tall = pltpu.VMEM((n_pairs, 2 * P2, LANES), bf16)
    wide2 = pltpu.VMEM((n_pairs, P2, 2 * LANES), bf16)
    return pl.pallas_call(
        functools.partial(_wkv_kernel, n_pairs=n_pairs),
        out_shape=jax.ShapeDtypeStruct((T, D), bf16),
        grid=(batch, nc),
        in_specs=[row(D, 0), row(D, 1), row(D, 2), row(RW_LORA_TILE, 3 * D // RW_LORA_TILE),
                  full((1, D)), full((1, D)), full(w2.shape), full(a2.shape), full(g2.shape),
                  pv_spec, pv_spec, pv_spec, pv_spec, pv_spec],
        out_specs=row(D, 0),
        scratch_shapes=[sq(f32), wide, wide, wide, wide, tall, tall, sq(bf16), sq(bf16), sq(bf16), sq(bf16), sq(bf16),
                        sq(f32), sq(bf16), sq(bf16), wide2, wide2, sq(bf16), sq(f32), sq(bf16), sq(f32)],
        compiler_params=_cparams(("parallel", "arbitrary")),
        name="wkv7",
    )(proj, proj, proj, proj, w0.reshape(1, D), a0.reshape(1, D), w2, a2, g2,
      pair_vec(k_k), pair_vec(k_a), pair_vec(r_k), pair_vec(lnx_g), pair_vec(lnx_b))


def _pool_kernel(x_ref, xh_ref, g_ref, sh_ref, sc_ref, gate_ref, w_ref, ps_ref, o_ref, *, tiles_per_batch, tm):
    i = pl.program_id(0)
    halo = 2 * SUBLANES
    g, sh, sc = g_ref[...], sh_ref[0], sc_ref[0]
    x = x_ref[...]
    h = _ada_norm(x, g, sh, sc)
    hh = _ada_norm(xh_ref[...], g, sh, sc)
    hh = jnp.where(i % tiles_per_batch == 0, 0.0, hh)
    ext = jnp.concatenate([hh, h], axis=0)
    pos = (i % tiles_per_batch) * tm + lax.broadcasted_iota(jnp.int32, (tm, 1), 0)
    for gi, w in enumerate(POOL_WINDOWS):
        cs = slice(gi * POOL_GROUP, (gi + 1) * POOL_GROUP)
        s = ext[:, cs]
        d = 1
        while d < w:
            s = s + pltpu.roll(s, d, axis=0)
            d *= 2
        cnt = jnp.minimum(pos + 1, w).astype(f32)
        pooled = s[halo:] / cnt - h[:, cs]
        mixed = _dot(pooled.astype(bf16), w_ref[gi]) * ps_ref[:, cs]
        o_ref[:, cs] = x[:, cs] + gate_ref[0][:, cs] * mixed


def _pool_layer(x, gain, mod, pool_w, pool_scale, *, seq, tm=512):
    T, D = x.shape
    tpb = seq // tm
    halo = 2 * SUBLANES
    vec = pl.BlockSpec((1, D), lambda i: (0, 0))
    return pl.pallas_call(
        functools.partial(_pool_kernel, tiles_per_batch=tpb, tm=tm),
        out_shape=jax.ShapeDtypeStruct((T, D), f32),
        grid=(T // tm,),
        in_specs=[
            pl.BlockSpec((tm, D), lambda i: (i, 0)),
            pl.BlockSpec((halo, D), lambda i: (jnp.maximum(i * (tm // halo) - 1, 0), 0)),
            vec,
            pl.BlockSpec((1, 1, D), lambda i: (i // tpb, 0, 0)),
            pl.BlockSpec((1, 1, D), lambda i: (i // tpb, 0, 1)),
            pl.BlockSpec((1, 1, D), lambda i: (i // tpb, 0, 2)),
            pl.BlockSpec(pool_w.shape, lambda i: (0, 0, 0)),
            vec,
        ],
        out_specs=pl.BlockSpec((tm, D), lambda i: (i, 0)),
        compiler_params=_cparams(("parallel",)),
        name="pool_mixer",
    )(x, x, gain, mod, mod, mod, pool_w, pool_scale.reshape(1, D))


def _route_kernel(x_ref, g_ref, sh_ref, sc_ref, w_ref, b_ref, h_ref, route_ref):
    h = _ada_norm(x_ref[...], g_ref[...], sh_ref[0], sc_ref[0])
    h_ref[...] = _pack_halves(h)
    h_hi, h_mid, _ = _split3(h)
    w_hi, w_mid, _ = _split3(w_ref[...])
    logits = _dot(h_hi, w_hi) + (_dot(h_hi, w_mid) + _dot(h_mid, w_hi)) + b_ref[...]
    lane = lax.broadcasted_iota(jnp.int32, logits.shape, 1)
    big = jnp.int32(LANES)

    def masked_argmax(mask):
        mx = jnp.max(jnp.where(mask, logits, NEG_BIG), axis=-1, keepdims=True)
        idx = jnp.min(jnp.where(jnp.logical_and(mask, logits == mx), lane, big), axis=-1, keepdims=True)
        return mx, idx

    gmask = lane < N_GROUPS
    gmax, gidx = masked_argmax(gmask)
    grp_w = 1.0 / jnp.sum(jnp.where(gmask, jnp.exp(logits - gmax), 0.0), axis=-1, keepdims=True)
    e_lo = EXP_LANE0 + gidx * EXPERTS_PER_GROUP
    emask = jnp.logical_and(lane >= e_lo, lane < e_lo + EXPERTS_PER_GROUP)
    m1, i1 = masked_argmax(emask)
    m2, i2 = masked_argmax(jnp.logical_and(emask, lane != i1))
    e21 = jnp.exp(m2 - m1)
    w1 = grp_w / (1.0 + e21)
    w2 = grp_w * e21 / (1.0 + e21)
    out = jnp.where(lane == 0, (i1 - EXP_LANE0).astype(f32), 0.0)
    out = jnp.where(lane == 1, (i2 - EXP_LANE0).astype(f32), out)
    out = jnp.where(lane == 2, w1, out)
    out = jnp.where(lane == 3, w2, out)
    route_ref[...] = out


def _route(x, gain, mod, w_router, b_router, *, seq, tm=512):
    T, D = x.shape
    tpb = seq // tm
    vec = pl.BlockSpec((1, D), lambda i: (0, 0))
    return pl.pallas_call(
        _route_kernel,
        out_shape=(jax.ShapeDtypeStruct((T, D // 2), jnp.uint32), jax.ShapeDtypeStruct((T, LANES), f32)),
        grid=(T // tm,),
        in_specs=[
            pl.BlockSpec((tm, D), lambda i: (i, 0)),
            vec,
            pl.BlockSpec((1, 1, D), lambda i: (i // tpb, 0, 0)),
            pl.BlockSpec((1, 1, D), lambda i: (i // tpb, 0, 1)),
            pl.BlockSpec((D, LANES), lambda i: (0, 0)),
            pl.BlockSpec((1, LANES), lambda i: (0, 0)),
        ],
        out_specs=(pl.BlockSpec((tm, D // 2), lambda i: (i, 0)), pl.BlockSpec((tm, LANES), lambda i: (i, 0))),
        compiler_params=_cparams(("parallel",)),
        name="moe_route",
    )(x, gain, mod, mod, w_router, b_router)


def _expert_kernel(be_ref, nu_ref, tok_ref, h_hbm, wg_ref, wu_ref, wd_ref, o_ref, xbuf, sems, xb_s, wg_s, wu_s, wd_s,
                   *, n_blocks):
    i = pl.program_id(0)
    slot = i % 2
    nxt = jnp.minimum(i + 1, n_blocks - 1)

    def row_copy(blk, slot_, r):
        tok = tok_ref[blk * MOE_ROWS + r]
        return pltpu.make_async_copy(h_hbm.at[pl.ds(tok, 1)], xbuf.at[slot_, pl.ds(r, 1)], sems.at[slot_])

    def issue_loop(blk, slot_):
        def body(r, carry):
            row_copy(blk, slot_, r).start()
            return carry
        lax.fori_loop(0, MOE_ROWS, body, 0, unroll=8)

    def wait_block(slot_):
        pltpu.make_async_copy(h_hbm.at[pl.ds(0, MOE_ROWS)], xbuf.at[slot_], sems.at[slot_]).wait()

    @pl.when(i == 0)
    def _():
        issue_loop(0, 0)

    wait_block(slot)
    used = i < nu_ref[0]

    @pl.when(used)
    def _():
        @pl.when(jnp.logical_or(i == 0, be_ref[i] != be_ref[jnp.maximum(i - 1, 0)]))
        def _():
            wg_s[...] = wg_ref[0, 0].astype(bf16)
            wu_s[...] = wu_ref[0, 0].astype(bf16)
            wd_s[...] = wd_ref[0, 0].astype(bf16)

        lo, hi = _unpack_halves(xbuf[slot])
        half = lo.shape[1]
        xb_s[:, :half] = lo.astype(bf16)
        xb_s[:, half:] = hi.astype(bf16)
        for r in range(MOE_ROWS):
            row_copy(nxt, 1 - slot, r).start(priority=r % 2)
        xb = xb_s[...]
        gate = _dot(xb, wg_s[...])
        up = _dot(xb, wu_s[...])
        hid = (gate * _sigmoid(gate) * up).astype(bf16)
        o_ref[...] = _pack_halves(_dot(hid, wd_s[...]))

    @pl.when(jnp.logical_not(used))
    def _():
        issue_loop(nxt, 1 - slot)
        o_ref[...] = jnp.zeros(o_ref.shape, o_ref.dtype)

    @pl.when(i == n_blocks - 1)
    def _():
        wait_block(1 - slot)


def _experts(h, slot_tok, block_e, n_used, w_gate, w_up, w_down, layer):
    DE = w_gate.shape[-1]
    D = w_gate.shape[-2]
    DP = h.shape[1]
    n_blocks = block_e.shape[0]

    def wsel(i, be, nu, tok):
        return (layer, be[jnp.minimum(i, nu[0] - 1)], 0, 0)

    grid_spec = pltpu.PrefetchScalarGridSpec(
        num_scalar_prefetch=3,
        grid=(n_blocks,),
        in_specs=[
            pl.BlockSpec(memory_space=pl.ANY),
            pl.BlockSpec((1, 1, D, DE), wsel),
            pl.BlockSpec((1, 1, D, DE), wsel),
            pl.BlockSpec((1, 1, DE, D), wsel),
        ],
        out_specs=pl.BlockSpec((MOE_ROWS, DP), lambda i, be, nu, tok: (i, 0)),
        scratch_shapes=[pltpu.VMEM((2, MOE_ROWS, DP), jnp.uint32), pltpu.SemaphoreType.DMA((2,)),
                        pltpu.VMEM((MOE_ROWS, D), bf16),
                        pltpu.VMEM((D, DE), bf16), pltpu.VMEM((D, DE), bf16), pltpu.VMEM((DE, D), bf16)],
    )
    return pl.pallas_call(
        functools.partial(_expert_kernel, n_blocks=n_blocks),
        out_shape=jax.ShapeDtypeStruct((n_blocks * MOE_ROWS, DP), jnp.uint32),
        grid_spec=grid_spec,
        compiler_params=_cparams(("arbitrary",)),
        name="moe_experts",
    )(block_e, n_used, slot_tok, h, w_gate, w_up, w_down)


def _combine_kernel(dest_ref, out_hbm, x_ref, route_ref, gate_ref, o_ref, rows, sems, *, tm, n_steps):
    i = pl.program_id(0)

    def issue(step, slot):
        def body(r, carry):
            t = step * tm + r
            for kk in range(TOP_K):
                d = dest_ref[t * TOP_K + kk]
                pltpu.make_async_copy(out_hbm.at[pl.ds(d, 1)], rows.at[slot, kk, pl.ds(r, 1)],
                                      sems.at[slot]).start(priority=kk)
            return carry
        lax.fori_loop(0, tm, body, 0, unroll=4)

    @pl.when(i == 0)
    def _():
        issue(0, 0)

    @pl.when(i + 1 < n_steps)
    def _():
        issue(i + 1, (i + 1) % 2)

    slot = i % 2
    for kk in range(TOP_K):
        pltpu.make_async_copy(out_hbm.at[pl.ds(0, tm)], rows.at[slot, kk], sems.at[slot]).wait()
    route = route_ref[...]
    w0, w1 = route[:, 2:3], route[:, 3:4]
    lo0, hi0 = _unpack_halves(rows[slot, 0])
    lo1, hi1 = _unpack_halves(rows[slot, 1])
    half = lo0.shape[1]
    gate = gate_ref[0]
    o_ref[:, :half] = x_ref[:, :half] + gate[:, :half] * (w0 * lo0 + w1 * lo1)
    o_ref[:, half:] = x_ref[:, half:] + gate[:, half:] * (w0 * hi0 + w1 * hi1)


def _combine(out_buf, dest, x, route, mod, *, seq, tm=512):
    T, D = x.shape
    n_steps = T // tm
    tpb = seq // tm
    grid_spec = pltpu.PrefetchScalarGridSpec(
        num_scalar_prefetch=1,
        grid=(n_steps,),
        in_specs=[
            pl.BlockSpec(memory_space=pl.ANY),
            pl.BlockSpec((tm, D), lambda i, d: (i, 0)),
            pl.BlockSpec((tm, LANES), lambda i, d: (i, 0)),
            pl.BlockSpec((1, 1, D), lambda i, d: (i // tpb, 0, 2)),
        ],
        out_specs=pl.BlockSpec((tm, D), lambda i, d: (i, 0)),
        scratch_shapes=[pltpu.VMEM((2, TOP_K, tm, out_buf.shape[1]), out_buf.dtype), pltpu.SemaphoreType.DMA((2,))],
    )
    return pl.pallas_call(
        functools.partial(_combine_kernel, tm=tm, n_steps=n_steps),
        out_shape=jax.ShapeDtypeStruct((T, D), f32),
        grid_spec=grid_spec,
        compiler_params=_cparams(("arbitrary",)),
        name="moe_combine",
    )(dest, out_buf, x, route, mod)


def _dispatch_plan(ids, n_tokens):
    N = n_tokens * TOP_K
    i32 = jnp.int32
    flat_ids = ids.reshape(-1)
    order = jnp.argsort(flat_ids).astype(i32)
    inv_order = jnp.argsort(order).astype(i32)
    experts = jnp.arange(N_EXPERTS, dtype=i32)
    counts = jnp.sum((flat_ids[:, None] == experts[None, :]).astype(i32), axis=0)
    start = jnp.cumsum(counts) - counts
    padded = ((counts + MOE_ROWS - 1) // MOE_ROWS) * MOE_ROWS
    seg_end = jnp.cumsum(padded).astype(i32)
    pad_start = seg_end - padded
    dest = (pad_start - start)[flat_ids] + inv_order
    n_blocks = -(-N // MOE_ROWS) + N_EXPERTS
    blk_row0 = jnp.arange(n_blocks, dtype=i32) * MOE_ROWS
    block_e = jnp.minimum(jnp.sum((blk_row0[:, None] >= seg_end[None, :]).astype(i32), axis=1), N_EXPERTS - 1)
    n_used = (seg_end[-1:] // MOE_ROWS).astype(i32)
    row = jnp.arange(n_blocks * MOE_ROWS, dtype=i32)
    row_e = jnp.repeat(block_e, MOE_ROWS)
    j = row - pad_start[row_e]
    src = order[jnp.clip(start[row_e] + j, 0, N - 1)] // TOP_K
    slot_tok = jnp.where(j < counts[row_e], src, 0).astype(i32)
    return dest.astype(i32), slot_tok, block_e.astype(i32), n_used


def _moe_layer(x, gain, mod, w_router, b_router, w_gate, w_up, w_down, layer, *, seq):
    T, D = x.shape
    h, route = _route(x, gain, mod, w_router, b_router, seq=seq)
    ids = route[:, :TOP_K].astype(jnp.int32)
    dest, slot_tok, block_e, n_used = _dispatch_plan(ids, T)
    out_buf = _experts(h, slot_tok, block_e, n_used, w_gate, w_up, w_down, layer)
    return _combine(out_buf, dest, x, route, mod, seq=seq)


def _router_params(w_grp, b_grp, w_exp, b_exp):
    D = w_grp.shape[0]
    w = jnp.zeros((D, LANES), f32).at[:, :N_GROUPS].set(w_grp).at[:, EXP_LANE0:EXP_LANE0 + N_EXPERTS].set(w_exp)
    b = jnp.zeros((1, LANES), f32).at[0, :N_GROUPS].set(b_grp).at[0, EXP_LANE0:EXP_LANE0 + N_EXPERTS].set(b_exp)
    return w, b


def _pad_cols(w, n):
    return jnp.zeros((w.shape[0], n), w.dtype).at[:, :w.shape[1]].set(w)


def _pad_rows(w, n):
    return jnp.zeros((n, w.shape[1]), w.dtype).at[:w.shape[0]].set(w)


def kernel(x, c, norm_g, ada_w, ada_b, rel_bias, attn_w_in, attn_w_o, attn_q_gain, attn_k_gain, attn_sinks, rw_mu, rw_w_rkv, rw_w0, rw_w1, rw_w2, rw_a0, rw_a1, rw_a2, rw_g1, rw_g2, rw_k_k, rw_k_a, rw_r_k, rw_lnx_g, rw_lnx_b, rw_w_o, pool_w, pool_scale, moe_w_grp, moe_b_grp, moe_w_exp, moe_b_exp, moe_w_gate, moe_w_up, moe_w_down):
    B, S, D = x.shape
    T = B * S
    xt = x.reshape(T, D)
    mods = _ada_mods(c, ada_w, ada_b)
    bias = _band_bias_masked(rel_bias)
    for layer in range(DEPTH):
        kind, idx = layer % N_MIXERS, layer // N_MIXERS
        gain = norm_g[layer, 0].reshape(1, D)
        mod = mods[2 * layer]
        if kind == 0:
            qkv = _norm_mm(xt, gain, mod, attn_w_in[idx].astype(bf16), seq=S)
            o = _attention(qkv, bias, attn_q_gain[idx], attn_k_gain[idx], attn_sinks[idx], batch=B, seq=S)
            xt = _mm_res(o, attn_w_o[idx].astype(bf16), xt, mod, seq=S)
        elif kind == 1:
            w_all = jnp.concatenate(
                [rw_w_rkv[idx, 0], rw_w_rkv[idx, 1], rw_w_rkv[idx, 2], _pad_cols(rw_w1[idx], LANES),
                 _pad_cols(rw_a1[idx], LANES), _pad_cols(rw_g1[idx], RW_PROJ_TILE - 2 * LANES)], axis=1).astype(bf16)
            proj = _rwkv_proj(xt, gain, mod, rw_mu[idx], w_all, seq=S)
            yg = _wkv(proj, rw_w0[idx], rw_a0[idx],
                      _pad_rows(rw_w2[idx], LANES).astype(bf16), _pad_rows(rw_a2[idx], LANES).astype(bf16),
                      rw_g2[idx].astype(bf16), rw_k_k[idx], rw_k_a[idx], rw_r_k[idx], rw_lnx_g[idx], rw_lnx_b[idx],
                      batch=B, seq=S)
            xt = _mm_res(yg, rw_w_o[idx].astype(bf16), xt, mod, seq=S)
        else:
            xt = _pool_layer(xt, gain, mod, pool_w[idx].astype(bf16), pool_scale[idx], seq=S)
        w_router, b_router = _router_params(moe_w_grp[layer], moe_b_grp[layer], moe_w_exp[layer], moe_b_exp[layer])
        xt = _moe_layer(xt, norm_g[layer, 1].reshape(1, D), mods[2 * layer + 1], w_router, b_router,
                        moe_w_gate, moe_w_up, moe_w_down, layer, seq=S)
    return xt.reshape(B, S, D)
```

```python
import functools
import math

import jax
import jax.numpy as jnp
import numpy as np
from jax import lax
from jax.experimental import pallas as pl
from jax.experimental.pallas import tpu as pltpu

f32 = jnp.float32
bf16 = jnp.bfloat16

D_MODEL = 2048
DEPTH = 4
N_MIXERS = 3
HEAD_DIM = 64
N_HEADS = D_MODEL // HEAD_DIM
N_KV_HEADS = 4
GQA_GROUP = N_HEADS // N_KV_HEADS
WINDOW = 128
Q_DIM = N_HEADS * HEAD_DIM
KV_DIM = N_KV_HEADS * HEAD_DIM
QKV_DIM = Q_DIM + 2 * KV_DIM
N_BUCKETS = 32
MAX_DISTANCE = 128
RWKV_HEAD = 64
GN_EPS = 64e-5
POOL_WINDOWS = (2, 4, 8, 16)
POOL_GROUP = D_MODEL // len(POOL_WINDOWS)
N_GROUPS = 4
EXPERTS_PER_GROUP = 8
N_EXPERTS = N_GROUPS * EXPERTS_PER_GROUP
TOP_K = 2
D_EXPERT = D_MODEL // 4
NORM_EPS = 1e-6

LANES = 128
SUBLANES = 8
VMEM_LIMIT = 56 * 1024 * 1024

MOE_ROWS = 256
WKV_CHUNK = 64
EXP_LANE0 = 32
NEG_BIG = -1e30


def _cparams(sem):
    return pltpu.CompilerParams(dimension_semantics=sem, vmem_limit_bytes=VMEM_LIMIT)


def _ada_norm(xf, gain, shift, scale):
    ms = jnp.mean(xf * xf, axis=-1, keepdims=True)
    return xf * lax.rsqrt(ms + NORM_EPS) * gain * (1.0 + scale) + shift


def _sigmoid(z):
    return 1.0 / (1.0 + jnp.exp(-z))


def _split3(x):
    hi = x.astype(bf16)
    r1 = x - hi.astype(f32)
    mid = r1.astype(bf16)
    lo = (r1 - mid.astype(f32)).astype(bf16)
    return hi, mid, lo


def _pack_halves(x):
    n = x.shape[1] // 2
    lo = pltpu.bitcast(x[:, :n].astype(bf16).astype(f32), jnp.uint32)
    hi = pltpu.bitcast(x[:, n:].astype(bf16).astype(f32), jnp.uint32)
    return (lo >> 16) | hi


def _unpack_halves(u):
    return pltpu.bitcast(u << 16, f32), pltpu.bitcast(u & jnp.uint32(0xFFFF0000), f32)


def _dot(a, b):
    return jnp.dot(a, b, preferred_element_type=f32)


def _dot_nt(a, b):
    return lax.dot_general(a, b, (((1,), (1,)), ((), ())), preferred_element_type=f32)


def _dot_tn(a, b):
    return lax.dot_general(a, b, (((0,), (0,)), ((), ())), preferred_element_type=f32)


def _ada_kernel(c_ref, w_ref, b_ref, o_ref):
    c = c_ref[...]
    ca = (c * _sigmoid(c)).astype(bf16)
    o_ref[0] = _dot(ca, w_ref[0].astype(bf16)) + b_ref[0]


def _ada_mods(c, ada_w, ada_b):
    B, D = c.shape
    n_mod = ada_w.shape[0] * ada_w.shape[1]
    N = ada_w.shape[-1]
    tn = 1024
    c_pad = jnp.zeros((SUBLANES, D), f32).at[:B].set(c)
    out = pl.pallas_call(
        _ada_kernel,
        out_shape=jax.ShapeDtypeStruct((n_mod, SUBLANES, N), f32),
        grid=(n_mod, N // tn),
        in_specs=[
            pl.BlockSpec((SUBLANES, D), lambda m, j: (0, 0)),
            pl.BlockSpec((1, D, tn), lambda m, j: (m, 0, j)),
            pl.BlockSpec((1, 1, tn), lambda m, j: (m, 0, j)),
        ],
        out_specs=pl.BlockSpec((1, SUBLANES, tn), lambda m, j: (m, 0, j)),
        compiler_params=_cparams(("parallel", "parallel")),
        name="ada_mods",
    )(c_pad, ada_w.reshape(n_mod, D, N), ada_b.reshape(n_mod, 1, N))
    return out[:, :B].reshape(n_mod, B, 1, N)


def _lookahead_tile(n_tiles):
    def tile(i, j):
        return jnp.where(jnp.logical_and(i == 0, j == 0), 0, jnp.minimum(i + 1, n_tiles - 1))
    return tile


def _norm_mm_kernel(x_ref, g_ref, sh_ref, sc_ref, w_ref, o_ref, lhs_ref):
    i = pl.program_id(0)
    j = pl.program_id(1)
    last = pl.num_programs(1) - 1
    cur = i % 2

    def normed():
        return _ada_norm(x_ref[...], g_ref[...], sh_ref[0], sc_ref[0]).astype(bf16)

    @pl.when(jnp.logical_and(i == 0, j == 0))
    def _():
        lhs_ref[0] = normed()

    @pl.when(j < last)
    def _():
        o_ref[...] = _dot(lhs_ref[cur], w_ref[...]).astype(o_ref.dtype)

    @pl.when(j == last)
    def _():
        o_ref[...] = _dot(lhs_ref[cur], w_ref[...]).astype(o_ref.dtype)
        lhs_ref[1 - cur] = normed()


def _norm_mm(x, gain, mod, w, *, seq, tm=512, tn=1280):
    T, D = x.shape
    N = w.shape[1]
    tpb = seq // tm
    nxt = _lookahead_tile(T // tm)
    assert N // tn >= 2
    return pl.pallas_call(
        _norm_mm_kernel,
        out_shape=jax.ShapeDtypeStruct((T, N), bf16),
        grid=(T // tm, N // tn),
        in_specs=[
            pl.BlockSpec((tm, D), lambda i, j: (nxt(i, j), 0)),
            pl.BlockSpec((1, D), lambda i, j: (0, 0)),
            pl.BlockSpec((1, 1, D), lambda i, j: (nxt(i, j) // tpb, 0, 0)),
            pl.BlockSpec((1, 1, D), lambda i, j: (nxt(i, j) // tpb, 0, 1)),
            pl.BlockSpec((D, tn), lambda i, j: (0, j)),
        ],
        out_specs=pl.BlockSpec((tm, tn), lambda i, j: (i, j)),
        scratch_shapes=[pltpu.VMEM((2, tm, D), bf16)],
        compiler_params=_cparams(("arbitrary", "arbitrary")),
        name="norm_mm",
    )(x, gain, mod, mod, w)


RW_MIXES = 6
RW_LORA_TILE = 512
RW_PROJ_TILE = 1024


def _rwkv_proj_kernel(x_ref, xh_ref, g_ref, sh_ref, sc_ref, mu_ref, w_ref, o_ref, lhs_ref, *, tiles_per_batch, n_big,
                      chunks_per_mix):
    i = pl.program_id(0)
    j = pl.program_id(1)

    @pl.when(j == 0)
    def _():
        g, sh, sc = g_ref[...], sh_ref[0], sc_ref[0]
        h = _ada_norm(x_ref[...], g, sh, sc)
        h_last = _ada_norm(xh_ref[...], g, sh, sc)[SUBLANES - 1:SUBLANES]
        h_last = jnp.where(i % tiles_per_batch == 0, 0.0, h_last)
        row = lax.broadcasted_iota(jnp.int32, h.shape, 0)
        dh = jnp.where(row == 0, h_last, pltpu.roll(h, 1, axis=0)) - h
        for m in range(RW_MIXES):
            lhs_ref[m] = (h + dh * mu_ref[m:m + 1, :]).astype(bf16)

    @pl.when(j < n_big)
    def _():
        o_ref[...] = _dot(lhs_ref[j // chunks_per_mix], w_ref[...]).astype(o_ref.dtype)

    @pl.when(j == n_big)
    def _():
        q = LANES
        o_ref[:, :RW_LORA_TILE] = jnp.concatenate(
            [_dot(lhs_ref[3], w_ref[:, :q]), _dot(lhs_ref[4], w_ref[:, q:2 * q]),
             _dot(lhs_ref[5], w_ref[:, 2 * q:RW_LORA_TILE])], axis=1).astype(o_ref.dtype)
        o_ref[:, RW_LORA_TILE:] = jnp.zeros((o_ref.shape[0], o_ref.shape[1] - RW_LORA_TILE), o_ref.dtype)


def _rwkv_proj(x, gain, mod, mu, w_all, *, seq, tm=512, tn=RW_PROJ_TILE):
    T, D = x.shape
    n_big = 3 * D // tn
    tpb = seq // tm
    vec_spec = pl.BlockSpec((1, D), lambda i, j: (0, 0))
    return pl.pallas_call(
        functools.partial(_rwkv_proj_kernel, tiles_per_batch=tpb, n_big=n_big, chunks_per_mix=D // tn),
        out_shape=jax.ShapeDtypeStruct((T, w_all.shape[1]), bf16),
        grid=(T // tm, n_big + 1),
        in_specs=[
            pl.BlockSpec((tm, D), lambda i, j: (i, 0)),
            pl.BlockSpec((SUBLANES, D), lambda i, j: (jnp.maximum(i * (tm // SUBLANES) - 1, 0), 0)),
            vec_spec,
            pl.BlockSpec((1, 1, D), lambda i, j: (i // tpb, 0, 0)),
            pl.BlockSpec((1, 1, D), lambda i, j: (i // tpb, 0, 1)),
            pl.BlockSpec((RW_MIXES, D), lambda i, j: (0, 0)),
            pl.BlockSpec((D, tn), lambda i, j: (0, j)),
        ],
        out_specs=pl.BlockSpec((tm, tn), lambda i, j: (i, j)),
        scratch_shapes=[pltpu.VMEM((RW_MIXES, tm, D), bf16)],
        compiler_params=_cparams(("parallel", "arbitrary")),
        name="rwkv_proj",
    )(x, x, gain, mod, mod, mu, w_all)


def _mm_res_kernel(a_ref, w_ref, x_ref, gate_ref, o_ref):
    o_ref[...] = x_ref[...] + gate_ref[0] * _dot(a_ref[...], w_ref[...])


def _mm_res(a, w, x, mod, *, seq, tm=1024, tn=1024):
    T, K = a.shape
    N = w.shape[1]
    tpb = seq // tm
    gate_blk = 2 * (N // tn)
    return pl.pallas_call(
        _mm_res_kernel,
        out_shape=jax.ShapeDtypeStruct((T, N), f32),
        grid=(T // tm, N // tn),
        in_specs=[
            pl.BlockSpec((tm, K), lambda i, j: (i, 0)),
            pl.BlockSpec((K, tn), lambda i, j: (0, j)),
            pl.BlockSpec((tm, tn), lambda i, j: (i, j)),
            pl.BlockSpec((1, 1, tn), lambda i, j: (i // tpb, 0, gate_blk + j)),
        ],
        out_specs=pl.BlockSpec((tm, tn), lambda i, j: (i, j)),
        compiler_params=_cparams(("parallel", "parallel")),
        name="mm_res",
    )(a, w, x, mod)


def _attn_kernel(sink_ref, q_ref, kp_ref, kc_ref, vp_ref, vc_ref, bias_ref, qg_ref, kg_ref, o_ref):
    n = pl.program_id(1)
    W = WINDOW
    GW = GQA_GROUP * W
    heads_per_tile = LANES // HEAD_DIM
    n_qt = Q_DIM // LANES
    qt_per_kv = n_qt // N_KV_HEADS
    lane = lax.broadcasted_iota(jnp.int32, (1, LANES), 1)
    m0 = lane < HEAD_DIM
    ri = lax.broadcasted_iota(jnp.int32, (LANES, LANES), 0)
    ci = lax.broadcasted_iota(jnp.int32, (LANES, LANES), 1)
    head_ones = jnp.where((ri // HEAD_DIM) == (ci // HEAD_DIM), 1.0, 0.0).astype(bf16)

    def inv_rms(t):
        tt = t * t
        hi = tt.astype(bf16)
        lo = (tt - hi.astype(f32)).astype(bf16)
        ssq = _dot(hi, head_ones) + _dot(lo, head_ones)
        return lax.rsqrt(ssq * (1.0 / HEAD_DIM) + NORM_EPS)

    q_stack = jnp.concatenate([q_ref[:, t * LANES:(t + 1) * LANES] for t in range(n_qt)], axis=0).astype(f32)
    q_stack = q_stack * inv_rms(q_stack) * qg_ref[...]
    k_band = jnp.concatenate([kp_ref[...], kc_ref[...]], axis=0).astype(f32)
    v_band = jnp.concatenate([vp_ref[...], vc_ref[...]], axis=0).astype(f32)
    col = lax.broadcasted_iota(jnp.int32, (GW, 2 * W), 1)
    key_ok = jnp.logical_or(n > 0, col >= W)
    grp = lax.broadcasted_iota(jnp.int32, (GW, 1), 0) // W
    ones_kv = jnp.ones((2 * W, LANES), bf16)
    for kt in range(KV_DIM // LANES):
        k_tile = k_band[:, kt * LANES:(kt + 1) * LANES]
        k_tile = k_tile * inv_rms(k_tile) * kg_ref[...]
        k_roll = pltpu.roll(k_tile, HEAD_DIM, axis=1)
        v_tile = v_band[:, kt * LANES:(kt + 1) * LANES]
        v_roll = pltpu.roll(v_tile, HEAD_DIM, axis=1)
        for side in range(heads_per_tile):
            h = kt * heads_per_tile + side
            first = m0 if side == 0 else jnp.logical_not(m0)
            k_dup = jnp.where(first, k_tile, k_roll).astype(bf16)
            v_dup = jnp.where(first, v_tile, v_roll).astype(bf16)
            lhs = []
            for j in range(qt_per_kv):
                q_t = q_stack[(h * qt_per_kv + j) * W:(h * qt_per_kv + j + 1) * W]
                lhs += [jnp.where(m0, q_t, 0.0), jnp.where(m0, 0.0, q_t)]
            lhs = jnp.concatenate(lhs, axis=0).astype(bf16)
            logits = _dot_nt(lhs, k_dup)
            logits = logits + bias_ref[h].reshape(GW, 2 * W)
            logits = jnp.where(key_ok, logits, NEG_BIG)
            sink = jnp.zeros((GW, 1), f32)
            for g in range(GQA_GROUP):
                sink = jnp.where(grp == g, sink_ref[h * GQA_GROUP + g], sink)
            m = jnp.maximum(jnp.max(logits, axis=-1, keepdims=True), sink)
            p = jnp.exp(logits - m).astype(bf16)
            denom = _dot(p, ones_kv) + jnp.exp(sink - m)
            o_full = _dot(p, v_dup) / denom
            for j in range(qt_per_kv):
                t = h * qt_per_kv + j
                o_ref[:, t * LANES:(t + 1) * LANES] = jnp.where(
                    m0, o_full[2 * j * W:(2 * j + 1) * W], o_full[(2 * j + 1) * W:(2 * j + 2) * W]).astype(o_ref.dtype)


def _band_bias_masked(rel_bias):
    max_exact = N_BUCKETS // 2
    i = jnp.arange(WINDOW)[:, None]
    j = jnp.arange(2 * WINDOW)[None, :]
    dist = WINDOW + i - j
    nn = jnp.maximum(dist, 0)
    nf = jnp.maximum(nn, 1).astype(f32)
    large = max_exact + (jnp.log(nf / max_exact) / math.log(MAX_DISTANCE / max_exact)
                         * (N_BUCKETS - max_exact)).astype(jnp.int32)
    large = jnp.minimum(large, N_BUCKETS - 1)
    bucket = jnp.where(nn < max_exact, nn, large)
    b = jnp.transpose(rel_bias[bucket], (2, 0, 1)).astype(f32)
    in_band = (j > i) & (j <= i + WINDOW)
    b = jnp.where(in_band[None], b, NEG_BIG)
    return b.reshape(N_KV_HEADS, GQA_GROUP, WINDOW, 2 * WINDOW)


def _attention(qkv, bias, q_gain, k_gain, sinks, *, batch, seq):
    T = qkv.shape[0]
    nb = seq // WINDOW
    kcol = Q_DIM // KV_DIM
    vcol = kcol + 1

    def cur(col):
        return lambda b, n, s: (b * nb + n, col)

    def prev(col):
        return lambda b, n, s: (b * nb + jnp.maximum(n - 1, 0), col)

    def tile_gain(g):
        return jnp.tile(g.astype(f32), LANES // HEAD_DIM).reshape(1, LANES)

    grid_spec = pltpu.PrefetchScalarGridSpec(
        num_scalar_prefetch=1,
        grid=(batch, nb),
        in_specs=[
            pl.BlockSpec((WINDOW, Q_DIM), lambda b, n, s: (b * nb + n, 0)),
            pl.BlockSpec((WINDOW, KV_DIM), prev(kcol)),
            pl.BlockSpec((WINDOW, KV_DIM), cur(kcol)),
            pl.BlockSpec((WINDOW, KV_DIM), prev(vcol)),
            pl.BlockSpec((WINDOW, KV_DIM), cur(vcol)),
            pl.BlockSpec((N_KV_HEADS, GQA_GROUP, WINDOW, 2 * WINDOW), lambda b, n, s: (0, 0, 0, 0)),
            pl.BlockSpec((1, LANES), lambda b, n, s: (0, 0)),
            pl.BlockSpec((1, LANES), lambda b, n, s: (0, 0)),
        ],
        out_specs=pl.BlockSpec((WINDOW, Q_DIM), lambda b, n, s: (b * nb + n, 0)),
    )
    return pl.pallas_call(
        _attn_kernel,
        out_shape=jax.ShapeDtypeStruct((T, Q_DIM), bf16),
        grid_spec=grid_spec,
        compiler_params=_cparams(("parallel", "parallel")),
        name="swa_attention",
    )(sinks, qkv, qkv, qkv, qkv, qkv, bias, tile_gain(q_gain * (HEAD_DIM ** -0.5)), tile_gain(k_gain))


def _wkv_kernel(r_ref, k_ref, v_ref, lora_ref, w0_ref, a0_ref, w2_ref, a2_ref, g2_ref,
                kk_ref, ka_ref, rk_ref, lng_ref, lnb_ref, o_ref,
                state_ref, w_s, c_s, a_s, g_s, ar_s, bk_s, vb_s, aab_s, aak_s, abk_s, x_s, p_s, xb_s,
                aav_s, uh_s, uv_s, rs_s, *, n_pairs):
    L = WKV_CHUNK
    P2 = 2 * L
    pairs = range(n_pairs)

    @pl.when(pl.program_id(1) == 0)
    def _():
        state_ref[...] = jnp.zeros(state_ref.shape, f32)

    lw = lora_ref[:, :LANES].astype(f32)
    la = lora_ref[:, LANES:2 * LANES]
    lg = lora_ref[:, 2 * LANES:].astype(f32)
    w_lin = w0_ref[...] + _dot(jnp.tanh(lw).astype(bf16), w2_ref[...])
    neg = -w_lin
    softplus = jnp.maximum(neg, 0.0) + jnp.log(1.0 + jnp.exp(-jnp.abs(neg)))
    logw = -jnp.exp(-softplus - 0.5)
    w_s[...] = logw
    a_s[...] = _sigmoid(a0_ref[...] + _dot(la, a2_ref[...]))
    g_s[...] = _dot(_sigmoid(lg).astype(bf16), g2_ref[...])
    ti = lax.broadcasted_iota(jnp.int32, (L, L), 0)
    tj = lax.broadcasted_iota(jnp.int32, (L, L), 1)
    tri = jnp.where(ti >= tj, 1.0, 0.0).astype(bf16)
    hi, mid, lo = _split3(logw)
    c_s[...] = _dot(tri, hi) + _dot(tri, mid) + _dot(tri, lo)

    lane = lax.broadcasted_iota(jnp.int32, (1, LANES), 1)
    m0 = lane < RWKV_HEAD
    ri = lax.broadcasted_iota(jnp.int32, (P2, P2), 0)
    ci = lax.broadcasted_iota(jnp.int32, (P2, P2), 1)
    same = (ri // L) == (ci // L)
    strict = jnp.logical_and(same, ri > ci)
    incl = jnp.logical_and(same, ri >= ci)
    eye = jnp.where(ri == ci, 1.0, 0.0)

    def seg_sum(x):
        s0 = jnp.sum(jnp.where(m0, x, 0.0), axis=-1, keepdims=True)
        s1 = jnp.sum(jnp.where(m0, 0.0, x), axis=-1, keepdims=True)
        return jnp.where(m0, s0, s1)

    def bd(x):
        return jnp.concatenate([jnp.where(m0, x, 0.0), jnp.where(m0, 0.0, x)], axis=0)

    def cols(p):
        return slice(p * LANES, (p + 1) * LANES)

    def k_mod(p):
        return k_ref[:, cols(p)].astype(f32) * (1.0 + (a_s[:, cols(p)] - 1.0) * ka_ref[p])

    for p in pairs:
        cs = cols(p)
        r = r_ref[:, cs].astype(f32)
        k = k_ref[:, cs].astype(f32)
        a = a_s[:, cs]
        cm = c_s[:, cs]
        kk = k * kk_ref[p]
        kk = kk / jnp.maximum(jnp.sqrt(seg_sum(kk * kk)), 1e-12)
        e_pos = jnp.exp(cm)
        e_neg = jnp.exp(-cm)
        e_exc = jnp.exp(cm - w_s[:, cs])
        ar_s[p, :P2] = bd(-kk * e_exc).astype(bf16)
        ar_s[p, P2:] = bd(r * e_pos).astype(bf16)
        bk_s[p, :P2] = bd(kk * a * e_neg).astype(bf16)
        bk_s[p, P2:] = bd(k_mod(p) * e_neg).astype(bf16)
        vb_s[p] = bd(v_ref[:, cs].astype(f32)).astype(bf16)

    for p in pairs:
        G = _dot_nt(ar_s[p], bk_s[p])
        A_ab = jnp.where(strict, G[:P2, :P2], 0.0)
        aab_s[p] = A_ab.astype(bf16)
        x_s[p] = eye + A_ab
        aak_s[p] = jnp.where(strict, G[:P2, P2:], 0.0).astype(bf16)
        abk_s[p, :, :LANES] = jnp.where(incl, G[P2:, :P2], 0.0).astype(bf16)
        abk_s[p, :, LANES:] = jnp.where(incl, G[P2:, P2:], 0.0).astype(bf16)

    for p in pairs:
        p_s[p] = _dot(aab_s[p], aab_s[p]).astype(bf16)
        aav_s[p, :, :LANES] = ar_s[p, :P2]
        aav_s[p, :, LANES:] = _dot(aak_s[p], vb_s[p]).astype(bf16)
    for _ in range(int(math.log2(L)) - 2):
        for p in pairs:
            pw = p_s[p]
            Z = _dot(jnp.concatenate([x_s[p].astype(bf16), pw], axis=0), pw)
            x_s[p] = x_s[p] + Z[:P2]
            p_s[p] = Z[P2:].astype(bf16)
    for p in pairs:
        X = x_s[p]
        xb_s[p] = (X + _dot(X.astype(bf16), p_s[p])).astype(bf16)
    for p in pairs:
        AU = _dot(xb_s[p], aav_s[p])
        ar_s[p, :P2] = AU[:, :LANES].astype(bf16)
        uh_s[p] = AU[:, LANES:]

    for p in pairs:
        T1 = _dot_nt(ar_s[p], state_ref[p].astype(bf16))
        uv_s[p, :P2] = (T1[:P2] + uh_s[p]).astype(bf16)
        uv_s[p, P2:] = vb_s[p]
        rs_s[p] = T1[P2:]

    for p in pairs:
        cs = cols(p)
        Y = rs_s[p] + _dot(abk_s[p], uv_s[p])
        g_last = jnp.exp(c_s[L - 1:L, cs])
        state_ref[p] = (state_ref[p] + _dot_tn(uv_s[p], bk_s[p])) * g_last
        y = Y[:L] + Y[L:]
        mean = seg_sum(y) * (1.0 / RWKV_HEAD)
        yc = y - mean
        var = seg_sum(yc * yc) * (1.0 / RWKV_HEAD)
        yn = yc * lax.rsqrt(var + GN_EPS) * lng_ref[p] + lnb_ref[p]
        r = r_ref[:, cs].astype(f32)
        v = v_ref[:, cs].astype(f32)
        yn = yn + seg_sum(r * k_mod(p) * rk_ref[p]) * v
        o_ref[:, cs] = (yn * g_s[:, cs]).astype(o_ref.dtype)


def _wkv(proj, w0, a0, w2, a2, g2, k_k, k_a, r_k, lnx_g, lnx_b, *, batch, seq):
    T = proj.shape[0]
    D = w0.shape[0]
    L = WKV_CHUNK
    P2 = 2 * L
    n_pairs = D // LANES
    nc = seq // L

    def row(width, col):
        return pl.BlockSpec((L, width), lambda b, t: (b * nc + t, col))

    def full(shape):
        return pl.BlockSpec(shape, lambda b, t: (0,) * len(shape))

    pair_vec = lambda a: a.reshape(n_pairs, 1, LANES).astype(f32)
    pv_spec = full((n_pairs, 1, LANES))
    wide = pltpu.VMEM((L, D), f32)
    sq = lambda dt: pltpu.VMEM((n_pairs, P2, P2), dt)
    tall = pltpu.VMEM((n_pairs, 2 * P2, LANES), bf16)
    wide2 = pltpu.VMEM((n_pairs, P2, 2 * LANES), bf16)
    return pl.pallas_call(
        functools.partial(_wkv_kernel, n_pairs=n_pairs),
        out_shape=jax.ShapeDtypeStruct((T, D), bf16),
        grid=(batch, nc),
        in_specs=[row(D, 0), row(D, 1), row(D, 2), row(RW_LORA_TILE, 3 * D // RW_LORA_TILE),
                  full((1, D)), full((1, D)), full(w2.shape), full(a2.shape), full(g2.shape),
                  pv_spec, pv_spec, pv_spec, pv_spec, pv_spec],
        out_specs=row(D, 0),
        scratch_shapes=[sq(f32), wide, wide, wide, wide, tall, tall, sq(bf16), sq(bf16), sq(bf16), wide2,
                        sq(f32), sq(bf16), sq(bf16), wide2, sq(f32), tall, sq(f32)],
        compiler_params=_cparams(("parallel", "arbitrary")),
        name="wkv7",
    )(proj, proj, proj, proj, w0.reshape(1, D), a0.reshape(1, D), w2, a2, g2,
      pair_vec(k_k), pair_vec(k_a), pair_vec(r_k), pair_vec(lnx_g), pair_vec(lnx_b))


def _pool_kernel(x_ref, xh_ref, g_ref, sh_ref, sc_ref, gate_ref, w_ref, ps_ref, o_ref, *, tiles_per_batch, tm):
    i = pl.program_id(0)
    halo = 2 * SUBLANES
    g, sh, sc = g_ref[...], sh_ref[0], sc_ref[0]
    x = x_ref[...]
    h = _ada_norm(x, g, sh, sc)
    hh = _ada_norm(xh_ref[...], g, sh, sc)
    hh = jnp.where(i % tiles_per_batch == 0, 0.0, hh)
    ext = jnp.concatenate([hh, h], axis=0)
    pos = (i % tiles_per_batch) * tm + lax.broadcasted_iota(jnp.int32, (tm, 1), 0)
    for gi, w in enumerate(POOL_WINDOWS):
        cs = slice(gi * POOL_GROUP, (gi + 1) * POOL_GROUP)
        s = ext[:, cs]
        d = 1
        while d < w:
            s = s + pltpu.roll(s, d, axis=0)
            d *= 2
        cnt = jnp.minimum(pos + 1, w).astype(f32)
        pooled = s[halo:] / cnt - h[:, cs]
        mixed = _dot(pooled.astype(bf16), w_ref[gi]) * ps_ref[:, cs]
        o_ref[:, cs] = x[:, cs] + gate_ref[0][:, cs] * mixed


def _pool_layer(x, gain, mod, pool_w, pool_scale, *, seq, tm=512):
    T, D = x.shape
    tpb = seq // tm
    halo = 2 * SUBLANES
    vec = pl.BlockSpec((1, D), lambda i: (0, 0))
    return pl.pallas_call(
        functools.partial(_pool_kernel, tiles_per_batch=tpb, tm=tm),
        out_shape=jax.ShapeDtypeStruct((T, D), f32),
        grid=(T // tm,),
        in_specs=[
            pl.BlockSpec((tm, D), lambda i: (i, 0)),
            pl.BlockSpec((halo, D), lambda i: (jnp.maximum(i * (tm // halo) - 1, 0), 0)),
            vec,
            pl.BlockSpec((1, 1, D), lambda i: (i // tpb, 0, 0)),
            pl.BlockSpec((1, 1, D), lambda i: (i // tpb, 0, 1)),
            pl.BlockSpec((1, 1, D), lambda i: (i // tpb, 0, 2)),
            pl.BlockSpec(pool_w.shape, lambda i: (0, 0, 0)),
            vec,
        ],
        out_specs=pl.BlockSpec((tm, D), lambda i: (i, 0)),
        compiler_params=_cparams(("parallel",)),
        name="pool_mixer",
    )(x, x, gain, mod, mod, mod, pool_w, pool_scale.reshape(1, D))


def _route_kernel(x_ref, g_ref, sh_ref, sc_ref, w_ref, b_ref, h_ref, route_ref):
    h = _ada_norm(x_ref[...], g_ref[...], sh_ref[0], sc_ref[0])
    h_ref[...] = _pack_halves(h)
    h_hi, h_mid, _ = _split3(h)
    w_hi, w_mid, _ = _split3(w_ref[...])
    logits = _dot(h_hi, w_hi) + (_dot(h_hi, w_mid) + _dot(h_mid, w_hi)) + b_ref[...]
    lane = lax.broadcasted_iota(jnp.int32, logits.shape, 1)
    big = jnp.int32(LANES)

    def masked_argmax(mask):
        mx = jnp.max(jnp.where(mask, logits, NEG_BIG), axis=-1, keepdims=True)
        idx = jnp.min(jnp.where(jnp.logical_and(mask, logits == mx), lane, big), axis=-1, keepdims=True)
        return mx, idx

    gmask = lane < N_GROUPS
    gmax, gidx = masked_argmax(gmask)
    grp_w = 1.0 / jnp.sum(jnp.where(gmask, jnp.exp(logits - gmax), 0.0), axis=-1, keepdims=True)
    e_lo = EXP_LANE0 + gidx * EXPERTS_PER_GROUP
    emask = jnp.logical_and(lane >= e_lo, lane < e_lo + EXPERTS_PER_GROUP)
    m1, i1 = masked_argmax(emask)
    m2, i2 = masked_argmax(jnp.logical_and(emask, lane != i1))
    e21 = jnp.exp(m2 - m1)
    w1 = grp_w / (1.0 + e21)
    w2 = grp_w * e21 / (1.0 + e21)
    out = jnp.where(lane == 0, (i1 - EXP_LANE0).astype(f32), 0.0)
    out = jnp.where(lane == 1, (i2 - EXP_LANE0).astype(f32), out)
    out = jnp.where(lane == 2, w1, out)
    out = jnp.where(lane == 3, w2, out)
    route_ref[...] = out


def _route(x, gain, mod, w_router, b_router, *, seq, tm=512):
    T, D = x.shape
    tpb = seq // tm
    vec = pl.BlockSpec((1, D), lambda i: (0, 0))
    return pl.pallas_call(
        _route_kernel,
        out_shape=(jax.ShapeDtypeStruct((T, D // 2), jnp.uint32), jax.ShapeDtypeStruct((T, LANES), f32)),
        grid=(T // tm,),
        in_specs=[
            pl.BlockSpec((tm, D), lambda i: (i, 0)),
            vec,
            pl.BlockSpec((1, 1, D), lambda i: (i // tpb, 0, 0)),
            pl.BlockSpec((1, 1, D), lambda i: (i // tpb, 0, 1)),
            pl.BlockSpec((D, LANES), lambda i: (0, 0)),
            pl.BlockSpec((1, LANES), lambda i: (0, 0)),
        ],
        out_specs=(pl.BlockSpec((tm, D // 2), lambda i: (i, 0)), pl.BlockSpec((tm, LANES), lambda i: (i, 0))),
        compiler_params=_cparams(("parallel",)),
        name="moe_route",
    )(x, gain, mod, mod, w_router, b_router)


def _expert_kernel(be_ref, cnt_ref, tok_ref, h_hbm, wg_ref, wu_ref, wd_ref, o_ref, xbuf, sems, xb_s, wg_s, wu_s, wd_s,
                   *, n_blocks):
    i = pl.program_id(0)
    slot = i % 2
    def n_rows(blk):
        return pl.multiple_of(((cnt_ref[blk] + SUBLANES - 1) // SUBLANES) * SUBLANES, SUBLANES)

    cnt = n_rows(i)

    def gather(blk, slot_):
        def body(r, carry):
            tok = tok_ref[blk * MOE_ROWS + r]
            pltpu.make_async_copy(h_hbm.at[pl.ds(tok, 1)], xbuf.at[slot_, pl.ds(r, 1)], sems.at[slot_]).start()
            return carry
        lax.fori_loop(0, n_rows(blk), body, 0)

    @pl.when(i == 0)
    def _():
        xbuf[...] = jnp.zeros(xbuf.shape, xbuf.dtype)
        gather(0, 0)

    @pl.when(i + 1 < n_blocks)
    def _():
        gather(i + 1, 1 - slot)

    @pl.when(cnt > 0)
    def _():
        pltpu.make_async_copy(h_hbm.at[pl.ds(0, cnt)], xbuf.at[slot, pl.ds(0, cnt)], sems.at[slot]).wait()

        @pl.when(jnp.logical_or(i == 0, be_ref[i] != be_ref[jnp.maximum(i - 1, 0)]))
        def _():
            wg_s[...] = wg_ref[0, 0].astype(bf16)
            wu_s[...] = wu_ref[0, 0].astype(bf16)
            wd_s[...] = wd_ref[0, 0].astype(bf16)

        lo, hi = _unpack_halves(xbuf[slot])
        half = lo.shape[1]
        xb_s[:, :half] = lo.astype(bf16)
        xb_s[:, half:] = hi.astype(bf16)
        xb = xb_s[...]
        gate = _dot(xb, wg_s[...])
        up = _dot(xb, wu_s[...])
        hid = (gate * _sigmoid(gate) * up).astype(bf16)
        o_ref[...] = _pack_halves(_dot(hid, wd_s[...]))

    @pl.when(cnt == 0)
    def _():
        o_ref[...] = jnp.zeros(o_ref.shape, o_ref.dtype)


def _experts(h, slot_tok, block_e, block_cnt, w_gate, w_up, w_down, layer):
    DE = w_gate.shape[-1]
    D = w_gate.shape[-2]
    DP = h.shape[1]
    n_blocks = block_e.shape[0]

    def wsel(i, be, cnt, tok):
        return (layer, be[i], 0, 0)

    grid_spec = pltpu.PrefetchScalarGridSpec(
        num_scalar_prefetch=3,
        grid=(n_blocks,),
        in_specs=[
            pl.BlockSpec(memory_space=pl.ANY),
            pl.BlockSpec((1, 1, D, DE), wsel),
            pl.BlockSpec((1, 1, D, DE), wsel),
            pl.BlockSpec((1, 1, DE, D), wsel),
        ],
        out_specs=pl.BlockSpec((MOE_ROWS, DP), lambda i, be, cnt, tok: (i, 0)),
        scratch_shapes=[pltpu.VMEM((2, MOE_ROWS, DP), jnp.uint32), pltpu.SemaphoreType.DMA((2,)),
                        pltpu.VMEM((MOE_ROWS, D), bf16),
                        pltpu.VMEM((D, DE), bf16), pltpu.VMEM((D, DE), bf16), pltpu.VMEM((DE, D), bf16)],
    )
    return pl.pallas_call(
        functools.partial(_expert_kernel, n_blocks=n_blocks),
        out_shape=jax.ShapeDtypeStruct((n_blocks * MOE_ROWS, DP), jnp.uint32),
        grid_spec=grid_spec,
        compiler_params=_cparams(("arbitrary",)),
        name="moe_experts",
    )(block_e, block_cnt, slot_tok, h, w_gate, w_up, w_down)


def _combine_kernel(dest_ref, out_hbm, x_ref, route_ref, gate_ref, o_ref, rows, sems, *, tm, n_steps):
    i = pl.program_id(0)

    def issue(step, slot):
        def body(r, carry):
            t = step * tm + r
            for kk in range(TOP_K):
                d = dest_ref[t * TOP_K + kk]
                pltpu.make_async_copy(out_hbm.at[pl.ds(d, 1)], rows.at[slot, kk, pl.ds(r, 1)],
                                      sems.at[slot]).start(priority=kk)
            return carry
        lax.fori_loop(0, tm, body, 0, unroll=4)

    @pl.when(i == 0)
    def _():
        issue(0, 0)

    @pl.when(i + 1 < n_steps)
    def _():
        issue(i + 1, (i + 1) % 2)

    slot = i % 2
    for kk in range(TOP_K):
        pltpu.make_async_copy(out_hbm.at[pl.ds(0, tm)], rows.at[slot, kk], sems.at[slot]).wait()
    route = route_ref[...]
    w0, w1 = route[:, 2:3], route[:, 3:4]
    lo0, hi0 = _unpack_halves(rows[slot, 0])
    lo1, hi1 = _unpack_halves(rows[slot, 1])
    half = lo0.shape[1]
    gate = gate_ref[0]
    o_ref[:, :half] = x_ref[:, :half] + gate[:, :half] * (w0 * lo0 + w1 * lo1)
    o_ref[:, half:] = x_ref[:, half:] + gate[:, half:] * (w0 * hi0 + w1 * hi1)


def _combine(out_buf, dest, x, route, mod, *, seq, tm=512):
    T, D = x.shape
    n_steps = T // tm
    tpb = seq // tm
    grid_spec = pltpu.PrefetchScalarGridSpec(
        num_scalar_prefetch=1,
        grid=(n_steps,),
        in_specs=[
            pl.BlockSpec(memory_space=pl.ANY),
            pl.BlockSpec((tm, D), lambda i, d: (i, 0)),
            pl.BlockSpec((tm, LANES), lambda i, d: (i, 0)),
            pl.BlockSpec((1, 1, D), lambda i, d: (i // tpb, 0, 2)),
        ],
        out_specs=pl.BlockSpec((tm, D), lambda i, d: (i, 0)),
        scratch_shapes=[pltpu.VMEM((2, TOP_K, tm, out_buf.shape[1]), out_buf.dtype), pltpu.SemaphoreType.DMA((2,))],
    )
    return pl.pallas_call(
        functools.partial(_combine_kernel, tm=tm, n_steps=n_steps),
        out_shape=jax.ShapeDtypeStruct((T, D), f32),
        grid_spec=grid_spec,
        compiler_params=_cparams(("arbitrary",)),
        name="moe_combine",
    )(dest, out_buf, x, route, mod)


def _dispatch_plan(ids, n_tokens):
    N = n_tokens * TOP_K
    i32 = jnp.int32
    flat_ids = ids.reshape(-1)
    order = jnp.argsort(flat_ids).astype(i32)
    inv_order = jnp.argsort(order).astype(i32)
    experts = jnp.arange(N_EXPERTS, dtype=i32)
    counts = jnp.sum((flat_ids[:, None] == experts[None, :]).astype(i32), axis=0)
    start = jnp.cumsum(counts) - counts
    padded = ((counts + MOE_ROWS - 1) // MOE_ROWS) * MOE_ROWS
    seg_end = jnp.cumsum(padded).astype(i32)
    pad_start = seg_end - padded
    dest = (pad_start - start)[flat_ids] + inv_order
    n_blocks = -(-N // MOE_ROWS) + N_EXPERTS
    blk_row0 = jnp.arange(n_blocks, dtype=i32) * MOE_ROWS
    block_e = jnp.minimum(jnp.sum((blk_row0[:, None] >= seg_end[None, :]).astype(i32), axis=1), N_EXPERTS - 1)
    block_cnt = jnp.clip(counts[block_e] - (blk_row0 - pad_start[block_e]), 0, MOE_ROWS)
    row = jnp.arange(n_blocks * MOE_ROWS, dtype=i32)
    row_e = jnp.repeat(block_e, MOE_ROWS)
    j = row - pad_start[row_e]
    src = order[jnp.clip(start[row_e] + j, 0, N - 1)] // TOP_K
    slot_tok = jnp.where(j < counts[row_e], src, 0).astype(i32)
    return dest.astype(i32), slot_tok, block_e.astype(i32), block_cnt.astype(i32)


def _moe_layer(x, gain, mod, w_router, b_router, w_gate, w_up, w_down, layer, *, seq):
    T, D = x.shape
    h, route = _route(x, gain, mod, w_router, b_router, seq=seq)
    ids = route[:, :TOP_K].astype(jnp.int32)
    dest, slot_tok, block_e, block_cnt = _dispatch_plan(ids, T)
    out_buf = _experts(h, slot_tok, block_e, block_cnt, w_gate, w_up, w_down, layer)
    return _combine(out_buf, dest, x, route, mod, seq=seq)


def _router_params(w_grp, b_grp, w_exp, b_exp):
    D = w_grp.shape[0]
    w = jnp.zeros((D, LANES), f32).at[:, :N_GROUPS].set(w_grp).at[:, EXP_LANE0:EXP_LANE0 + N_EXPERTS].set(w_exp)
    b = jnp.zeros((1, LANES), f32).at[0, :N_GROUPS].set(b_grp).at[0, EXP_LANE0:EXP_LANE0 + N_EXPERTS].set(b_exp)
    return w, b


def _pad_cols(w, n):
    return jnp.zeros((w.shape[0], n), w.dtype).at[:, :w.shape[1]].set(w)


def _pad_rows(w, n):
    return jnp.zeros((n, w.shape[1]), w.dtype).at[:w.shape[0]].set(w)


def kernel(x, c, norm_g, ada_w, ada_b, rel_bias, attn_w_in, attn_w_o, attn_q_gain, attn_k_gain, attn_sinks, rw_mu, rw_w_rkv, rw_w0, rw_w1, rw_w2, rw_a0, rw_a1, rw_a2, rw_g1, rw_g2, rw_k_k, rw_k_a, rw_r_k, rw_lnx_g, rw_lnx_b, rw_w_o, pool_w, pool_scale, moe_w_grp, moe_b_grp, moe_w_exp, moe_b_exp, moe_w_gate, moe_w_up, moe_w_down):
    B, S, D = x.shape
    T = B * S
    xt = x.reshape(T, D)
    mods = _ada_mods(c, ada_w, ada_b)
    bias = _band_bias_masked(rel_bias)
    for layer in range(DEPTH):
        kind, idx = layer % N_MIXERS, layer // N_MIXERS
        gain = norm_g[layer, 0].reshape(1, D)
        mod = mods[2 * layer]
        if kind == 0:
            qkv = _norm_mm(xt, gain, mod, attn_w_in[idx].astype(bf16), seq=S)
            o = _attention(qkv, bias, attn_q_gain[idx], attn_k_gain[idx], attn_sinks[idx], batch=B, seq=S)
            xt = _mm_res(o, attn_w_o[idx].astype(bf16), xt, mod, seq=S)
        elif kind == 1:
            w_all = jnp.concatenate(
                [rw_w_rkv[idx, 0], rw_w_rkv[idx, 1], rw_w_rkv[idx, 2], _pad_cols(rw_w1[idx], LANES),
                 _pad_cols(rw_a1[idx], LANES), _pad_cols(rw_g1[idx], RW_PROJ_TILE - 2 * LANES)], axis=1).astype(bf16)
            proj = _rwkv_proj(xt, gain, mod, rw_mu[idx], w_all, seq=S)
            yg = _wkv(proj, rw_w0[idx], rw_a0[idx],
                      _pad_rows(rw_w2[idx], LANES).astype(bf16), _pad_rows(rw_a2[idx], LANES).astype(bf16),
                      rw_g2[idx].astype(bf16), rw_k_k[idx], rw_k_a[idx], rw_r_k[idx], rw_lnx_g[idx], rw_lnx_b[idx],
                      batch=B, seq=S)
            xt = _mm_res(yg, rw_w_o[idx].astype(bf16), xt, mod, seq=S)
        else:
            xt = _pool_layer(xt, gain, mod, pool_w[idx].astype(bf16), pool_scale[idx], seq=S)
        w_router, b_router = _router_params(moe_w_grp[layer], moe_b_grp[layer], moe_w_exp[layer], moe_b_exp[layer])
        xt = _moe_layer(xt, norm_g[layer, 1].reshape(1, D), mods[2 * layer + 1], w_router, b_router,
                        moe_w_gate, moe_w_up, moe_w_down, layer, seq=S)
    return xt.reshape(B, S, D)
```

```python
import functools
import math

import jax
import jax.numpy as jnp
import numpy as np
from jax import lax
from jax.experimental import pallas as pl
from jax.experimental.pallas import tpu as pltpu

f32 = jnp.float32
bf16 = jnp.bfloat16

D_MODEL = 2048
DEPTH = 4
N_MIXERS = 3
HEAD_DIM = 64
N_HEADS = D_MODEL // HEAD_DIM
N_KV_HEADS = 4
GQA_GROUP = N_HEADS // N_KV_HEADS
WINDOW = 128
Q_DIM = N_HEADS * HEAD_DIM
KV_DIM = N_KV_HEADS * HEAD_DIM
QKV_DIM = Q_DIM + 2 * KV_DIM
N_BUCKETS = 32
MAX_DISTANCE = 128
RWKV_HEAD = 64
GN_EPS = 64e-5
POOL_WINDOWS = (2, 4, 8, 16)
POOL_GROUP = D_MODEL // len(POOL_WINDOWS)
N_GROUPS = 4
EXPERTS_PER_GROUP = 8
N_EXPERTS = N_GROUPS * EXPERTS_PER_GROUP
TOP_K = 2
D_EXPERT = D_MODEL // 4
NORM_EPS = 1e-6

LANES = 128
SUBLANES = 8
VMEM_LIMIT = 56 * 1024 * 1024

MOE_ROWS = 256
PACE_LINKS = 3
WKV_CHUNK = 64
EXP_LANE0 = 32
NEG_BIG = -1e30


def _cparams(sem):
    return pltpu.CompilerParams(dimension_semantics=sem, vmem_limit_bytes=VMEM_LIMIT)


def _ada_norm(xf, gain, shift, scale):
    ms = jnp.mean(xf * xf, axis=-1, keepdims=True)
    return xf * lax.rsqrt(ms + NORM_EPS) * gain * (1.0 + scale) + shift


def _sigmoid(z):
    return 1.0 / (1.0 + jnp.exp(-z))


def _split3(x):
    hi = x.astype(bf16)
    r1 = x - hi.astype(f32)
    mid = r1.astype(bf16)
    lo = (r1 - mid.astype(f32)).astype(bf16)
    return hi, mid, lo


def _pack_halves(x):
    n = x.shape[1] // 2
    lo = pltpu.bitcast(x[:, :n].astype(bf16).astype(f32), jnp.uint32)
    hi = pltpu.bitcast(x[:, n:].astype(bf16).astype(f32), jnp.uint32)
    return (lo >> 16) | hi


def _unpack_halves(u):
    return pltpu.bitcast(u << 16, f32), pltpu.bitcast(u & jnp.uint32(0xFFFF0000), f32)


def _dot(a, b):
    return jnp.dot(a, b, preferred_element_type=f32)


def _dot_nt(a, b):
    return lax.dot_general(a, b, (((1,), (1,)), ((), ())), preferred_element_type=f32)


def _dot_tn(a, b):
    return lax.dot_general(a, b, (((0,), (0,)), ((), ())), preferred_element_type=f32)


def _ada_kernel(c_ref, w_ref, b_ref, o_ref):
    c = c_ref[...]
    ca = (c * _sigmoid(c)).astype(bf16)
    o_ref[0] = _dot(ca, w_ref[0].astype(bf16)) + b_ref[0]


def _ada_mods(c, ada_w, ada_b):
    B, D = c.shape
    n_mod = ada_w.shape[0] * ada_w.shape[1]
    N = ada_w.shape[-1]
    tn = 1024
    c_pad = jnp.zeros((SUBLANES, D), f32).at[:B].set(c)
    out = pl.pallas_call(
        _ada_kernel,
        out_shape=jax.ShapeDtypeStruct((n_mod, SUBLANES, N), f32),
        grid=(n_mod, N // tn),
        in_specs=[
            pl.BlockSpec((SUBLANES, D), lambda m, j: (0, 0)),
            pl.BlockSpec((1, D, tn), lambda m, j: (m, 0, j)),
            pl.BlockSpec((1, 1, tn), lambda m, j: (m, 0, j)),
        ],
        out_specs=pl.BlockSpec((1, SUBLANES, tn), lambda m, j: (m, 0, j)),
        compiler_params=_cparams(("parallel", "parallel")),
        name="ada_mods",
    )(c_pad, ada_w.reshape(n_mod, D, N), ada_b.reshape(n_mod, 1, N))
    return out[:, :B].reshape(n_mod, B, 1, N)


def _lookahead_tile(n_tiles):
    def tile(i, j):
        return jnp.where(jnp.logical_and(i == 0, j == 0), 0, jnp.minimum(i + 1, n_tiles - 1))
    return tile


def _norm_mm_kernel(x_ref, g_ref, sh_ref, sc_ref, w_ref, o_ref, lhs_ref):
    i = pl.program_id(0)
    j = pl.program_id(1)
    last = pl.num_programs(1) - 1
    cur = i % 2

    def normed():
        return _ada_norm(x_ref[...], g_ref[...], sh_ref[0], sc_ref[0]).astype(bf16)

    @pl.when(jnp.logical_and(i == 0, j == 0))
    def _():
        lhs_ref[0] = normed()

    @pl.when(j < last)
    def _():
        o_ref[...] = _dot(lhs_ref[cur], w_ref[...]).astype(o_ref.dtype)

    @pl.when(j == last)
    def _():
        o_ref[...] = _dot(lhs_ref[cur], w_ref[...]).astype(o_ref.dtype)
        lhs_ref[1 - cur] = normed()


def _norm_mm(x, gain, mod, w, *, seq, tm=512, tn=1280):
    T, D = x.shape
    N = w.shape[1]
    tpb = seq // tm
    nxt = _lookahead_tile(T // tm)
    assert N // tn >= 2
    return pl.pallas_call(
        _norm_mm_kernel,
        out_shape=jax.ShapeDtypeStruct((T, N), bf16),
        grid=(T // tm, N // tn),
        in_specs=[
            pl.BlockSpec((tm, D), lambda i, j: (nxt(i, j), 0)),
            pl.BlockSpec((1, D), lambda i, j: (0, 0)),
            pl.BlockSpec((1, 1, D), lambda i, j: (nxt(i, j) // tpb, 0, 0)),
            pl.BlockSpec((1, 1, D), lambda i, j: (nxt(i, j) // tpb, 0, 1)),
            pl.BlockSpec((D, tn), lambda i, j: (0, j)),
        ],
        out_specs=pl.BlockSpec((tm, tn), lambda i, j: (i, j)),
        scratch_shapes=[pltpu.VMEM((2, tm, D), bf16)],
        compiler_params=_cparams(("arbitrary", "arbitrary")),
        name="norm_mm",
    )(x, gain, mod, mod, w)


RW_MIXES = 6
RW_LORA_TILE = 512
RW_PROJ_TILE = 1024


def _rwkv_proj_kernel(x_ref, xh_ref, g_ref, sh_ref, sc_ref, mu_ref, w_ref, o_ref, lhs_ref, *, tiles_per_batch, n_big,
                      chunks_per_mix):
    i = pl.program_id(0)
    j = pl.program_id(1)

    @pl.when(j == 0)
    def _():
        g, sh, sc = g_ref[...], sh_ref[0], sc_ref[0]
        h = _ada_norm(x_ref[...], g, sh, sc)
        h_last = _ada_norm(xh_ref[...], g, sh, sc)[SUBLANES - 1:SUBLANES]
        h_last = jnp.where(i % tiles_per_batch == 0, 0.0, h_last)
        row = lax.broadcasted_iota(jnp.int32, h.shape, 0)
        dh = jnp.where(row == 0, h_last, pltpu.roll(h, 1, axis=0)) - h
        for m in range(RW_MIXES):
            lhs_ref[m] = (h + dh * mu_ref[m:m + 1, :]).astype(bf16)

    @pl.when(j < n_big)
    def _():
        o_ref[...] = _dot(lhs_ref[j // chunks_per_mix], w_ref[...]).astype(o_ref.dtype)

    @pl.when(j == n_big)
    def _():
        q = LANES
        o_ref[:, :RW_LORA_TILE] = jnp.concatenate(
            [_dot(lhs_ref[3], w_ref[:, :q]), _dot(lhs_ref[4], w_ref[:, q:2 * q]),
             _dot(lhs_ref[5], w_ref[:, 2 * q:RW_LORA_TILE])], axis=1).astype(o_ref.dtype)
        o_ref[:, RW_LORA_TILE:] = jnp.zeros((o_ref.shape[0], o_ref.shape[1] - RW_LORA_TILE), o_ref.dtype)


def _rwkv_proj(x, gain, mod, mu, w_all, *, seq, tm=512, tn=RW_PROJ_TILE):
    T, D = x.shape
    n_big = 3 * D // tn
    tpb = seq // tm
    vec_spec = pl.BlockSpec((1, D), lambda i, j: (0, 0))
    return pl.pallas_call(
        functools.partial(_rwkv_proj_kernel, tiles_per_batch=tpb, n_big=n_big, chunks_per_mix=D // tn),
        out_shape=jax.ShapeDtypeStruct((T, w_all.shape[1]), bf16),
        grid=(T // tm, n_big + 1),
        in_specs=[
            pl.BlockSpec((tm, D), lambda i, j: (i, 0)),
            pl.BlockSpec((SUBLANES, D), lambda i, j: (jnp.maximum(i * (tm // SUBLANES) - 1, 0), 0)),
            vec_spec,
            pl.BlockSpec((1, 1, D), lambda i, j: (i // tpb, 0, 0)),
            pl.BlockSpec((1, 1, D), lambda i, j: (i // tpb, 0, 1)),
            pl.BlockSpec((RW_MIXES, D), lambda i, j: (0, 0)),
            pl.BlockSpec((D, tn), lambda i, j: (0, j)),
        ],
        out_specs=pl.BlockSpec((tm, tn), lambda i, j: (i, j)),
        scratch_shapes=[pltpu.VMEM((RW_MIXES, tm, D), bf16)],
        compiler_params=_cparams(("parallel", "arbitrary")),
        name="rwkv_proj",
    )(x, x, gain, mod, mod, mu, w_all)


def _mm_res_kernel(a_ref, w_ref, x_ref, gate_ref, o_ref):
    o_ref[...] = x_ref[...] + gate_ref[0] * _dot(a_ref[...], w_ref[...])


def _mm_res(a, w, x, mod, *, seq, tm=1024, tn=1024):
    T, K = a.shape
    N = w.shape[1]
    tpb = seq // tm
    gate_blk = 2 * (N // tn)
    return pl.pallas_call(
        _mm_res_kernel,
        out_shape=jax.ShapeDtypeStruct((T, N), f32),
        grid=(T // tm, N // tn),
        in_specs=[
            pl.BlockSpec((tm, K), lambda i, j: (i, 0)),
            pl.BlockSpec((K, tn), lambda i, j: (0, j)),
            pl.BlockSpec((tm, tn), lambda i, j: (i, j)),
            pl.BlockSpec((1, 1, tn), lambda i, j: (i // tpb, 0, gate_blk + j)),
        ],
        out_specs=pl.BlockSpec((tm, tn), lambda i, j: (i, j)),
        compiler_params=_cparams(("parallel", "parallel")),
        name="mm_res",
    )(a, w, x, mod)


def _attn_kernel(sink_ref, q_ref, kp_ref, kc_ref, vp_ref, vc_ref, bias_ref, qg_ref, kg_ref, o_ref):
    n = pl.program_id(1)
    W = WINDOW
    GW = GQA_GROUP * W
    heads_per_tile = LANES // HEAD_DIM
    n_qt = Q_DIM // LANES
    qt_per_kv = n_qt // N_KV_HEADS
    lane = lax.broadcasted_iota(jnp.int32, (1, LANES), 1)
    m0 = lane < HEAD_DIM
    ri = lax.broadcasted_iota(jnp.int32, (LANES, LANES), 0)
    ci = lax.broadcasted_iota(jnp.int32, (LANES, LANES), 1)
    head_ones = jnp.where((ri // HEAD_DIM) == (ci // HEAD_DIM), 1.0, 0.0).astype(bf16)

    def inv_rms(t):
        tt = t * t
        hi = tt.astype(bf16)
        lo = (tt - hi.astype(f32)).astype(bf16)
        ssq = _dot(hi, head_ones) + _dot(lo, head_ones)
        return lax.rsqrt(ssq * (1.0 / HEAD_DIM) + NORM_EPS)

    q_stack = jnp.concatenate([q_ref[:, t * LANES:(t + 1) * LANES] for t in range(n_qt)], axis=0).astype(f32)
    q_stack = q_stack * inv_rms(q_stack) * qg_ref[...]
    k_band = jnp.concatenate([kp_ref[...], kc_ref[...]], axis=0).astype(f32)
    v_band = jnp.concatenate([vp_ref[...], vc_ref[...]], axis=0).astype(f32)
    col = lax.broadcasted_iota(jnp.int32, (GW, 2 * W), 1)
    key_ok = jnp.logical_or(n > 0, col >= W)
    grp = lax.broadcasted_iota(jnp.int32, (GW, 1), 0) // W
    ones_kv = jnp.ones((2 * W, LANES), bf16)
    for kt in range(KV_DIM // LANES):
        k_tile = k_band[:, kt * LANES:(kt + 1) * LANES]
        k_tile = k_tile * inv_rms(k_tile) * kg_ref[...]
        k_roll = pltpu.roll(k_tile, HEAD_DIM, axis=1)
        v_tile = v_band[:, kt * LANES:(kt + 1) * LANES]
        v_roll = pltpu.roll(v_tile, HEAD_DIM, axis=1)
        for side in range(heads_per_tile):
            h = kt * heads_per_tile + side
            first = m0 if side == 0 else jnp.logical_not(m0)
            k_dup = jnp.where(first, k_tile, k_roll).astype(bf16)
            v_dup = jnp.where(first, v_tile, v_roll).astype(bf16)
            lhs = []
            for j in range(qt_per_kv):
                q_t = q_stack[(h * qt_per_kv + j) * W:(h * qt_per_kv + j + 1) * W]
                lhs += [jnp.where(m0, q_t, 0.0), jnp.where(m0, 0.0, q_t)]
            lhs = jnp.concatenate(lhs, axis=0).astype(bf16)
            logits = _dot_nt(lhs, k_dup)
            logits = logits + bias_ref[h].reshape(GW, 2 * W)
            logits = jnp.where(key_ok, logits, NEG_BIG)
            sink = jnp.zeros((GW, 1), f32)
            for g in range(GQA_GROUP):
                sink = jnp.where(grp == g, sink_ref[h * GQA_GROUP + g], sink)
            m = jnp.maximum(jnp.max(logits, axis=-1, keepdims=True), sink)
            p = jnp.exp(logits - m).astype(bf16)
            denom = _dot(p, ones_kv) + jnp.exp(sink - m)
            o_full = _dot(p, v_dup) / denom
            for j in range(qt_per_kv):
                t = h * qt_per_kv + j
                o_ref[:, t * LANES:(t + 1) * LANES] = jnp.where(
                    m0, o_full[2 * j * W:(2 * j + 1) * W], o_full[(2 * j + 1) * W:(2 * j + 2) * W]).astype(o_ref.dtype)


def _band_bias_masked(rel_bias):
    max_exact = N_BUCKETS // 2
    i = jnp.arange(WINDOW)[:, None]
    j = jnp.arange(2 * WINDOW)[None, :]
    dist = WINDOW + i - j
    nn = jnp.maximum(dist, 0)
    nf = jnp.maximum(nn, 1).astype(f32)
    large = max_exact + (jnp.log(nf / max_exact) / math.log(MAX_DISTANCE / max_exact)
                         * (N_BUCKETS - max_exact)).astype(jnp.int32)
    large = jnp.minimum(large, N_BUCKETS - 1)
    bucket = jnp.where(nn < max_exact, nn, large)
    b = jnp.transpose(rel_bias[bucket], (2, 0, 1)).astype(f32)
    in_band = (j > i) & (j <= i + WINDOW)
    b = jnp.where(in_band[None], b, NEG_BIG)
    return b.reshape(N_KV_HEADS, GQA_GROUP, WINDOW, 2 * WINDOW)


def _attention(qkv, bias, q_gain, k_gain, sinks, *, batch, seq):
    T = qkv.shape[0]
    nb = seq // WINDOW
    kcol = Q_DIM // KV_DIM
    vcol = kcol + 1

    def cur(col):
        return lambda b, n, s: (b * nb + n, col)

    def prev(col):
        return lambda b, n, s: (b * nb + jnp.maximum(n - 1, 0), col)

    def tile_gain(g):
        return jnp.tile(g.astype(f32), LANES // HEAD_DIM).reshape(1, LANES)

    grid_spec = pltpu.PrefetchScalarGridSpec(
        num_scalar_prefetch=1,
        grid=(batch, nb),
        in_specs=[
            pl.BlockSpec((WINDOW, Q_DIM), lambda b, n, s: (b * nb + n, 0)),
            pl.BlockSpec((WINDOW, KV_DIM), prev(kcol)),
            pl.BlockSpec((WINDOW, KV_DIM), cur(kcol)),
            pl.BlockSpec((WINDOW, KV_DIM), prev(vcol)),
            pl.BlockSpec((WINDOW, KV_DIM), cur(vcol)),
            pl.BlockSpec((N_KV_HEADS, GQA_GROUP, WINDOW, 2 * WINDOW), lambda b, n, s: (0, 0, 0, 0)),
            pl.BlockSpec((1, LANES), lambda b, n, s: (0, 0)),
            pl.BlockSpec((1, LANES), lambda b, n, s: (0, 0)),
        ],
        out_specs=pl.BlockSpec((WINDOW, Q_DIM), lambda b, n, s: (b * nb + n, 0)),
    )
    return pl.pallas_call(
        _attn_kernel,
        out_shape=jax.ShapeDtypeStruct((T, Q_DIM), bf16),
        grid_spec=grid_spec,
        compiler_params=_cparams(("parallel", "parallel")),
        name="swa_attention",
    )(sinks, qkv, qkv, qkv, qkv, qkv, bias, tile_gain(q_gain * (HEAD_DIM ** -0.5)), tile_gain(k_gain))


def _wkv_kernel(r_ref, k_ref, v_ref, lora_ref, w0_ref, a0_ref, w2_ref, a2_ref, g2_ref,
                kk_ref, ka_ref, rk_ref, lng_ref, lnb_ref, o_ref,
                state_ref, w_s, c_s, a_s, g_s, ar_s, bk_s, vb_s, aab_s, aak_s, abk_s, x_s, p_s, xb_s,
                aav_s, uh_s, uv_s, rs_s, *, n_pairs):
    L = WKV_CHUNK
    P2 = 2 * L
    pairs = range(n_pairs)

    @pl.when(pl.program_id(1) == 0)
    def _():
        state_ref[...] = jnp.zeros(state_ref.shape, f32)

    lw = lora_ref[:, :LANES].astype(f32)
    la = lora_ref[:, LANES:2 * LANES]
    lg = lora_ref[:, 2 * LANES:].astype(f32)
    w_lin = w0_ref[...] + _dot(jnp.tanh(lw).astype(bf16), w2_ref[...])
    neg = -w_lin
    softplus = jnp.maximum(neg, 0.0) + jnp.log(1.0 + jnp.exp(-jnp.abs(neg)))
    logw = -jnp.exp(-softplus - 0.5)
    w_s[...] = logw
    a_s[...] = _sigmoid(a0_ref[...] + _dot(la, a2_ref[...]))
    g_s[...] = _dot(_sigmoid(lg).astype(bf16), g2_ref[...])
    ti = lax.broadcasted_iota(jnp.int32, (L, L), 0)
    tj = lax.broadcasted_iota(jnp.int32, (L, L), 1)
    tri = jnp.where(ti >= tj, 1.0, 0.0).astype(bf16)
    hi, mid, lo = _split3(logw)
    c_s[...] = _dot(tri, hi) + _dot(tri, mid) + _dot(tri, lo)

    lane = lax.broadcasted_iota(jnp.int32, (1, LANES), 1)
    m0 = lane < RWKV_HEAD
    ri = lax.broadcasted_iota(jnp.int32, (P2, P2), 0)
    ci = lax.broadcasted_iota(jnp.int32, (P2, P2), 1)
    same = (ri // L) == (ci // L)
    strict = jnp.logical_and(same, ri > ci)
    incl = jnp.logical_and(same, ri >= ci)
    eye = jnp.where(ri == ci, 1.0, 0.0)

    def seg_sum(x):
        s0 = jnp.sum(jnp.where(m0, x, 0.0), axis=-1, keepdims=True)
        s1 = jnp.sum(jnp.where(m0, 0.0, x), axis=-1, keepdims=True)
        return jnp.where(m0, s0, s1)

    def bd(x):
        return jnp.concatenate([jnp.where(m0, x, 0.0), jnp.where(m0, 0.0, x)], axis=0)

    def cols(p):
        return slice(p * LANES, (p + 1) * LANES)

    def k_mod(p):
        return k_ref[:, cols(p)].astype(f32) * (1.0 + (a_s[:, cols(p)] - 1.0) * ka_ref[p])

    for p in pairs:
        cs = cols(p)
        r = r_ref[:, cs].astype(f32)
        k = k_ref[:, cs].astype(f32)
        a = a_s[:, cs]
        cm = c_s[:, cs]
        kk = k * kk_ref[p]
        kk = kk / jnp.maximum(jnp.sqrt(seg_sum(kk * kk)), 1e-12)
        e_pos = jnp.exp(cm)
        e_neg = jnp.exp(-cm)
        e_exc = jnp.exp(cm - w_s[:, cs])
        ar_s[p, :P2] = bd(-kk * e_exc).astype(bf16)
        ar_s[p, P2:] = bd(r * e_pos).astype(bf16)
        bk_s[p, :P2] = bd(kk * a * e_neg).astype(bf16)
        bk_s[p, P2:] = bd(k_mod(p) * e_neg).astype(bf16)
        vb_s[p] = bd(v_ref[:, cs].astype(f32)).astype(bf16)

    for p in pairs:
        G = _dot_nt(ar_s[p], bk_s[p])
        A_ab = jnp.where(strict, G[:P2, :P2], 0.0)
        aab_s[p] = A_ab.astype(bf16)
        x_s[p] = eye + A_ab
        aak_s[p] = jnp.where(strict, G[:P2, P2:], 0.0).astype(bf16)
        abk_s[p, :, :LANES] = jnp.where(incl, G[P2:, :P2], 0.0).astype(bf16)
        abk_s[p, :, LANES:] = jnp.where(incl, G[P2:, P2:], 0.0).astype(bf16)

    for p in pairs:
        p_s[p] = _dot(aab_s[p], aab_s[p]).astype(bf16)
        aav_s[p, :, :LANES] = ar_s[p, :P2]
        aav_s[p, :, LANES:] = _dot(aak_s[p], vb_s[p]).astype(bf16)
    for _ in range(int(math.log2(L)) - 2):
        for p in pairs:
            pw = p_s[p]
            Z = _dot(jnp.concatenate([x_s[p].astype(bf16), pw], axis=0), pw)
            x_s[p] = x_s[p] + Z[:P2]
            p_s[p] = Z[P2:].astype(bf16)
    for p in pairs:
        X = x_s[p]
        xb_s[p] = (X + _dot(X.astype(bf16), p_s[p])).astype(bf16)
    for p in pairs:
        AU = _dot(xb_s[p], aav_s[p])
        ar_s[p, :P2] = AU[:, :LANES].astype(bf16)
        uh_s[p] = AU[:, LANES:]

    for p in pairs:
        T1 = _dot_nt(ar_s[p], state_ref[p].astype(bf16))
        uv_s[p, :P2] = (T1[:P2] + uh_s[p]).astype(bf16)
        uv_s[p, P2:] = vb_s[p]
        rs_s[p] = T1[P2:]

    for p in pairs:
        cs = cols(p)
        Y = rs_s[p] + _dot(abk_s[p], uv_s[p])
        g_last = jnp.exp(c_s[L - 1:L, cs])
        state_ref[p] = (state_ref[p] + _dot_tn(uv_s[p], bk_s[p])) * g_last
        y = Y[:L] + Y[L:]
        mean = seg_sum(y) * (1.0 / RWKV_HEAD)
        yc = y - mean
        var = seg_sum(yc * yc) * (1.0 / RWKV_HEAD)
        yn = yc * lax.rsqrt(var + GN_EPS) * lng_ref[p] + lnb_ref[p]
        r = r_ref[:, cs].astype(f32)
        v = v_ref[:, cs].astype(f32)
        yn = yn + seg_sum(r * k_mod(p) * rk_ref[p]) * v
        o_ref[:, cs] = (yn * g_s[:, cs]).astype(o_ref.dtype)


def _wkv(proj, w0, a0, w2, a2, g2, k_k, k_a, r_k, lnx_g, lnx_b, *, batch, seq):
    T = proj.shape[0]
    D = w0.shape[0]
    L = WKV_CHUNK
    P2 = 2 * L
    n_pairs = D // LANES
    nc = seq // L

    def row(width, col):
        return pl.BlockSpec((L, width), lambda b, t: (b * nc + t, col))

    def full(shape):
        return pl.BlockSpec(shape, lambda b, t: (0,) * len(shape))

    pair_vec = lambda a: a.reshape(n_pairs, 1, LANES).astype(f32)
    pv_spec = full((n_pairs, 1, LANES))
    wide = pltpu.VMEM((L, D), f32)
    sq = lambda dt: pltpu.VMEM((n_pairs, P2, P2), dt)
    tall = pltpu.VMEM((n_pairs, 2 * P2, LANES), bf16)
    wide2 = pltpu.VMEM((n_pairs, P2, 2 * LANES), bf16)
    return pl.pallas_call(
        functools.partial(_wkv_kernel, n_pairs=n_pairs),
        out_shape=jax.ShapeDtypeStruct((T, D), bf16),
        grid=(batch, nc),
        in_specs=[row(D, 0), row(D, 1), row(D, 2), row(RW_LORA_TILE, 3 * D // RW_LORA_TILE),
                  full((1, D)), full((1, D)), full(w2.shape), full(a2.shape), full(g2.shape),
                  pv_spec, pv_spec, pv_spec, pv_spec, pv_spec],
        out_specs=row(D, 0),
        scratch_shapes=[sq(f32), wide, wide, wide, wide, tall, tall, sq(bf16), sq(bf16), sq(bf16), wide2,
                        sq(f32), sq(bf16), sq(bf16), wide2, sq(f32), tall, sq(f32)],
        compiler_params=_cparams(("parallel", "arbitrary")),
        name="wkv7",
    )(proj, proj, proj, proj, w0.reshape(1, D), a0.reshape(1, D), w2, a2, g2,
      pair_vec(k_k), pair_vec(k_a), pair_vec(r_k), pair_vec(lnx_g), pair_vec(lnx_b))


def _pool_kernel(x_ref, xh_ref, g_ref, sh_ref, sc_ref, gate_ref, w_ref, ps_ref, o_ref, *, tiles_per_batch, tm):
    i = pl.program_id(0)
    halo = 2 * SUBLANES
    g, sh, sc = g_ref[...], sh_ref[0], sc_ref[0]
    x = x_ref[...]
    h = _ada_norm(x, g, sh, sc)
    hh = _ada_norm(xh_ref[...], g, sh, sc)
    hh = jnp.where(i % tiles_per_batch == 0, 0.0, hh)
    ext = jnp.concatenate([hh, h], axis=0)
    pos = (i % tiles_per_batch) * tm + lax.broadcasted_iota(jnp.int32, (tm, 1), 0)
    for gi, w in enumerate(POOL_WINDOWS):
        cs = slice(gi * POOL_GROUP, (gi + 1) * POOL_GROUP)
        s = ext[:, cs]
        d = 1
        while d < w:
            s = s + pltpu.roll(s, d, axis=0)
            d *= 2
        cnt = jnp.minimum(pos + 1, w).astype(f32)
        pooled = s[halo:] / cnt - h[:, cs]
        mixed = _dot(pooled.astype(bf16), w_ref[gi]) * ps_ref[:, cs]
        o_ref[:, cs] = x[:, cs] + gate_ref[0][:, cs] * mixed


def _pool_layer(x, gain, mod, pool_w, pool_scale, *, seq, tm=512):
    T, D = x.shape
    tpb = seq // tm
    halo = 2 * SUBLANES
    vec = pl.BlockSpec((1, D), lambda i: (0, 0))
    return pl.pallas_call(
        functools.partial(_pool_kernel, tiles_per_batch=tpb, tm=tm),
        out_shape=jax.ShapeDtypeStruct((T, D), f32),
        grid=(T // tm,),
        in_specs=[
            pl.BlockSpec((tm, D), lambda i: (i, 0)),
            pl.BlockSpec((halo, D), lambda i: (jnp.maximum(i * (tm // halo) - 1, 0), 0)),
            vec,
            pl.BlockSpec((1, 1, D), lambda i: (i // tpb, 0, 0)),
            pl.BlockSpec((1, 1, D), lambda i: (i // tpb, 0, 1)),
            pl.BlockSpec((1, 1, D), lambda i: (i // tpb, 0, 2)),
            pl.BlockSpec(pool_w.shape, lambda i: (0, 0, 0)),
            vec,
        ],
        out_specs=pl.BlockSpec((tm, D), lambda i: (i, 0)),
        compiler_params=_cparams(("parallel",)),
        name="pool_mixer",
    )(x, x, gain, mod, mod, mod, pool_w, pool_scale.reshape(1, D))


def _route_kernel(x_ref, g_ref, sh_ref, sc_ref, w_ref, b_ref, h_ref, route_ref):
    h = _ada_norm(x_ref[...], g_ref[...], sh_ref[0], sc_ref[0])
    h_ref[...] = _pack_halves(h)
    h_hi, h_mid, _ = _split3(h)
    w_hi, w_mid, _ = _split3(w_ref[...])
    logits = _dot(h_hi, w_hi) + (_dot(h_hi, w_mid) + _dot(h_mid, w_hi)) + b_ref[...]
    lane = lax.broadcasted_iota(jnp.int32, logits.shape, 1)
    big = jnp.int32(LANES)

    def masked_argmax(mask):
        mx = jnp.max(jnp.where(mask, logits, NEG_BIG), axis=-1, keepdims=True)
        idx = jnp.min(jnp.where(jnp.logical_and(mask, logits == mx), lane, big), axis=-1, keepdims=True)
        return mx, idx

    gmask = lane < N_GROUPS
    gmax, gidx = masked_argmax(gmask)
    grp_w = 1.0 / jnp.sum(jnp.where(gmask, jnp.exp(logits - gmax), 0.0), axis=-1, keepdims=True)
    e_lo = EXP_LANE0 + gidx * EXPERTS_PER_GROUP
    emask = jnp.logical_and(lane >= e_lo, lane < e_lo + EXPERTS_PER_GROUP)
    m1, i1 = masked_argmax(emask)
    m2, i2 = masked_argmax(jnp.logical_and(emask, lane != i1))
    e21 = jnp.exp(m2 - m1)
    w1 = grp_w / (1.0 + e21)
    w2 = grp_w * e21 / (1.0 + e21)
    out = jnp.where(lane == 0, (i1 - EXP_LANE0).astype(f32), 0.0)
    out = jnp.where(lane == 1, (i2 - EXP_LANE0).astype(f32), out)
    out = jnp.where(lane == 2, w1, out)
    out = jnp.where(lane == 3, w2, out)
    route_ref[...] = out


def _route(x, gain, mod, w_router, b_router, *, seq, tm=512):
    T, D = x.shape
    tpb = seq // tm
    vec = pl.BlockSpec((1, D), lambda i: (0, 0))
    return pl.pallas_call(
        _route_kernel,
        out_shape=(jax.ShapeDtypeStruct((T, D // 2), jnp.uint32), jax.ShapeDtypeStruct((T, LANES), f32)),
        grid=(T // tm,),
        in_specs=[
            pl.BlockSpec((tm, D), lambda i: (i, 0)),
            vec,
            pl.BlockSpec((1, 1, D), lambda i: (i // tpb, 0, 0)),
            pl.BlockSpec((1, 1, D), lambda i: (i // tpb, 0, 1)),
            pl.BlockSpec((D, LANES), lambda i: (0, 0)),
            pl.BlockSpec((1, LANES), lambda i: (0, 0)),
        ],
        out_specs=(pl.BlockSpec((tm, D // 2), lambda i: (i, 0)), pl.BlockSpec((tm, LANES), lambda i: (i, 0))),
        compiler_params=_cparams(("parallel",)),
        name="moe_route",
    )(x, gain, mod, mod, w_router, b_router)


def _expert_kernel(be_ref, nu_ref, pace_ref, tok_ref, h_hbm, wg_ref, wu_ref, wd_ref, o_ref, xbuf, sems, xb_s, wg_s, wu_s,
                   wd_s, *, n_blocks):
    i = pl.program_id(0)
    slot = i % 2
    nxt = jnp.minimum(i + 1, n_blocks - 1)

    def row_copy(blk, slot_, r):
        tok = tok_ref[blk * MOE_ROWS + r]
        return pltpu.make_async_copy(h_hbm.at[pl.ds(tok, 1)], xbuf.at[slot_, pl.ds(r, 1)], sems.at[slot_])

    def issue_loop(blk, slot_):
        def body(r, carry):
            row_copy(blk, slot_, r).start()
            return carry
        lax.fori_loop(0, MOE_ROWS, body, 0, unroll=8)

    def wait_block(slot_):
        pltpu.make_async_copy(h_hbm.at[pl.ds(0, MOE_ROWS)], xbuf.at[slot_], sems.at[slot_]).wait()

    @pl.when(i == 0)
    def _():
        issue_loop(0, 0)

    wait_block(slot)
    used = i < nu_ref[0]

    @pl.when(used)
    def _():
        @pl.when(jnp.logical_or(i == 0, be_ref[i] != be_ref[jnp.maximum(i - 1, 0)]))
        def _():
            wg_s[...] = wg_ref[0, 0].astype(bf16)
            wu_s[...] = wu_ref[0, 0].astype(bf16)
            wd_s[...] = wd_ref[0, 0].astype(bf16)

        lo, hi = _unpack_halves(xbuf[slot])
        half = lo.shape[1]
        xb_s[:, :half] = lo.astype(bf16)
        xb_s[:, half:] = hi.astype(bf16)
        lag = pace_ref[0]
        for r in range(MOE_ROWS):
            for _ in range(PACE_LINKS):
                lag = lag + (lag >> 31)
            tok = tok_ref[nxt * MOE_ROWS + r + lag]
            pltpu.make_async_copy(h_hbm.at[pl.ds(tok, 1)], xbuf.at[1 - slot, pl.ds(r, 1)],
                                  sems.at[1 - slot]).start(priority=r % 2)
        xb = xb_s[...]
        gate = _dot(xb, wg_s[...])
        up = _dot(xb, wu_s[...])
        hid = (gate * _sigmoid(gate) * up).astype(bf16)
        o_ref[...] = _pack_halves(_dot(hid, wd_s[...]))

    @pl.when(jnp.logical_not(used))
    def _():
        issue_loop(nxt, 1 - slot)
        o_ref[...] = jnp.zeros(o_ref.shape, o_ref.dtype)

    @pl.when(i == n_blocks - 1)
    def _():
        wait_block(1 - slot)


def _experts(h, slot_tok, block_e, n_used, w_gate, w_up, w_down, layer):
    DE = w_gate.shape[-1]
    D = w_gate.shape[-2]
    DP = h.shape[1]
    n_blocks = block_e.shape[0]

    def wsel(i, be, nu, pace, tok):
        return (layer, be[jnp.minimum(i, nu[0] - 1)], 0, 0)

    grid_spec = pltpu.PrefetchScalarGridSpec(
        num_scalar_prefetch=4,
        grid=(n_blocks,),
        in_specs=[
            pl.BlockSpec(memory_space=pl.ANY),
            pl.BlockSpec((1, 1, D, DE), wsel),
            pl.BlockSpec((1, 1, D, DE), wsel),
            pl.BlockSpec((1, 1, DE, D), wsel),
        ],
        out_specs=pl.BlockSpec((MOE_ROWS, DP), lambda i, be, nu, pace, tok: (i, 0)),
        scratch_shapes=[pltpu.VMEM((2, MOE_ROWS, DP), jnp.uint32), pltpu.SemaphoreType.DMA((2,)),
                        pltpu.VMEM((MOE_ROWS, D), bf16),
                        pltpu.VMEM((D, DE), bf16), pltpu.VMEM((D, DE), bf16), pltpu.VMEM((DE, D), bf16)],
    )
    return pl.pallas_call(
        functools.partial(_expert_kernel, n_blocks=n_blocks),
        out_shape=jax.ShapeDtypeStruct((n_blocks * MOE_ROWS, DP), jnp.uint32),
        grid_spec=grid_spec,
        compiler_params=_cparams(("arbitrary",)),
        name="moe_experts",
    )(block_e, n_used, jnp.zeros((SUBLANES,), jnp.int32), slot_tok, h, w_gate, w_up, w_down)


def _combine_kernel(dest_ref, out_hbm, x_ref, route_ref, gate_ref, o_ref, rows, sems, *, tm, n_steps):
    i = pl.program_id(0)

    def issue(step, slot):
        def body(r, carry):
            t = step * tm + r
            for kk in range(TOP_K):
                d = dest_ref[t * TOP_K + kk]
                pltpu.make_async_copy(out_hbm.at[pl.ds(d, 1)], rows.at[slot, kk, pl.ds(r, 1)],
                                      sems.at[slot]).start(priority=kk)
            return carry
        lax.fori_loop(0, tm, body, 0, unroll=4)

    @pl.when(i == 0)
    def _():
        issue(0, 0)

    @pl.when(i + 1 < n_steps)
    def _():
        issue(i + 1, (i + 1) % 2)

    slot = i % 2
    for kk in range(TOP_K):
        pltpu.make_async_copy(out_hbm.at[pl.ds(0, tm)], rows.at[slot, kk], sems.at[slot]).wait()
    route = route_ref[...]
    w0, w1 = route[:, 2:3], route[:, 3:4]
    lo0, hi0 = _unpack_halves(rows[slot, 0])
    lo1, hi1 = _unpack_halves(rows[slot, 1])
    half = lo0.shape[1]
    gate = gate_ref[0]
    o_ref[:, :half] = x_ref[:, :half] + gate[:, :half] * (w0 * lo0 + w1 * lo1)
    o_ref[:, half:] = x_ref[:, half:] + gate[:, half:] * (w0 * hi0 + w1 * hi1)


def _combine(out_buf, dest, x, route, mod, *, seq, tm=512):
    T, D = x.shape
    n_steps = T // tm
    tpb = seq // tm
    grid_spec = pltpu.PrefetchScalarGridSpec(
        num_scalar_prefetch=1,
        grid=(n_steps,),
        in_specs=[
            pl.BlockSpec(memory_space=pl.ANY),
            pl.BlockSpec((tm, D), lambda i, d: (i, 0)),
            pl.BlockSpec((tm, LANES), lambda i, d: (i, 0)),
            pl.BlockSpec((1, 1, D), lambda i, d: (i // tpb, 0, 2)),
        ],
        out_specs=pl.BlockSpec((tm, D), lambda i, d: (i, 0)),
        scratch_shapes=[pltpu.VMEM((2, TOP_K, tm, out_buf.shape[1]), out_buf.dtype), pltpu.SemaphoreType.DMA((2,))],
    )
    return pl.pallas_call(
        functools.partial(_combine_kernel, tm=tm, n_steps=n_steps),
        out_shape=jax.ShapeDtypeStruct((T, D), f32),
        grid_spec=grid_spec,
        compiler_params=_cparams(("arbitrary",)),
        name="moe_combine",
    )(dest, out_buf, x, route, mod)


def _dispatch_plan(ids, n_tokens):
    N = n_tokens * TOP_K
    i32 = jnp.int32
    flat_ids = ids.reshape(-1)
    order = jnp.argsort(flat_ids).astype(i32)
    inv_order = jnp.argsort(order).astype(i32)
    experts = jnp.arange(N_EXPERTS, dtype=i32)
    counts = jnp.sum((flat_ids[:, None] == experts[None, :]).astype(i32), axis=0)
    start = jnp.cumsum(counts) - counts
    padded = ((counts + MOE_ROWS - 1) // MOE_ROWS) * MOE_ROWS
    seg_end = jnp.cumsum(padded).astype(i32)
    pad_start = seg_end - padded
    dest = (pad_start - start)[flat_ids] + inv_order
    n_blocks = -(-N // MOE_ROWS) + N_EXPERTS
    blk_row0 = jnp.arange(n_blocks, dtype=i32) * MOE_ROWS
    block_e = jnp.minimum(jnp.sum((blk_row0[:, None] >= seg_end[None, :]).astype(i32), axis=1), N_EXPERTS - 1)
    n_used = (seg_end[-1:] // MOE_ROWS).astype(i32)
    row = jnp.arange(n_blocks * MOE_ROWS, dtype=i32)
    row_e = jnp.repeat(block_e, MOE_ROWS)
    j = row - pad_start[row_e]
    src = order[jnp.clip(start[row_e] + j, 0, N - 1)] // TOP_K
    slot_tok = jnp.where(j < counts[row_e], src, 0).astype(i32)
    return dest.astype(i32), slot_tok, block_e.astype(i32), n_used


def _moe_layer(x, gain, mod, w_router, b_router, w_gate, w_up, w_down, layer, *, seq):
    T, D = x.shape
    h, route = _route(x, gain, mod, w_router, b_router, seq=seq)
    ids = route[:, :TOP_K].astype(jnp.int32)
    dest, slot_tok, block_e, n_used = _dispatch_plan(ids, T)
    out_buf = _experts(h, slot_tok, block_e, n_used, w_gate, w_up, w_down, layer)
    return _combine(out_buf, dest, x, route, mod, seq=seq)


def _router_params(w_grp, b_grp, w_exp, b_exp):
    D = w_grp.shape[0]
    w = jnp.zeros((D, LANES), f32).at[:, :N_GROUPS].set(w_grp).at[:, EXP_LANE0:EXP_LANE0 + N_EXPERTS].set(w_exp)
    b = jnp.zeros((1, LANES), f32).at[0, :N_GROUPS].set(b_grp).at[0, EXP_LANE0:EXP_LANE0 + N_EXPERTS].set(b_exp)
    return w, b


def _pad_cols(w, n):
    return jnp.zeros((w.shape[0], n), w.dtype).at[:, :w.shape[1]].set(w)


def _pad_rows(w, n):
    return jnp.zeros((n, w.shape[1]), w.dtype).at[:w.shape[0]].set(w)


def kernel(x, c, norm_g, ada_w, ada_b, rel_bias, attn_w_in, attn_w_o, attn_q_gain, attn_k_gain, attn_sinks, rw_mu, rw_w_rkv, rw_w0, rw_w1, rw_w2, rw_a0, rw_a1, rw_a2, rw_g1, rw_g2, rw_k_k, rw_k_a, rw_r_k, rw_lnx_g, rw_lnx_b, rw_w_o, pool_w, pool_scale, moe_w_grp, moe_b_grp, moe_w_exp, moe_b_exp, moe_w_gate, moe_w_up, moe_w_down):
    B, S, D = x.shape
    T = B * S
    xt = x.reshape(T, D)
    mods = _ada_mods(c, ada_w, ada_b)
    bias = _band_bias_masked(rel_bias)
    for layer in range(DEPTH):
        kind, idx = layer % N_MIXERS, layer // N_MIXERS
        gain = norm_g[layer, 0].reshape(1, D)
        mod = mods[2 * layer]
        if kind == 0:
            qkv = _norm_mm(xt, gain, mod, attn_w_in[idx].astype(bf16), seq=S)
            o = _attention(qkv, bias, attn_q_gain[idx], attn_k_gain[idx], attn_sinks[idx], batch=B, seq=S)
            xt = _mm_res(o, attn_w_o[idx].astype(bf16), xt, mod, seq=S)
        elif kind == 1:
            w_all = jnp.concatenate(
                [rw_w_rkv[idx, 0], rw_w_rkv[idx, 1], rw_w_rkv[idx, 2], _pad_cols(rw_w1[idx], LANES),
                 _pad_cols(rw_a1[idx], LANES), _pad_cols(rw_g1[idx], RW_PROJ_TILE - 2 * LANES)], axis=1).astype(bf16)
            proj = _rwkv_proj(xt, gain, mod, rw_mu[idx], w_all, seq=S)
            yg = _wkv(proj, rw_w0[idx], rw_a0[idx],
                      _pad_rows(rw_w2[idx], LANES).astype(bf16), _pad_rows(rw_a2[idx], LANES).astype(bf16),
                      rw_g2[idx].astype(bf16), rw_k_k[idx], rw_k_a[idx], rw_r_k[idx], rw_lnx_g[idx], rw_lnx_b[idx],
                      batch=B, seq=S)
            xt = _mm_res(yg, rw_w_o[idx].astype(bf16), xt, mod, seq=S)
        else:
            xt = _pool_layer(xt, gain, mod, pool_w[idx].astype(bf16), pool_scale[idx], seq=S)
        w_router, b_router = _router_params(moe_w_grp[layer], moe_b_grp[layer], moe_w_exp[layer], moe_b_exp[layer])
        xt = _moe_layer(xt, norm_g[layer, 1].reshape(1, D), mods[2 * layer + 1], w_router, b_router,
                        moe_w_gate, moe_w_up, moe_w_down, layer, seq=S)
    return xt.reshape(B, S, D)
```

```python
import functools
import math

import jax
import jax.numpy as jnp
import numpy as np
from jax import lax
from jax.experimental import pallas as pl
from jax.experimental.pallas import tpu as pltpu

f32 = jnp.float32
bf16 = jnp.bfloat16

D_MODEL = 2048
DEPTH = 4
N_MIXERS = 3
HEAD_DIM = 64
N_HEADS = D_MODEL // HEAD_DIM
N_KV_HEADS = 4
GQA_GROUP = N_HEADS // N_KV_HEADS
WINDOW = 128
Q_DIM = N_HEADS * HEAD_DIM
KV_DIM = N_KV_HEADS * HEAD_DIM
QKV_DIM = Q_DIM + 2 * KV_DIM
N_BUCKETS = 32
MAX_DISTANCE = 128
RWKV_HEAD = 64
GN_EPS = 64e-5
POOL_WINDOWS = (2, 4, 8, 16)
POOL_GROUP = D_MODEL // len(POOL_WINDOWS)
N_GROUPS = 4
EXPERTS_PER_GROUP = 8
N_EXPERTS = N_GROUPS * EXPERTS_PER_GROUP
TOP_K = 2
D_EXPERT = D_MODEL // 4
NORM_EPS = 1e-6

LANES = 128
SUBLANES = 8
VMEM_LIMIT = 56 * 1024 * 1024

MOE_ROWS = 512
WKV_CHUNK = 64
EXP_LANE0 = 32
NEG_BIG = -1e30


def _cparams(sem):
    return pltpu.CompilerParams(dimension_semantics=sem, vmem_limit_bytes=VMEM_LIMIT)


def _ada_norm(xf, gain, shift, scale):
    ms = jnp.mean(xf * xf, axis=-1, keepdims=True)
    return xf * lax.rsqrt(ms + NORM_EPS) * gain * (1.0 + scale) + shift


def _sigmoid(z):
    return 1.0 / (1.0 + jnp.exp(-z))


def _split3(x):
    hi = x.astype(bf16)
    r1 = x - hi.astype(f32)
    mid = r1.astype(bf16)
    lo = (r1 - mid.astype(f32)).astype(bf16)
    return hi, mid, lo


def _pack_halves(x):
    n = x.shape[1] // 2
    lo = pltpu.bitcast(x[:, :n].astype(bf16).astype(f32), jnp.uint32)
    hi = pltpu.bitcast(x[:, n:].astype(bf16).astype(f32), jnp.uint32)
    return (lo >> 16) | hi


def _unpack_halves(u):
    return pltpu.bitcast(u << 16, f32), pltpu.bitcast(u & jnp.uint32(0xFFFF0000), f32)


def _dot(a, b):
    return jnp.dot(a, b, preferred_element_type=f32)


def _dot_nt(a, b):
    return lax.dot_general(a, b, (((1,), (1,)), ((), ())), preferred_element_type=f32)


def _dot_tn(a, b):
    return lax.dot_general(a, b, (((0,), (0,)), ((), ())), preferred_element_type=f32)


def _ada_kernel(c_ref, w_ref, b_ref, o_ref):
    c = c_ref[...]
    ca = (c * _sigmoid(c)).astype(bf16)
    o_ref[0] = _dot(ca, w_ref[0].astype(bf16)) + b_ref[0]


def _ada_mods(c, ada_w, ada_b):
    B, D = c.shape
    n_mod = ada_w.shape[0] * ada_w.shape[1]
    N = ada_w.shape[-1]
    tn = 1024
    c_pad = jnp.zeros((SUBLANES, D), f32).at[:B].set(c)
    out = pl.pallas_call(
        _ada_kernel,
        out_shape=jax.ShapeDtypeStruct((n_mod, SUBLANES, N), f32),
        grid=(n_mod, N // tn),
        in_specs=[
            pl.BlockSpec((SUBLANES, D), lambda m, j: (0, 0)),
            pl.BlockSpec((1, D, tn), lambda m, j: (m, 0, j)),
            pl.BlockSpec((1, 1, tn), lambda m, j: (m, 0, j)),
        ],
        out_specs=pl.BlockSpec((1, SUBLANES, tn), lambda m, j: (m, 0, j)),
        compiler_params=_cparams(("parallel", "parallel")),
        name="ada_mods",
    )(c_pad, ada_w.reshape(n_mod, D, N), ada_b.reshape(n_mod, 1, N))
    return out[:, :B].reshape(n_mod, B, 1, N)


def _lookahead_tile(n_tiles):
    def tile(i, j):
        return jnp.where(jnp.logical_and(i == 0, j == 0), 0, jnp.minimum(i + 1, n_tiles - 1))
    return tile


def _norm_mm_kernel(x_ref, g_ref, sh_ref, sc_ref, w_ref, o_ref, lhs_ref):
    i = pl.program_id(0)
    j = pl.program_id(1)
    last = pl.num_programs(1) - 1
    cur = i % 2

    def normed():
        return _ada_norm(x_ref[...], g_ref[...], sh_ref[0], sc_ref[0]).astype(bf16)

    @pl.when(jnp.logical_and(i == 0, j == 0))
    def _():
        lhs_ref[0] = normed()

    @pl.when(j < last)
    def _():
        o_ref[...] = _dot(lhs_ref[cur], w_ref[...]).astype(o_ref.dtype)

    @pl.when(j == last)
    def _():
        o_ref[...] = _dot(lhs_ref[cur], w_ref[...]).astype(o_ref.dtype)
        lhs_ref[1 - cur] = normed()


def _norm_mm(x, gain, mod, w, *, seq, tm=512, tn=1280):
    T, D = x.shape
    N = w.shape[1]
    tpb = seq // tm
    nxt = _lookahead_tile(T // tm)
    assert N // tn >= 2
    return pl.pallas_call(
        _norm_mm_kernel,
        out_shape=jax.ShapeDtypeStruct((T, N), bf16),
        grid=(T // tm, N // tn),
        in_specs=[
            pl.BlockSpec((tm, D), lambda i, j: (nxt(i, j), 0)),
            pl.BlockSpec((1, D), lambda i, j: (0, 0)),
            pl.BlockSpec((1, 1, D), lambda i, j: (nxt(i, j) // tpb, 0, 0)),
            pl.BlockSpec((1, 1, D), lambda i, j: (nxt(i, j) // tpb, 0, 1)),
            pl.BlockSpec((D, tn), lambda i, j: (0, j)),
        ],
        out_specs=pl.BlockSpec((tm, tn), lambda i, j: (i, j)),
        scratch_shapes=[pltpu.VMEM((2, tm, D), bf16)],
        compiler_params=_cparams(("arbitrary", "arbitrary")),
        name="norm_mm",
    )(x, gain, mod, mod, w)


RW_MIXES = 6
RW_LORA_TILE = 512
RW_PROJ_TILE = 1024
RW_PREP_ROWS = 16


def _rwkv_proj_kernel(x_ref, xh_ref, g_ref, sh_ref, sc_ref, mu_ref, w_ref, o_ref, lhs_ref, *, tiles_per_batch, n_big,
                      chunks_per_mix):
    i = pl.program_id(0)
    j = pl.program_id(1)

    @pl.when(j == 0)
    def _():
        g, sh, sc = g_ref[...], sh_ref[0], sc_ref[0]
        h_last = _ada_norm(xh_ref[...], g, sh, sc)[SUBLANES - 1:SUBLANES]
        h_last = jnp.where(i % tiles_per_batch == 0, 0.0, h_last)
        rows = RW_PREP_ROWS
        row = lax.broadcasted_iota(jnp.int32, (rows, x_ref.shape[1]), 0)

        def prep(c, prev):
            r0 = pl.multiple_of(c * rows, rows)
            h = _ada_norm(x_ref[pl.ds(r0, rows), :], g, sh, sc)
            dh = jnp.where(row == 0, prev, pltpu.roll(h, 1, axis=0)) - h
            for m in range(RW_MIXES):
                lhs_ref[m, pl.ds(r0, rows), :] = (h + dh * mu_ref[m:m + 1, :]).astype(bf16)
            return h[rows - 1:rows]

        lax.fori_loop(0, x_ref.shape[0] // rows, prep, h_last)

    @pl.when(j < n_big)
    def _():
        o_ref[...] = _dot(lhs_ref[j // chunks_per_mix], w_ref[...]).astype(o_ref.dtype)

    @pl.when(j == n_big)
    def _():
        q = LANES
        o_ref[:, :RW_LORA_TILE] = jnp.concatenate(
            [_dot(lhs_ref[3], w_ref[:, :q]), _dot(lhs_ref[4], w_ref[:, q:2 * q]),
             _dot(lhs_ref[5], w_ref[:, 2 * q:RW_LORA_TILE])], axis=1).astype(o_ref.dtype)
        o_ref[:, RW_LORA_TILE:] = jnp.zeros((o_ref.shape[0], o_ref.shape[1] - RW_LORA_TILE), o_ref.dtype)


def _rwkv_proj(x, gain, mod, mu, w_all, *, seq, tm=512, tn=RW_PROJ_TILE):
    T, D = x.shape
    n_big = 3 * D // tn
    tpb = seq // tm
    vec_spec = pl.BlockSpec((1, D), lambda i, j: (0, 0))
    return pl.pallas_call(
        functools.partial(_rwkv_proj_kernel, tiles_per_batch=tpb, n_big=n_big, chunks_per_mix=D // tn),
        out_shape=jax.ShapeDtypeStruct((T, w_all.shape[1]), bf16),
        grid=(T // tm, n_big + 1),
        in_specs=[
            pl.BlockSpec((tm, D), lambda i, j: (i, 0)),
            pl.BlockSpec((SUBLANES, D), lambda i, j: (jnp.maximum(i * (tm // SUBLANES) - 1, 0), 0)),
            vec_spec,
            pl.BlockSpec((1, 1, D), lambda i, j: (i // tpb, 0, 0)),
            pl.BlockSpec((1, 1, D), lambda i, j: (i // tpb, 0, 1)),
            pl.BlockSpec((RW_MIXES, D), lambda i, j: (0, 0)),
            pl.BlockSpec((D, tn), lambda i, j: (0, j)),
        ],
        out_specs=pl.BlockSpec((tm, tn), lambda i, j: (i, j)),
        scratch_shapes=[pltpu.VMEM((RW_MIXES, tm, D), bf16)],
        compiler_params=_cparams(("parallel", "arbitrary")),
        name="rwkv_proj",
    )(x, x, gain, mod, mod, mu, w_all)


def _mm_res_kernel(a_ref, w_ref, x_ref, gate_ref, o_ref):
    o_ref[...] = x_ref[...] + gate_ref[0] * _dot(a_ref[...], w_ref[...])


def _mm_res(a, w, x, mod, *, seq, tm=1024, tn=1024):
    T, K = a.shape
    N = w.shape[1]
    tpb = seq // tm
    gate_blk = 2 * (N // tn)
    return pl.pallas_call(
        _mm_res_kernel,
        out_shape=jax.ShapeDtypeStruct((T, N), f32),
        grid=(T // tm, N // tn),
        in_specs=[
            pl.BlockSpec((tm, K), lambda i, j: (i, 0)),
            pl.BlockSpec((K, tn), lambda i, j: (0, j)),
            pl.BlockSpec((tm, tn), lambda i, j: (i, j)),
            pl.BlockSpec((1, 1, tn), lambda i, j: (i // tpb, 0, gate_blk + j)),
        ],
        out_specs=pl.BlockSpec((tm, tn), lambda i, j: (i, j)),
        compiler_params=_cparams(("parallel", "parallel")),
        name="mm_res",
    )(a, w, x, mod)


def _attn_kernel(sink_ref, q_ref, kp_ref, kc_ref, vp_ref, vc_ref, bias_ref, qg_ref, kg_ref, o_ref):
    n = pl.program_id(1)
    W = WINDOW
    GW = GQA_GROUP * W
    heads_per_tile = LANES // HEAD_DIM
    n_qt = Q_DIM // LANES
    qt_per_kv = n_qt // N_KV_HEADS
    lane = lax.broadcasted_iota(jnp.int32, (1, LANES), 1)
    m0 = lane < HEAD_DIM
    ri = lax.broadcasted_iota(jnp.int32, (LANES, LANES), 0)
    ci = lax.broadcasted_iota(jnp.int32, (LANES, LANES), 1)
    head_ones = jnp.where((ri // HEAD_DIM) == (ci // HEAD_DIM), 1.0, 0.0).astype(bf16)

    def inv_rms(t):
        tt = t * t
        hi = tt.astype(bf16)
        lo = (tt - hi.astype(f32)).astype(bf16)
        ssq = _dot(hi, head_ones) + _dot(lo, head_ones)
        return lax.rsqrt(ssq * (1.0 / HEAD_DIM) + NORM_EPS)

    q_stack = jnp.concatenate([q_ref[:, t * LANES:(t + 1) * LANES] for t in range(n_qt)], axis=0).astype(f32)
    q_stack = q_stack * inv_rms(q_stack) * qg_ref[...]
    k_band = jnp.concatenate([kp_ref[...], kc_ref[...]], axis=0).astype(f32)
    v_band = jnp.concatenate([vp_ref[...], vc_ref[...]], axis=0).astype(f32)
    col = lax.broadcasted_iota(jnp.int32, (GW, 2 * W), 1)
    key_ok = jnp.logical_or(n > 0, col >= W)
    grp = lax.broadcasted_iota(jnp.int32, (GW, 1), 0) // W
    ones_kv = jnp.ones((2 * W, LANES), bf16)
    for kt in range(KV_DIM // LANES):
        k_tile = k_band[:, kt * LANES:(kt + 1) * LANES]
        k_tile = k_tile * inv_rms(k_tile) * kg_ref[...]
        k_roll = pltpu.roll(k_tile, HEAD_DIM, axis=1)
        v_tile = v_band[:, kt * LANES:(kt + 1) * LANES]
        v_roll = pltpu.roll(v_tile, HEAD_DIM, axis=1)
        for side in range(heads_per_tile):
            h = kt * heads_per_tile + side
            first = m0 if side == 0 else jnp.logical_not(m0)
            k_dup = jnp.where(first, k_tile, k_roll).astype(bf16)
            v_dup = jnp.where(first, v_tile, v_roll).astype(bf16)
            lhs = []
            for j in range(qt_per_kv):
                q_t = q_stack[(h * qt_per_kv + j) * W:(h * qt_per_kv + j + 1) * W]
                lhs += [jnp.where(m0, q_t, 0.0), jnp.where(m0, 0.0, q_t)]
            lhs = jnp.concatenate(lhs, axis=0).astype(bf16)
            logits = _dot_nt(lhs, k_dup)
            logits = logits + bias_ref[h].reshape(GW, 2 * W)
            logits = jnp.where(key_ok, logits, NEG_BIG)
            sink = jnp.zeros((GW, 1), f32)
            for g in range(GQA_GROUP):
                sink = jnp.where(grp == g, sink_ref[h * GQA_GROUP + g], sink)
            m = jnp.maximum(jnp.max(logits, axis=-1, keepdims=True), sink)
            p = jnp.exp(logits - m).astype(bf16)
            denom = _dot(p, ones_kv) + jnp.exp(sink - m)
            o_full = _dot(p, v_dup) / denom
            for j in range(qt_per_kv):
                t = h * qt_per_kv + j
                o_ref[:, t * LANES:(t + 1) * LANES] = jnp.where(
                    m0, o_full[2 * j * W:(2 * j + 1) * W], o_full[(2 * j + 1) * W:(2 * j + 2) * W]).astype(o_ref.dtype)


def _band_bias_masked(rel_bias):
    max_exact = N_BUCKETS // 2
    i = jnp.arange(WINDOW)[:, None]
    j = jnp.arange(2 * WINDOW)[None, :]
    dist = WINDOW + i - j
    nn = jnp.maximum(dist, 0)
    nf = jnp.maximum(nn, 1).astype(f32)
    large = max_exact + (jnp.log(nf / max_exact) / math.log(MAX_DISTANCE / max_exact)
                         * (N_BUCKETS - max_exact)).astype(jnp.int32)
    large = jnp.minimum(large, N_BUCKETS - 1)
    bucket = jnp.where(nn < max_exact, nn, large)
    b = jnp.transpose(rel_bias[bucket], (2, 0, 1)).astype(f32)
    in_band = (j > i) & (j <= i + WINDOW)
    b = jnp.where(in_band[None], b, NEG_BIG)
    return b.reshape(N_KV_HEADS, GQA_GROUP, WINDOW, 2 * WINDOW)


def _attention(qkv, bias, q_gain, k_gain, sinks, *, batch, seq):
    T = qkv.shape[0]
    nb = seq // WINDOW
    kcol = Q_DIM // KV_DIM
    vcol = kcol + 1

    def cur(col):
        return lambda b, n, s: (b * nb + n, col)

    def prev(col):
        return lambda b, n, s: (b * nb + jnp.maximum(n - 1, 0), col)

    def tile_gain(g):
        return jnp.tile(g.astype(f32), LANES // HEAD_DIM).reshape(1, LANES)

    grid_spec = pltpu.PrefetchScalarGridSpec(
        num_scalar_prefetch=1,
        grid=(batch, nb),
        in_specs=[
            pl.BlockSpec((WINDOW, Q_DIM), lambda b, n, s: (b * nb + n, 0)),
            pl.BlockSpec((WINDOW, KV_DIM), prev(kcol)),
            pl.BlockSpec((WINDOW, KV_DIM), cur(kcol)),
            pl.BlockSpec((WINDOW, KV_DIM), prev(vcol)),
            pl.BlockSpec((WINDOW, KV_DIM), cur(vcol)),
            pl.BlockSpec((N_KV_HEADS, GQA_GROUP, WINDOW, 2 * WINDOW), lambda b, n, s: (0, 0, 0, 0)),
            pl.BlockSpec((1, LANES), lambda b, n, s: (0, 0)),
            pl.BlockSpec((1, LANES), lambda b, n, s: (0, 0)),
        ],
        out_specs=pl.BlockSpec((WINDOW, Q_DIM), lambda b, n, s: (b * nb + n, 0)),
    )
    return pl.pallas_call(
        _attn_kernel,
        out_shape=jax.ShapeDtypeStruct((T, Q_DIM), bf16),
        grid_spec=grid_spec,
        compiler_params=_cparams(("parallel", "parallel")),
        name="swa_attention",
    )(sinks, qkv, qkv, qkv, qkv, qkv, bias, tile_gain(q_gain * (HEAD_DIM ** -0.5)), tile_gain(k_gain))


def _wkv_kernel(r_ref, k_ref, v_ref, lora_ref, w0_ref, a0_ref, w2_ref, a2_ref, g2_ref,
                kk_ref, ka_ref, rk_ref, lng_ref, lnb_ref, o_ref,
                state_ref, w_s, c_s, a_s, g_s, ar_s, bk_s, vb_s, aab_s, aak_s, abk_s, x_s, p_s, xb_s,
                aav_s, uh_s, uv_s, rs_s, *, n_pairs):
    L = WKV_CHUNK
    P2 = 2 * L
    pairs = range(n_pairs)

    @pl.when(pl.program_id(1) == 0)
    def _():
        state_ref[...] = jnp.zeros(state_ref.shape, f32)

    lw = lora_ref[:, :LANES].astype(f32)
    la = lora_ref[:, LANES:2 * LANES]
    lg = lora_ref[:, 2 * LANES:].astype(f32)
    w_lin = w0_ref[...] + _dot(jnp.tanh(lw).astype(bf16), w2_ref[...])
    neg = -w_lin
    softplus = jnp.maximum(neg, 0.0) + jnp.log(1.0 + jnp.exp(-jnp.abs(neg)))
    logw = -jnp.exp(-softplus - 0.5)
    w_s[...] = logw
    a_s[...] = _sigmoid(a0_ref[...] + _dot(la, a2_ref[...]))
    g_s[...] = _dot(_sigmoid(lg).astype(bf16), g2_ref[...])
    ti = lax.broadcasted_iota(jnp.int32, (L, L), 0)
    tj = lax.broadcasted_iota(jnp.int32, (L, L), 1)
    tri = jnp.where(ti >= tj, 1.0, 0.0).astype(bf16)
    hi, mid, lo = _split3(logw)
    c_s[...] = _dot(tri, hi) + _dot(tri, mid) + _dot(tri, lo)

    lane = lax.broadcasted_iota(jnp.int32, (1, LANES), 1)
    m0 = lane < RWKV_HEAD
    ri = lax.broadcasted_iota(jnp.int32, (P2, P2), 0)
    ci = lax.broadcasted_iota(jnp.int32, (P2, P2), 1)
    same = (ri // L) == (ci // L)
    strict = jnp.logical_and(same, ri > ci)
    incl = jnp.logical_and(same, ri >= ci)
    eye = jnp.where(ri == ci, 1.0, 0.0)

    def seg_sum(x):
        s0 = jnp.sum(jnp.where(m0, x, 0.0), axis=-1, keepdims=True)
        s1 = jnp.sum(jnp.where(m0, 0.0, x), axis=-1, keepdims=True)
        return jnp.where(m0, s0, s1)

    def bd(x):
        return jnp.concatenate([jnp.where(m0, x, 0.0), jnp.where(m0, 0.0, x)], axis=0)

    def cols(p):
        return slice(p * LANES, (p + 1) * LANES)

    def k_mod(p):
        return k_ref[:, cols(p)].astype(f32) * (1.0 + (a_s[:, cols(p)] - 1.0) * ka_ref[p])

    for p in pairs:
        cs = cols(p)
        r = r_ref[:, cs].astype(f32)
        k = k_ref[:, cs].astype(f32)
        a = a_s[:, cs]
        cm = c_s[:, cs]
        kk = k * kk_ref[p]
        kk = kk / jnp.maximum(jnp.sqrt(seg_sum(kk * kk)), 1e-12)
        e_pos = jnp.exp(cm)
        e_neg = jnp.exp(-cm)
        e_exc = jnp.exp(cm - w_s[:, cs])
        ar_s[p, :P2] = bd(-kk * e_exc).astype(bf16)
        ar_s[p, P2:] = bd(r * e_pos).astype(bf16)
        bk_s[p, :P2] = bd(kk * a * e_neg).astype(bf16)
        bk_s[p, P2:] = bd(k_mod(p) * e_neg).astype(bf16)
        vb_s[p] = bd(v_ref[:, cs].astype(f32)).astype(bf16)

    for p in pairs:
        G = _dot_nt(ar_s[p], bk_s[p])
        A_ab = jnp.where(strict, G[:P2, :P2], 0.0)
        aab_s[p] = A_ab.astype(bf16)
        x_s[p] = eye + A_ab
        aak_s[p] = jnp.where(strict, G[:P2, P2:], 0.0).astype(bf16)
        abk_s[p, :, :LANES] = jnp.where(incl, G[P2:, :P2], 0.0).astype(bf16)
        abk_s[p, :, LANES:] = jnp.where(incl, G[P2:, P2:], 0.0).astype(bf16)

    for p in pairs:
        p_s[p] = _dot(aab_s[p], aab_s[p]).astype(bf16)
        aav_s[p, :, :LANES] = ar_s[p, :P2]
        aav_s[p, :, LANES:] = _dot(aak_s[p], vb_s[p]).astype(bf16)
    for _ in range(int(math.log2(L)) - 2):
        for p in pairs:
            pw = p_s[p]
            Z = _dot(jnp.concatenate([x_s[p].astype(bf16), pw], axis=0), pw)
            x_s[p] = x_s[p] + Z[:P2]
            p_s[p] = Z[P2:].astype(bf16)
    for p in pairs:
        X = x_s[p]
        xb_s[p] = (X + _dot(X.astype(bf16), p_s[p])).astype(bf16)
    for p in pairs:
        AU = _dot(xb_s[p], aav_s[p])
        ar_s[p, :P2] = AU[:, :LANES].astype(bf16)
        uh_s[p] = AU[:, LANES:]

    for p in pairs:
        T1 = _dot_nt(ar_s[p], state_ref[p].astype(bf16))
        uv_s[p, :P2] = (T1[:P2] + uh_s[p]).astype(bf16)
        uv_s[p, P2:] = vb_s[p]
        rs_s[p] = T1[P2:]

    for p in pairs:
        cs = cols(p)
        Y = rs_s[p] + _dot(abk_s[p], uv_s[p])
        g_last = jnp.exp(c_s[L - 1:L, cs])
        state_ref[p] = (state_ref[p] + _dot_tn(uv_s[p], bk_s[p])) * g_last
        y = Y[:L] + Y[L:]
        mean = seg_sum(y) * (1.0 / RWKV_HEAD)
        yc = y - mean
        var = seg_sum(yc * yc) * (1.0 / RWKV_HEAD)
        yn = yc * lax.rsqrt(var + GN_EPS) * lng_ref[p] + lnb_ref[p]
        r = r_ref[:, cs].astype(f32)
        v = v_ref[:, cs].astype(f32)
        yn = yn + seg_sum(r * k_mod(p) * rk_ref[p]) * v
        o_ref[:, cs] = (yn * g_s[:, cs]).astype(o_ref.dtype)


def _wkv(proj, w0, a0, w2, a2, g2, k_k, k_a, r_k, lnx_g, lnx_b, *, batch, seq):
    T = proj.shape[0]
    D = w0.shape[0]
    L = WKV_CHUNK
    P2 = 2 * L
    n_pairs = D // LANES
    nc = seq // L

    def row(width, col):
        return pl.BlockSpec((L, width), lambda b, t: (b * nc + t, col))

    def full(shape):
        return pl.BlockSpec(shape, lambda b, t: (0,) * len(shape))

    pair_vec = lambda a: a.reshape(n_pairs, 1, LANES).astype(f32)
    pv_spec = full((n_pairs, 1, LANES))
    wide = pltpu.VMEM((L, D), f32)
    sq = lambda dt: pltpu.VMEM((n_pairs, P2, P2), dt)
    tall = pltpu.VMEM((n_pairs, 2 * P2, LANES), bf16)
    wide2 = pltpu.VMEM((n_pairs, P2, 2 * LANES), bf16)
    return pl.pallas_call(
        functools.partial(_wkv_kernel, n_pairs=n_pairs),
        out_shape=jax.ShapeDtypeStruct((T, D), bf16),
        grid=(batch, nc),
        in_specs=[row(D, 0), row(D, 1), row(D, 2), row(RW_LORA_TILE, 3 * D // RW_LORA_TILE),
                  full((1, D)), full((1, D)), full(w2.shape), full(a2.shape), full(g2.shape),
                  pv_spec, pv_spec, pv_spec, pv_spec, pv_spec],
        out_specs=row(D, 0),
        scratch_shapes=[sq(f32), wide, wide, wide, wide, tall, tall, sq(bf16), sq(bf16), sq(bf16), wide2,
                        sq(f32), sq(bf16), sq(bf16), wide2, sq(f32), tall, sq(f32)],
        compiler_params=_cparams(("parallel", "arbitrary")),
        name="wkv7",
    )(proj, proj, proj, proj, w0.reshape(1, D), a0.reshape(1, D), w2, a2, g2,
      pair_vec(k_k), pair_vec(k_a), pair_vec(r_k), pair_vec(lnx_g), pair_vec(lnx_b))


def _pool_kernel(x_ref, xh_ref, g_ref, sh_ref, sc_ref, gate_ref, w_ref, ps_ref, o_ref, *, tiles_per_batch, tm):
    i = pl.program_id(0)
    halo = 2 * SUBLANES
    g, sh, sc = g_ref[...], sh_ref[0], sc_ref[0]
    x = x_ref[...]
    h = _ada_norm(x, g, sh, sc)
    hh = _ada_norm(xh_ref[...], g, sh, sc)
    hh = jnp.where(i % tiles_per_batch == 0, 0.0, hh)
    ext = jnp.concatenate([hh, h], axis=0)
    pos = (i % tiles_per_batch) * tm + lax.broadcasted_iota(jnp.int32, (tm, 1), 0)
    for gi, w in enumerate(POOL_WINDOWS):
        cs = slice(gi * POOL_GROUP, (gi + 1) * POOL_GROUP)
        s = ext[:, cs]
        d = 1
        while d < w:
            s = s + pltpu.roll(s, d, axis=0)
            d *= 2
        cnt = jnp.minimum(pos + 1, w).astype(f32)
        pooled = s[halo:] / cnt - h[:, cs]
        mixed = _dot(pooled.astype(bf16), w_ref[gi]) * ps_ref[:, cs]
        o_ref[:, cs] = x[:, cs] + gate_ref[0][:, cs] * mixed


def _pool_layer(x, gain, mod, pool_w, pool_scale, *, seq, tm=512):
    T, D = x.shape
    tpb = seq // tm
    halo = 2 * SUBLANES
    vec = pl.BlockSpec((1, D), lambda i: (0, 0))
    return pl.pallas_call(
        functools.partial(_pool_kernel, tiles_per_batch=tpb, tm=tm),
        out_shape=jax.ShapeDtypeStruct((T, D), f32),
        grid=(T // tm,),
        in_specs=[
            pl.BlockSpec((tm, D), lambda i: (i, 0)),
            pl.BlockSpec((halo, D), lambda i: (jnp.maximum(i * (tm // halo) - 1, 0), 0)),
            vec,
            pl.BlockSpec((1, 1, D), lambda i: (i // tpb, 0, 0)),
            pl.BlockSpec((1, 1, D), lambda i: (i // tpb, 0, 1)),
            pl.BlockSpec((1, 1, D), lambda i: (i // tpb, 0, 2)),
            pl.BlockSpec(pool_w.shape, lambda i: (0, 0, 0)),
            vec,
        ],
        out_specs=pl.BlockSpec((tm, D), lambda i: (i, 0)),
        compiler_params=_cparams(("parallel",)),
        name="pool_mixer",
    )(x, x, gain, mod, mod, mod, pool_w, pool_scale.reshape(1, D))


def _route_kernel(x_ref, g_ref, sh_ref, sc_ref, w_ref, b_ref, h_ref, route_ref):
    h = _ada_norm(x_ref[...], g_ref[...], sh_ref[0], sc_ref[0])
    h_ref[...] = _pack_halves(h)
    h_hi, h_mid, _ = _split3(h)
    w_hi, w_mid, _ = _split3(w_ref[...])
    logits = _dot(h_hi, w_hi) + (_dot(h_hi, w_mid) + _dot(h_mid, w_hi)) + b_ref[...]
    lane = lax.broadcasted_iota(jnp.int32, logits.shape, 1)
    big = jnp.int32(LANES)

    def masked_argmax(mask):
        mx = jnp.max(jnp.where(mask, logits, NEG_BIG), axis=-1, keepdims=True)
        idx = jnp.min(jnp.where(jnp.logical_and(mask, logits == mx), lane, big), axis=-1, keepdims=True)
        return mx, idx

    gmask = lane < N_GROUPS
    gmax, gidx = masked_argmax(gmask)
    grp_w = 1.0 / jnp.sum(jnp.where(gmask, jnp.exp(logits - gmax), 0.0), axis=-1, keepdims=True)
    e_lo = EXP_LANE0 + gidx * EXPERTS_PER_GROUP
    emask = jnp.logical_and(lane >= e_lo, lane < e_lo + EXPERTS_PER_GROUP)
    m1, i1 = masked_argmax(emask)
    m2, i2 = masked_argmax(jnp.logical_and(emask, lane != i1))
    e21 = jnp.exp(m2 - m1)
    w1 = grp_w / (1.0 + e21)
    w2 = grp_w * e21 / (1.0 + e21)
    out = jnp.where(lane == 0, (i1 - EXP_LANE0).astype(f32), 0.0)
    out = jnp.where(lane == 1, (i2 - EXP_LANE0).astype(f32), out)
    out = jnp.where(lane == 2, w1, out)
    out = jnp.where(lane == 3, w2, out)
    route_ref[...] = out


def _route(x, gain, mod, w_router, b_router, *, seq, tm=512):
    T, D = x.shape
    tpb = seq // tm
    vec = pl.BlockSpec((1, D), lambda i: (0, 0))
    return pl.pallas_call(
        _route_kernel,
        out_shape=(jax.ShapeDtypeStruct((T, D // 2), jnp.uint32), jax.ShapeDtypeStruct((T, LANES), f32)),
        grid=(T // tm,),
        in_specs=[
            pl.BlockSpec((tm, D), lambda i: (i, 0)),
            vec,
            pl.BlockSpec((1, 1, D), lambda i: (i // tpb, 0, 0)),
            pl.BlockSpec((1, 1, D), lambda i: (i // tpb, 0, 1)),
            pl.BlockSpec((D, LANES), lambda i: (0, 0)),
            pl.BlockSpec((1, LANES), lambda i: (0, 0)),
        ],
        out_specs=(pl.BlockSpec((tm, D // 2), lambda i: (i, 0)), pl.BlockSpec((tm, LANES), lambda i: (i, 0))),
        compiler_params=_cparams(("parallel",)),
        name="moe_route",
    )(x, gain, mod, mod, w_router, b_router)


def _expert_kernel(be_ref, nu_ref, tok_ref, h_hbm, wg_ref, wu_ref, wd_ref, o_ref, xbuf, sems, xb_s, wg_s, wu_s, wd_s,
                   *, n_blocks):
    del n_blocks
    i = pl.program_id(0)
    slot = i % 2
    n_used = nu_ref[0]
    used = i < n_used

    def row_copy(blk, slot_, r):
        tok = tok_ref[blk * MOE_ROWS + r]
        return pltpu.make_async_copy(h_hbm.at[pl.ds(tok, 1)], xbuf.at[slot_, pl.ds(r, 1)], sems.at[slot_])

    def gather(blk, slot_):
        for r in range(MOE_ROWS):
            row_copy(blk, slot_, r).start(priority=r % 2)

    @pl.when(i == 0)
    def _():
        gather(0, 0)

    @pl.when(i + 1 < n_used)
    def _():
        gather(i + 1, 1 - slot)

    @pl.when(used)
    def _():
        pltpu.make_async_copy(h_hbm.at[pl.ds(0, MOE_ROWS)], xbuf.at[slot], sems.at[slot]).wait()

        @pl.when(jnp.logical_or(i == 0, be_ref[i] != be_ref[jnp.maximum(i - 1, 0)]))
        def _():
            wg_s[...] = wg_ref[0, 0].astype(bf16)
            wu_s[...] = wu_ref[0, 0].astype(bf16)
            wd_s[...] = wd_ref[0, 0].astype(bf16)

        lo, hi = _unpack_halves(xbuf[slot])
        half = lo.shape[1]
        xb_s[:, :half] = lo.astype(bf16)
        xb_s[:, half:] = hi.astype(bf16)
        xb = xb_s[...]
        gate = _dot(xb, wg_s[...])
        up = _dot(xb, wu_s[...])
        hid = (gate * _sigmoid(gate) * up).astype(bf16)
        o_ref[...] = _pack_halves(_dot(hid, wd_s[...]))

    @pl.when(jnp.logical_not(used))
    def _():
        o_ref[...] = jnp.zeros(o_ref.shape, o_ref.dtype)


def _experts(h, slot_tok, block_e, n_used, w_gate, w_up, w_down, layer):
    DE = w_gate.shape[-1]
    D = w_gate.shape[-2]
    DP = h.shape[1]
    n_blocks = block_e.shape[0]

    def wsel(i, be, nu, tok):
        return (layer, be[jnp.minimum(i, nu[0] - 1)], 0, 0)

    grid_spec = pltpu.PrefetchScalarGridSpec(
        num_scalar_prefetch=3,
        grid=(n_blocks,),
        in_specs=[
            pl.BlockSpec(memory_space=pl.ANY),
            pl.BlockSpec((1, 1, D, DE), wsel),
            pl.BlockSpec((1, 1, D, DE), wsel),
            pl.BlockSpec((1, 1, DE, D), wsel),
        ],
        out_specs=pl.BlockSpec((MOE_ROWS, DP), lambda i, be, nu, tok: (i, 0)),
        scratch_shapes=[pltpu.VMEM((2, MOE_ROWS, DP), jnp.uint32), pltpu.SemaphoreType.DMA((2,)),
                        pltpu.VMEM((MOE_ROWS, D), bf16),
                        pltpu.VMEM((D, DE), bf16), pltpu.VMEM((D, DE), bf16), pltpu.VMEM((DE, D), bf16)],
    )
    return pl.pallas_call(
        functools.partial(_expert_kernel, n_blocks=n_blocks),
        out_shape=jax.ShapeDtypeStruct((n_blocks * MOE_ROWS, DP), jnp.uint32),
        grid_spec=grid_spec,
        compiler_params=_cparams(("arbitrary",)),
        name="moe_experts",
    )(block_e, n_used, slot_tok, h, w_gate, w_up, w_down)


def _combine_kernel(dest_ref, out_hbm, x_ref, route_ref, gate_ref, o_ref, rows, sems, *, tm, n_steps):
    i = pl.program_id(0)

    def issue(step, slot):
        def body(r, carry):
            t = step * tm + r
            for kk in range(TOP_K):
                d = dest_ref[t * TOP_K + kk]
                pltpu.make_async_copy(out_hbm.at[pl.ds(d, 1)], rows.at[slot, kk, pl.ds(r, 1)],
                                      sems.at[slot]).start(priority=kk)
            return carry
        lax.fori_loop(0, tm, body, 0, unroll=4)

    @pl.when(i == 0)
    def _():
        issue(0, 0)

    @pl.when(i + 1 < n_steps)
    def _():
        issue(i + 1, (i + 1) % 2)

    slot = i % 2
    for kk in range(TOP_K):
        pltpu.make_async_copy(out_hbm.at[pl.ds(0, tm)], rows.at[slot, kk], sems.at[slot]).wait()
    route = route_ref[...]
    w0, w1 = route[:, 2:3], route[:, 3:4]
    lo0, hi0 = _unpack_halves(rows[slot, 0])
    lo1, hi1 = _unpack_halves(rows[slot, 1])
    half = lo0.shape[1]
    gate = gate_ref[0]
    o_ref[:, :half] = x_ref[:, :half] + gate[:, :half] * (w0 * lo0 + w1 * lo1)
    o_ref[:, half:] = x_ref[:, half:] + gate[:, half:] * (w0 * hi0 + w1 * hi1)


def _combine(out_buf, dest, x, route, mod, *, seq, tm=512):
    T, D = x.shape
    n_steps = T // tm
    tpb = seq // tm
    grid_spec = pltpu.PrefetchScalarGridSpec(
        num_scalar_prefetch=1,
        grid=(n_steps,),
        in_specs=[
            pl.BlockSpec(memory_space=pl.ANY),
            pl.BlockSpec((tm, D), lambda i, d: (i, 0)),
            pl.BlockSpec((tm, LANES), lambda i, d: (i, 0)),
            pl.BlockSpec((1, 1, D), lambda i, d: (i // tpb, 0, 2)),
        ],
        out_specs=pl.BlockSpec((tm, D), lambda i, d: (i, 0)),
        scratch_shapes=[pltpu.VMEM((2, TOP_K, tm, out_buf.shape[1]), out_buf.dtype), pltpu.SemaphoreType.DMA((2,))],
    )
    return pl.pallas_call(
        functools.partial(_combine_kernel, tm=tm, n_steps=n_steps),
        out_shape=jax.ShapeDtypeStruct((T, D), f32),
        grid_spec=grid_spec,
        compiler_params=_cparams(("arbitrary",)),
        name="moe_combine",
    )(dest, out_buf, x, route, mod)


def _dispatch_plan(ids, n_tokens):
    N = n_tokens * TOP_K
    i32 = jnp.int32
    flat_ids = ids.reshape(-1)
    order = jnp.argsort(flat_ids).astype(i32)
    inv_order = jnp.argsort(order).astype(i32)
    experts = jnp.arange(N_EXPERTS, dtype=i32)
    counts = jnp.sum((flat_ids[:, None] == experts[None, :]).astype(i32), axis=0)
    start = jnp.cumsum(counts) - counts
    padded = ((counts + MOE_ROWS - 1) // MOE_ROWS) * MOE_ROWS
    seg_end = jnp.cumsum(padded).astype(i32)
    pad_start = seg_end - padded
    dest = (pad_start - start)[flat_ids] + inv_order
    n_blocks = -(-N // MOE_ROWS) + N_EXPERTS
    blk_row0 = jnp.arange(n_blocks, dtype=i32) * MOE_ROWS
    block_e = jnp.minimum(jnp.sum((blk_row0[:, None] >= seg_end[None, :]).astype(i32), axis=1), N_EXPERTS - 1)
    n_used = (seg_end[-1:] // MOE_ROWS).astype(i32)
    row = jnp.arange(n_blocks * MOE_ROWS, dtype=i32)
    row_e = jnp.repeat(block_e, MOE_ROWS)
    j = row - pad_start[row_e]
    src = order[jnp.clip(start[row_e] + j, 0, N - 1)] // TOP_K
    slot_tok = jnp.where(j < counts[row_e], src, 0).astype(i32)
    return dest.astype(i32), slot_tok, block_e.astype(i32), n_used


def _moe_layer(x, gain, mod, w_router, b_router, w_gate, w_up, w_down, layer, *, seq):
    T, D = x.shape
    h, route = _route(x, gain, mod, w_router, b_router, seq=seq)
    ids = route[:, :TOP_K].astype(jnp.int32)
    dest, slot_tok, block_e, n_used = _dispatch_plan(ids, T)
    out_buf = _experts(h, slot_tok, block_e, n_used, w_gate, w_up, w_down, layer)
    return _combine(out_buf, dest, x, route, mod, seq=seq)


def _router_params(w_grp, b_grp, w_exp, b_exp):
    D = w_grp.shape[0]
    w = jnp.zeros((D, LANES), f32).at[:, :N_GROUPS].set(w_grp).at[:, EXP_LANE0:EXP_LANE0 + N_EXPERTS].set(w_exp)
    b = jnp.zeros((1, LANES), f32).at[0, :N_GROUPS].set(b_grp).at[0, EXP_LANE0:EXP_LANE0 + N_EXPERTS].set(b_exp)
    return w, b


def _pad_cols(w, n):
    return jnp.zeros((w.shape[0], n), w.dtype).at[:, :w.shape[1]].set(w)


def _pad_rows(w, n):
    return jnp.zeros((n, w.shape[1]), w.dtype).at[:w.shape[0]].set(w)


def kernel(x, c, norm_g, ada_w, ada_b, rel_bias, attn_w_in, attn_w_o, attn_q_gain, attn_k_gain, attn_sinks, rw_mu, rw_w_rkv, rw_w0, rw_w1, rw_w2, rw_a0, rw_a1, rw_a2, rw_g1, rw_g2, rw_k_k, rw_k_a, rw_r_k, rw_lnx_g, rw_lnx_b, rw_w_o, pool_w, pool_scale, moe_w_grp, moe_b_grp, moe_w_exp, moe_b_exp, moe_w_gate, moe_w_up, moe_w_down):
    B, S, D = x.shape
    T = B * S
    xt = x.reshape(T, D)
    mods = _ada_mods(c, ada_w, ada_b)
    bias = _band_bias_masked(rel_bias)
    for layer in range(DEPTH):
        kind, idx = layer % N_MIXERS, layer // N_MIXERS
        gain = norm_g[layer, 0].reshape(1, D)
        mod = mods[2 * layer]
        if kind == 0:
            qkv = _norm_mm(xt, gain, mod, attn_w_in[idx].astype(bf16), seq=S)
            o = _attention(qkv, bias, attn_q_gain[idx], attn_k_gain[idx], attn_sinks[idx], batch=B, seq=S)
            xt = _mm_res(o, attn_w_o[idx].astype(bf16), xt, mod, seq=S)
        elif kind == 1:
            w_all = jnp.concatenate(
                [rw_w_rkv[idx, 0], rw_w_rkv[idx, 1], rw_w_rkv[idx, 2], _pad_cols(rw_w1[idx], LANES),
                 _pad_cols(rw_a1[idx], LANES), _pad_cols(rw_g1[idx], RW_PROJ_TILE - 2 * LANES)], axis=1).astype(bf16)
            proj = _rwkv_proj(xt, gain, mod, rw_mu[idx], w_all, seq=S)
            yg = _wkv(proj, rw_w0[idx], rw_a0[idx],
                      _pad_rows(rw_w2[idx], LANES).astype(bf16), _pad_rows(rw_a2[idx], LANES).astype(bf16),
                      rw_g2[idx].astype(bf16), rw_k_k[idx], rw_k_a[idx], rw_r_k[idx], rw_lnx_g[idx], rw_lnx_b[idx],
                      batch=B, seq=S)
            xt = _mm_res(yg, rw_w_o[idx].astype(bf16), xt, mod, seq=S)
        else:
            xt = _pool_layer(xt, gain, mod, pool_w[idx].astype(bf16), pool_scale[idx], seq=S)
        w_router, b_router = _router_params(moe_w_grp[layer], moe_b_grp[layer], moe_w_exp[layer], moe_b_exp[layer])
        xt = _moe_layer(xt, norm_g[layer, 1].reshape(1, D), mods[2 * layer + 1], w_router, b_router,
                        moe_w_gate, moe_w_up, moe_w_down, layer, seq=S)
    return xt.reshape(B, S, D)
```

```python
import functools
import math

import jax
import jax.numpy as jnp
import numpy as np
from jax import lax
from jax.experimental import pallas as pl
from jax.experimental.pallas import tpu as pltpu

f32 = jnp.float32
bf16 = jnp.bfloat16

D_MODEL = 2048
DEPTH = 4
N_MIXERS = 3
HEAD_DIM = 64
N_HEADS = D_MODEL // HEAD_DIM
N_KV_HEADS = 4
GQA_GROUP = N_HEADS // N_KV_HEADS
WINDOW = 128
Q_DIM = N_HEADS * HEAD_DIM
KV_DIM = N_KV_HEADS * HEAD_DIM
QKV_DIM = Q_DIM + 2 * KV_DIM
N_BUCKETS = 32
MAX_DISTANCE = 128
RWKV_HEAD = 64
GN_EPS = 64e-5
POOL_WINDOWS = (2, 4, 8, 16)
POOL_GROUP = D_MODEL // len(POOL_WINDOWS)
N_GROUPS = 4
EXPERTS_PER_GROUP = 8
N_EXPERTS = N_GROUPS * EXPERTS_PER_GROUP
TOP_K = 2
D_EXPERT = D_MODEL // 4
NORM_EPS = 1e-6

LANES = 128
SUBLANES = 8
VMEM_LIMIT = 56 * 1024 * 1024

MOE_ROWS = 512
WKV_CHUNK = 64
EXP_LANE0 = 32
NEG_BIG = -1e30


def _cparams(sem):
    return pltpu.CompilerParams(dimension_semantics=sem, vmem_limit_bytes=VMEM_LIMIT)


def _ada_norm(xf, gain, shift, scale):
    ms = jnp.mean(xf * xf, axis=-1, keepdims=True)
    return xf * lax.rsqrt(ms + NORM_EPS) * gain * (1.0 + scale) + shift


def _sigmoid(z):
    return 1.0 / (1.0 + jnp.exp(-z))


def _split3(x):
    hi = x.astype(bf16)
    r1 = x - hi.astype(f32)
    mid = r1.astype(bf16)
    lo = (r1 - mid.astype(f32)).astype(bf16)
    return hi, mid, lo


def _pack_halves(x):
    n = x.shape[1] // 2
    lo = pltpu.bitcast(x[:, :n].astype(bf16).astype(f32), jnp.uint32)
    hi = pltpu.bitcast(x[:, n:].astype(bf16).astype(f32), jnp.uint32)
    return (lo >> 16) | hi


def _unpack_halves(u):
    return pltpu.bitcast(u << 16, f32), pltpu.bitcast(u & jnp.uint32(0xFFFF0000), f32)


def _dot(a, b):
    return jnp.dot(a, b, preferred_element_type=f32)


def _dot_nt(a, b):
    return lax.dot_general(a, b, (((1,), (1,)), ((), ())), preferred_element_type=f32)


def _dot_tn(a, b):
    return lax.dot_general(a, b, (((0,), (0,)), ((), ())), preferred_element_type=f32)


def _ada_kernel(c_ref, w_ref, b_ref, o_ref):
    c = c_ref[...]
    ca = (c * _sigmoid(c)).astype(bf16)
    o_ref[0] = _dot(ca, w_ref[0].astype(bf16)) + b_ref[0]


def _ada_mods(c, ada_w, ada_b):
    B, D = c.shape
    n_mod = ada_w.shape[0] * ada_w.shape[1]
    N = ada_w.shape[-1]
    tn = 1024
    c_pad = jnp.zeros((SUBLANES, D), f32).at[:B].set(c)
    out = pl.pallas_call(
        _ada_kernel,
        out_shape=jax.ShapeDtypeStruct((n_mod, SUBLANES, N), f32),
        grid=(n_mod, N // tn),
        in_specs=[
            pl.BlockSpec((SUBLANES, D), lambda m, j: (0, 0)),
            pl.BlockSpec((1, D, tn), lambda m, j: (m, 0, j)),
            pl.BlockSpec((1, 1, tn), lambda m, j: (m, 0, j)),
        ],
        out_specs=pl.BlockSpec((1, SUBLANES, tn), lambda m, j: (m, 0, j)),
        compiler_params=_cparams(("parallel", "parallel")),
        name="ada_mods",
    )(c_pad, ada_w.reshape(n_mod, D, N), ada_b.reshape(n_mod, 1, N))
    return out[:, :B].reshape(n_mod, B, 1, N)


def _lookahead_tile(n_tiles):
    def tile(i, j):
        return jnp.where(jnp.logical_and(i == 0, j == 0), 0, jnp.minimum(i + 1, n_tiles - 1))
    return tile


def _norm_mm_kernel(x_ref, g_ref, sh_ref, sc_ref, w_ref, o_ref, lhs_ref):
    i = pl.program_id(0)
    j = pl.program_id(1)
    last = pl.num_programs(1) - 1
    cur = i % 2

    def normed():
        return _ada_norm(x_ref[...], g_ref[...], sh_ref[0], sc_ref[0]).astype(bf16)

    @pl.when(jnp.logical_and(i == 0, j == 0))
    def _():
        lhs_ref[0] = normed()

    @pl.when(j < last)
    def _():
        o_ref[...] = _dot(lhs_ref[cur], w_ref[...]).astype(o_ref.dtype)

    @pl.when(j == last)
    def _():
        o_ref[...] = _dot(lhs_ref[cur], w_ref[...]).astype(o_ref.dtype)
        lhs_ref[1 - cur] = normed()


def _norm_mm(x, gain, mod, w, *, seq, tm=512, tn=1280):
    T, D = x.shape
    N = w.shape[1]
    tpb = seq // tm
    nxt = _lookahead_tile(T // tm)
    assert N // tn >= 2
    return pl.pallas_call(
        _norm_mm_kernel,
        out_shape=jax.ShapeDtypeStruct((T, N), bf16),
        grid=(T // tm, N // tn),
        in_specs=[
            pl.BlockSpec((tm, D), lambda i, j: (nxt(i, j), 0)),
            pl.BlockSpec((1, D), lambda i, j: (0, 0)),
            pl.BlockSpec((1, 1, D), lambda i, j: (nxt(i, j) // tpb, 0, 0)),
            pl.BlockSpec((1, 1, D), lambda i, j: (nxt(i, j) // tpb, 0, 1)),
            pl.BlockSpec((D, tn), lambda i, j: (0, j)),
        ],
        out_specs=pl.BlockSpec((tm, tn), lambda i, j: (i, j)),
        scratch_shapes=[pltpu.VMEM((2, tm, D), bf16)],
        compiler_params=_cparams(("arbitrary", "arbitrary")),
        name="norm_mm",
    )(x, gain, mod, mod, w)


RW_MIXES = 6
RW_LORA_TILE = 512
RW_PROJ_TILE = 1024
RW_PREP_ROWS = 16


def _rwkv_proj_kernel(x_ref, xh_ref, g_ref, sh_ref, sc_ref, mu_ref, w_ref, o_ref, lhs_ref, *, tiles_per_batch, n_big,
                      chunks_per_mix):
    i = pl.program_id(0)
    j = pl.program_id(1)

    @pl.when(j == 0)
    def _():
        g, sh, sc = g_ref[...], sh_ref[0], sc_ref[0]
        h_last = _ada_norm(xh_ref[...], g, sh, sc)[SUBLANES - 1:SUBLANES]
        h_last = jnp.where(i % tiles_per_batch == 0, 0.0, h_last)
        rows = RW_PREP_ROWS
        row = lax.broadcasted_iota(jnp.int32, (rows, x_ref.shape[1]), 0)

        def prep(c, prev):
            r0 = pl.multiple_of(c * rows, rows)
            h = _ada_norm(x_ref[pl.ds(r0, rows), :], g, sh, sc)
            dh = jnp.where(row == 0, prev, pltpu.roll(h, 1, axis=0)) - h
            for m in range(RW_MIXES):
                lhs_ref[m, pl.ds(r0, rows), :] = (h + dh * mu_ref[m:m + 1, :]).astype(bf16)
            return h[rows - 1:rows]

        lax.fori_loop(0, x_ref.shape[0] // rows, prep, h_last)

    @pl.when(j < n_big)
    def _():
        o_ref[...] = _dot(lhs_ref[j // chunks_per_mix], w_ref[...]).astype(o_ref.dtype)

    @pl.when(j == n_big)
    def _():
        q = LANES
        o_ref[:, :RW_LORA_TILE] = jnp.concatenate(
            [_dot(lhs_ref[3], w_ref[:, :q]), _dot(lhs_ref[4], w_ref[:, q:2 * q]),
             _dot(lhs_ref[5], w_ref[:, 2 * q:RW_LORA_TILE])], axis=1).astype(o_ref.dtype)
        o_ref[:, RW_LORA_TILE:] = jnp.zeros((o_ref.shape[0], o_ref.shape[1] - RW_LORA_TILE), o_ref.dtype)


def _rwkv_proj(x, gain, mod, mu, w_all, *, seq, tm=512, tn=RW_PROJ_TILE):
    T, D = x.shape
    n_big = 3 * D // tn
    tpb = seq // tm
    vec_spec = pl.BlockSpec((1, D), lambda i, j: (0, 0))
    return pl.pallas_call(
        functools.partial(_rwkv_proj_kernel, tiles_per_batch=tpb, n_big=n_big, chunks_per_mix=D // tn),
        out_shape=jax.ShapeDtypeStruct((T, w_all.shape[1]), bf16),
        grid=(T // tm, n_big + 1),
        in_specs=[
            pl.BlockSpec((tm, D), lambda i, j: (i, 0)),
            pl.BlockSpec((SUBLANES, D), lambda i, j: (jnp.maximum(i * (tm // SUBLANES) - 1, 0), 0)),
            vec_spec,
            pl.BlockSpec((1, 1, D), lambda i, j: (i // tpb, 0, 0)),
            pl.BlockSpec((1, 1, D), lambda i, j: (i // tpb, 0, 1)),
            pl.BlockSpec((RW_MIXES, D), lambda i, j: (0, 0)),
            pl.BlockSpec((D, tn), lambda i, j: (0, j)),
        ],
        out_specs=pl.BlockSpec((tm, tn), lambda i, j: (i, j)),
        scratch_shapes=[pltpu.VMEM((RW_MIXES, tm, D), bf16)],
        compiler_params=_cparams(("parallel", "arbitrary")),
        name="rwkv_proj",
    )(x, x, gain, mod, mod, mu, w_all)


def _mm_res_kernel(a_ref, w_ref, x_ref, gate_ref, o_ref):
    o_ref[...] = x_ref[...] + gate_ref[0] * _dot(a_ref[...], w_ref[...])


def _mm_res(a, w, x, mod, *, seq, tm=1024, tn=1024):
    T, K = a.shape
    N = w.shape[1]
    tpb = seq // tm
    gate_blk = 2 * (N // tn)
    return pl.pallas_call(
        _mm_res_kernel,
        out_shape=jax.ShapeDtypeStruct((T, N), f32),
        grid=(T // tm, N // tn),
        in_specs=[
            pl.BlockSpec((tm, K), lambda i, j: (i, 0)),
            pl.BlockSpec((K, tn), lambda i, j: (0, j)),
            pl.BlockSpec((tm, tn), lambda i, j: (i, j)),
            pl.BlockSpec((1, 1, tn), lambda i, j: (i // tpb, 0, gate_blk + j)),
        ],
        out_specs=pl.BlockSpec((tm, tn), lambda i, j: (i, j)),
        compiler_params=_cparams(("parallel", "parallel")),
        name="mm_res",
    )(a, w, x, mod)


def _attn_kernel(sink_ref, q_ref, kp_ref, kc_ref, vp_ref, vc_ref, bias_ref, qg_ref, kg_ref, o_ref):
    n = pl.program_id(1)
    W = WINDOW
    GW = GQA_GROUP * W
    heads_per_tile = LANES // HEAD_DIM
    n_qt = Q_DIM // LANES
    qt_per_kv = n_qt // N_KV_HEADS
    lane = lax.broadcasted_iota(jnp.int32, (1, LANES), 1)
    m0 = lane < HEAD_DIM
    ri = lax.broadcasted_iota(jnp.int32, (LANES, LANES), 0)
    ci = lax.broadcasted_iota(jnp.int32, (LANES, LANES), 1)
    head_ones = jnp.where((ri // HEAD_DIM) == (ci // HEAD_DIM), 1.0, 0.0).astype(bf16)

    def inv_rms(t):
        tt = t * t
        hi = tt.astype(bf16)
        lo = (tt - hi.astype(f32)).astype(bf16)
        ssq = _dot(hi, head_ones) + _dot(lo, head_ones)
        return lax.rsqrt(ssq * (1.0 / HEAD_DIM) + NORM_EPS)

    q_stack = jnp.concatenate([q_ref[:, t * LANES:(t + 1) * LANES] for t in range(n_qt)], axis=0).astype(f32)
    q_stack = q_stack * inv_rms(q_stack) * qg_ref[...]
    k_band = jnp.concatenate([kp_ref[...], kc_ref[...]], axis=0).astype(f32)
    v_band = jnp.concatenate([vp_ref[...], vc_ref[...]], axis=0).astype(f32)
    col = lax.broadcasted_iota(jnp.int32, (GW, 2 * W), 1)
    key_ok = jnp.logical_or(n > 0, col >= W)
    grp = lax.broadcasted_iota(jnp.int32, (GW, 1), 0) // W
    ones_kv = jnp.ones((2 * W, LANES), bf16)
    for kt in range(KV_DIM // LANES):
        k_tile = k_band[:, kt * LANES:(kt + 1) * LANES]
        k_tile = k_tile * inv_rms(k_tile) * kg_ref[...]
        k_roll = pltpu.roll(k_tile, HEAD_DIM, axis=1)
        v_tile = v_band[:, kt * LANES:(kt + 1) * LANES]
        v_roll = pltpu.roll(v_tile, HEAD_DIM, axis=1)
        for side in range(heads_per_tile):
            h = kt * heads_per_tile + side
            first = m0 if side == 0 else jnp.logical_not(m0)
            k_dup = jnp.where(first, k_tile, k_roll).astype(bf16)
            v_dup = jnp.where(first, v_tile, v_roll).astype(bf16)
            lhs = []
            for j in range(qt_per_kv):
                q_t = q_stack[(h * qt_per_kv + j) * W:(h * qt_per_kv + j + 1) * W]
                lhs += [jnp.where(m0, q_t, 0.0), jnp.where(m0, 0.0, q_t)]
            lhs = jnp.concatenate(lhs, axis=0).astype(bf16)
            logits = _dot_nt(lhs, k_dup)
            logits = logits + bias_ref[h].reshape(GW, 2 * W)
            logits = jnp.where(key_ok, logits, NEG_BIG)
            sink = jnp.zeros((GW, 1), f32)
            for g in range(GQA_GROUP):
                sink = jnp.where(grp == g, sink_ref[h * GQA_GROUP + g], sink)
            m = jnp.maximum(jnp.max(logits, axis=-1, keepdims=True), sink)
            p = jnp.exp(logits - m).astype(bf16)
            pv = _dot(p, jnp.concatenate([v_dup, ones_kv], axis=1))
            o_full = pv[:, :LANES] / (pv[:, LANES:] + jnp.exp(sink - m))
            for j in range(qt_per_kv):
                t = h * qt_per_kv + j
                o_ref[:, t * LANES:(t + 1) * LANES] = jnp.where(
                    m0, o_full[2 * j * W:(2 * j + 1) * W], o_full[(2 * j + 1) * W:(2 * j + 2) * W]).astype(o_ref.dtype)


def _band_bias_masked(rel_bias):
    max_exact = N_BUCKETS // 2
    i = jnp.arange(WINDOW)[:, None]
    j = jnp.arange(2 * WINDOW)[None, :]
    dist = WINDOW + i - j
    nn = jnp.maximum(dist, 0)
    nf = jnp.maximum(nn, 1).astype(f32)
    large = max_exact + (jnp.log(nf / max_exact) / math.log(MAX_DISTANCE / max_exact)
                         * (N_BUCKETS - max_exact)).astype(jnp.int32)
    large = jnp.minimum(large, N_BUCKETS - 1)
    bucket = jnp.where(nn < max_exact, nn, large)
    b = jnp.transpose(rel_bias[bucket], (2, 0, 1)).astype(f32)
    in_band = (j > i) & (j <= i + WINDOW)
    b = jnp.where(in_band[None], b, NEG_BIG)
    return b.reshape(N_KV_HEADS, GQA_GROUP, WINDOW, 2 * WINDOW)


def _attention(qkv, bias, q_gain, k_gain, sinks, *, batch, seq):
    T = qkv.shape[0]
    nb = seq // WINDOW
    kcol = Q_DIM // KV_DIM
    vcol = kcol + 1

    def cur(col):
        return lambda b, n, s: (b * nb + n, col)

    def prev(col):
        return lambda b, n, s: (b * nb + jnp.maximum(n - 1, 0), col)

    def tile_gain(g):
        return jnp.tile(g.astype(f32), LANES // HEAD_DIM).reshape(1, LANES)

    grid_spec = pltpu.PrefetchScalarGridSpec(
        num_scalar_prefetch=1,
        grid=(batch, nb),
        in_specs=[
            pl.BlockSpec((WINDOW, Q_DIM), lambda b, n, s: (b * nb + n, 0)),
            pl.BlockSpec((WINDOW, KV_DIM), prev(kcol)),
            pl.BlockSpec((WINDOW, KV_DIM), cur(kcol)),
            pl.BlockSpec((WINDOW, KV_DIM), prev(vcol)),
            pl.BlockSpec((WINDOW, KV_DIM), cur(vcol)),
            pl.BlockSpec((N_KV_HEADS, GQA_GROUP, WINDOW, 2 * WINDOW), lambda b, n, s: (0, 0, 0, 0)),
            pl.BlockSpec((1, LANES), lambda b, n, s: (0, 0)),
            pl.BlockSpec((1, LANES), lambda b, n, s: (0, 0)),
        ],
        out_specs=pl.BlockSpec((WINDOW, Q_DIM), lambda b, n, s: (b * nb + n, 0)),
    )
    return pl.pallas_call(
        _attn_kernel,
        out_shape=jax.ShapeDtypeStruct((T, Q_DIM), bf16),
        grid_spec=grid_spec,
        compiler_params=_cparams(("parallel", "parallel")),
        name="swa_attention",
    )(sinks, qkv, qkv, qkv, qkv, qkv, bias, tile_gain(q_gain * (HEAD_DIM ** -0.5)), tile_gain(k_gain))


def _wkv_kernel(r_ref, k_ref, v_ref, lora_ref, w0_ref, a0_ref, w2_ref, a2_ref, g2_ref,
                kk_ref, ka_ref, rk_ref, lng_ref, lnb_ref, o_ref,
                state_ref, w_s, c_s, a_s, g_s, ar_s, bk_s, vb_s, aab_s, aak_s, abk_s, x_s, p_s, xb_s,
                aav_s, uh_s, uv_s, rs_s, *, n_pairs):
    L = WKV_CHUNK
    P2 = 2 * L
    pairs = range(n_pairs)

    @pl.when(pl.program_id(1) == 0)
    def _():
        state_ref[...] = jnp.zeros(state_ref.shape, f32)

    lw = lora_ref[:, :LANES].astype(f32)
    la = lora_ref[:, LANES:2 * LANES]
    lg = lora_ref[:, 2 * LANES:].astype(f32)
    w_lin = w0_ref[...] + _dot(jnp.tanh(lw).astype(bf16), w2_ref[...])
    neg = -w_lin
    softplus = jnp.maximum(neg, 0.0) + jnp.log(1.0 + jnp.exp(-jnp.abs(neg)))
    logw = -jnp.exp(-softplus - 0.5)
    w_s[...] = logw
    a_s[...] = _sigmoid(a0_ref[...] + _dot(la, a2_ref[...]))
    g_s[...] = _dot(_sigmoid(lg).astype(bf16), g2_ref[...])
    ti = lax.broadcasted_iota(jnp.int32, (L, L), 0)
    tj = lax.broadcasted_iota(jnp.int32, (L, L), 1)
    tri = jnp.where(ti >= tj, 1.0, 0.0).astype(bf16)
    hi, mid, lo = _split3(logw)
    c_s[...] = _dot(tri, hi) + _dot(tri, mid) + _dot(tri, lo)

    lane = lax.broadcasted_iota(jnp.int32, (1, LANES), 1)
    m0 = lane < RWKV_HEAD
    ri = lax.broadcasted_iota(jnp.int32, (P2, P2), 0)
    ci = lax.broadcasted_iota(jnp.int32, (P2, P2), 1)
    same = (ri // L) == (ci // L)
    strict = jnp.logical_and(same, ri > ci)
    incl = jnp.logical_and(same, ri >= ci)
    eye = jnp.where(ri == ci, 1.0, 0.0)

    def seg_sum(x):
        s0 = jnp.sum(jnp.where(m0, x, 0.0), axis=-1, keepdims=True)
        s1 = jnp.sum(jnp.where(m0, 0.0, x), axis=-1, keepdims=True)
        return jnp.where(m0, s0, s1)

    def bd(x):
        return jnp.concatenate([jnp.where(m0, x, 0.0), jnp.where(m0, 0.0, x)], axis=0)

    def cols(p):
        return slice(p * LANES, (p + 1) * LANES)

    def k_mod(p):
        return k_ref[:, cols(p)].astype(f32) * (1.0 + (a_s[:, cols(p)] - 1.0) * ka_ref[p])

    for p in pairs:
        cs = cols(p)
        r = r_ref[:, cs].astype(f32)
        k = k_ref[:, cs].astype(f32)
        a = a_s[:, cs]
        cm = c_s[:, cs]
        kk = k * kk_ref[p]
        kk = kk / jnp.maximum(jnp.sqrt(seg_sum(kk * kk)), 1e-12)
        e_pos = jnp.exp(cm)
        e_neg = jnp.exp(-cm)
        e_exc = jnp.exp(cm - w_s[:, cs])
        ar_s[p, :P2] = bd(-kk * e_exc).astype(bf16)
        ar_s[p, P2:] = bd(r * e_pos).astype(bf16)
        bk_s[p, :P2] = bd(kk * a * e_neg).astype(bf16)
        bk_s[p, P2:] = bd(k_mod(p) * e_neg).astype(bf16)
        vb_s[p] = bd(v_ref[:, cs].astype(f32)).astype(bf16)

    for p in pairs:
        G = _dot_nt(ar_s[p], bk_s[p])
        A_ab = jnp.where(strict, G[:P2, :P2], 0.0)
        aab_s[p] = A_ab.astype(bf16)
        x_s[p] = eye + A_ab
        aak_s[p] = jnp.where(strict, G[:P2, P2:], 0.0).astype(bf16)
        abk_s[p, :, :LANES] = jnp.where(incl, G[P2:, :P2], 0.0).astype(bf16)
        abk_s[p, :, LANES:] = jnp.where(incl, G[P2:, P2:], 0.0).astype(bf16)

    for p in pairs:
        p_s[p] = _dot(aab_s[p], aab_s[p]).astype(bf16)
        aav_s[p, :, :LANES] = ar_s[p, :P2]
        aav_s[p, :, LANES:] = _dot(aak_s[p], vb_s[p]).astype(bf16)
    for _ in range(int(math.log2(L)) - 2):
        for p in pairs:
            pw = p_s[p]
            Z = _dot(jnp.concatenate([x_s[p].astype(bf16), pw], axis=0), pw)
            x_s[p] = x_s[p] + Z[:P2]
            p_s[p] = Z[P2:].astype(bf16)
    for p in pairs:
        X = x_s[p]
        xb_s[p] = (X + _dot(X.astype(bf16), p_s[p])).astype(bf16)
    for p in pairs:
        AU = _dot(xb_s[p], aav_s[p])
        ar_s[p, :P2] = AU[:, :LANES].astype(bf16)
        uh_s[p] = AU[:, LANES:]

    for p in pairs:
        T1 = _dot_nt(ar_s[p], state_ref[p].astype(bf16))
        uv_s[p, :P2] = (T1[:P2] + uh_s[p]).astype(bf16)
        uv_s[p, P2:] = vb_s[p]
        rs_s[p] = T1[P2:]

    for p in pairs:
        cs = cols(p)
        Y = rs_s[p] + _dot(abk_s[p], uv_s[p])
        g_last = jnp.exp(c_s[L - 1:L, cs])
        state_ref[p] = (state_ref[p] + _dot_tn(uv_s[p], bk_s[p])) * g_last
        y = Y[:L] + Y[L:]
        mean = seg_sum(y) * (1.0 / RWKV_HEAD)
        yc = y - mean
        var = seg_sum(yc * yc) * (1.0 / RWKV_HEAD)
        yn = yc * lax.rsqrt(var + GN_EPS) * lng_ref[p] + lnb_ref[p]
        r = r_ref[:, cs].astype(f32)
        v = v_ref[:, cs].astype(f32)
        yn = yn + seg_sum(r * k_mod(p) * rk_ref[p]) * v
        o_ref[:, cs] = (yn * g_s[:, cs]).astype(o_ref.dtype)


def _wkv(proj, w0, a0, w2, a2, g2, k_k, k_a, r_k, lnx_g, lnx_b, *, batch, seq):
    T = proj.shape[0]
    D = w0.shape[0]
    L = WKV_CHUNK
    P2 = 2 * L
    n_pairs = D // LANES
    nc = seq // L

    def row(width, col):
        return pl.BlockSpec((L, width), lambda b, t: (b * nc + t, col))

    def full(shape):
        return pl.BlockSpec(shape, lambda b, t: (0,) * len(shape))

    pair_vec = lambda a: a.reshape(n_pairs, 1, LANES).astype(f32)
    pv_spec = full((n_pairs, 1, LANES))
    wide = pltpu.VMEM((L, D), f32)
    sq = lambda dt: pltpu.VMEM((n_pairs, P2, P2), dt)
    tall = pltpu.VMEM((n_pairs, 2 * P2, LANES), bf16)
    wide2 = pltpu.VMEM((n_pairs, P2, 2 * LANES), bf16)
    return pl.pallas_call(
        functools.partial(_wkv_kernel, n_pairs=n_pairs),
        out_shape=jax.ShapeDtypeStruct((T, D), bf16),
        grid=(batch, nc),
        in_specs=[row(D, 0), row(D, 1), row(D, 2), row(RW_LORA_TILE, 3 * D // RW_LORA_TILE),
                  full((1, D)), full((1, D)), full(w2.shape), full(a2.shape), full(g2.shape),
                  pv_spec, pv_spec, pv_spec, pv_spec, pv_spec],
        out_specs=row(D, 0),
        scratch_shapes=[sq(f32), wide, wide, wide, wide, tall, tall, sq(bf16), sq(bf16), sq(bf16), wide2,
                        sq(f32), sq(bf16), sq(bf16), wide2, sq(f32), tall, sq(f32)],
        compiler_params=_cparams(("parallel", "arbitrary")),
        name="wkv7",
    )(proj, proj, proj, proj, w0.reshape(1, D), a0.reshape(1, D), w2, a2, g2,
      pair_vec(k_k), pair_vec(k_a), pair_vec(r_k), pair_vec(lnx_g), pair_vec(lnx_b))


def _pool_kernel(x_ref, xh_ref, g_ref, sh_ref, sc_ref, gate_ref, w_ref, ps_ref, o_ref, *, tiles_per_batch, tm):
    i = pl.program_id(0)
    halo = 2 * SUBLANES
    g, sh, sc = g_ref[...], sh_ref[0], sc_ref[0]
    x = x_ref[...]
    h = _ada_norm(x, g, sh, sc)
    hh = _ada_norm(xh_ref[...], g, sh, sc)
    hh = jnp.where(i % tiles_per_batch == 0, 0.0, hh)
    ext = jnp.concatenate([hh, h], axis=0)
    pos = (i % tiles_per_batch) * tm + lax.broadcasted_iota(jnp.int32, (tm, 1), 0)
    for gi, w in enumerate(POOL_WINDOWS):
        cs = slice(gi * POOL_GROUP, (gi + 1) * POOL_GROUP)
        s = ext[:, cs]
        d = 1
        while d < w:
            s = s + pltpu.roll(s, d, axis=0)
            d *= 2
        cnt = jnp.minimum(pos + 1, w).astype(f32)
        pooled = s[halo:] / cnt - h[:, cs]
        mixed = _dot(pooled.astype(bf16), w_ref[gi]) * ps_ref[:, cs]
        o_ref[:, cs] = x[:, cs] + gate_ref[0][:, cs] * mixed


def _pool_layer(x, gain, mod, pool_w, pool_scale, *, seq, tm=512):
    T, D = x.shape
    tpb = seq // tm
    halo = 2 * SUBLANES
    vec = pl.BlockSpec((1, D), lambda i: (0, 0))
    return pl.pallas_call(
        functools.partial(_pool_kernel, tiles_per_batch=tpb, tm=tm),
        out_shape=jax.ShapeDtypeStruct((T, D), f32),
        grid=(T // tm,),
        in_specs=[
            pl.BlockSpec((tm, D), lambda i: (i, 0)),
            pl.BlockSpec((halo, D), lambda i: (jnp.maximum(i * (tm // halo) - 1, 0), 0)),
            vec,
            pl.BlockSpec((1, 1, D), lambda i: (i // tpb, 0, 0)),
            pl.BlockSpec((1, 1, D), lambda i: (i // tpb, 0, 1)),
            pl.BlockSpec((1, 1, D), lambda i: (i // tpb, 0, 2)),
            pl.BlockSpec(pool_w.shape, lambda i: (0, 0, 0)),
            vec,
        ],
        out_specs=pl.BlockSpec((tm, D), lambda i: (i, 0)),
        compiler_params=_cparams(("parallel",)),
        name="pool_mixer",
    )(x, x, gain, mod, mod, mod, pool_w, pool_scale.reshape(1, D))


def _route_kernel(x_ref, g_ref, sh_ref, sc_ref, w_ref, b_ref, h_ref, route_ref):
    h = _ada_norm(x_ref[...], g_ref[...], sh_ref[0], sc_ref[0])
    h_ref[...] = _pack_halves(h)
    h_hi, h_mid, _ = _split3(h)
    w_hi, w_mid, _ = _split3(w_ref[...])
    logits = _dot(h_hi, w_hi) + (_dot(h_hi, w_mid) + _dot(h_mid, w_hi)) + b_ref[...]
    lane = lax.broadcasted_iota(jnp.int32, logits.shape, 1)
    big = jnp.int32(LANES)

    def masked_argmax(mask):
        mx = jnp.max(jnp.where(mask, logits, NEG_BIG), axis=-1, keepdims=True)
        idx = jnp.min(jnp.where(jnp.logical_and(mask, logits == mx), lane, big), axis=-1, keepdims=True)
        return mx, idx

    gmask = lane < N_GROUPS
    gmax, gidx = masked_argmax(gmask)
    grp_w = 1.0 / jnp.sum(jnp.where(gmask, jnp.exp(logits - gmax), 0.0), axis=-1, keepdims=True)
    e_lo = EXP_LANE0 + gidx * EXPERTS_PER_GROUP
    emask = jnp.logical_and(lane >= e_lo, lane < e_lo + EXPERTS_PER_GROUP)
    m1, i1 = masked_argmax(emask)
    m2, i2 = masked_argmax(jnp.logical_and(emask, lane != i1))
    e21 = jnp.exp(m2 - m1)
    w1 = grp_w / (1.0 + e21)
    w2 = grp_w * e21 / (1.0 + e21)
    out = jnp.where(lane == 0, (i1 - EXP_LANE0).astype(f32), 0.0)
    out = jnp.where(lane == 1, (i2 - EXP_LANE0).astype(f32), out)
    out = jnp.where(lane == 2, w1, out)
    out = jnp.where(lane == 3, w2, out)
    route_ref[...] = out


def _route(x, gain, mod, w_router, b_router, *, seq, tm=512):
    T, D = x.shape
    tpb = seq // tm
    vec = pl.BlockSpec((1, D), lambda i: (0, 0))
    return pl.pallas_call(
        _route_kernel,
        out_shape=(jax.ShapeDtypeStruct((T, D // 2), jnp.uint32), jax.ShapeDtypeStruct((T, LANES), f32)),
        grid=(T // tm,),
        in_specs=[
            pl.BlockSpec((tm, D), lambda i: (i, 0)),
            vec,
            pl.BlockSpec((1, 1, D), lambda i: (i // tpb, 0, 0)),
            pl.BlockSpec((1, 1, D), lambda i: (i // tpb, 0, 1)),
            pl.BlockSpec((D, LANES), lambda i: (0, 0)),
            pl.BlockSpec((1, LANES), lambda i: (0, 0)),
        ],
        out_specs=(pl.BlockSpec((tm, D // 2), lambda i: (i, 0)), pl.BlockSpec((tm, LANES), lambda i: (i, 0))),
        compiler_params=_cparams(("parallel",)),
        name="moe_route",
    )(x, gain, mod, mod, w_router, b_router)


def _expert_kernel(be_ref, nu_ref, tok_ref, h_hbm, wg_ref, wu_ref, wd_ref, o_ref, xbuf, sems, xb_s, wg_s, wu_s, wd_s,
                   *, n_blocks):
    del n_blocks
    i = pl.program_id(0)
    slot = i % 2
    n_used = nu_ref[0]
    used = i < n_used

    def row_copy(blk, slot_, r):
        tok = tok_ref[blk * MOE_ROWS + r]
        return pltpu.make_async_copy(h_hbm.at[pl.ds(tok, 1)], xbuf.at[slot_, pl.ds(r, 1)], sems.at[slot_])

    def gather(blk, slot_):
        for r in range(MOE_ROWS):
            row_copy(blk, slot_, r).start(priority=r % 2)

    @pl.when(i == 0)
    def _():
        gather(0, 0)

    @pl.when(i + 1 < n_used)
    def _():
        gather(i + 1, 1 - slot)

    @pl.when(used)
    def _():
        pltpu.make_async_copy(h_hbm.at[pl.ds(0, MOE_ROWS)], xbuf.at[slot], sems.at[slot]).wait()

        @pl.when(jnp.logical_or(i == 0, be_ref[i] != be_ref[jnp.maximum(i - 1, 0)]))
        def _():
            wg_s[...] = wg_ref[0, 0].astype(bf16)
            wu_s[...] = wu_ref[0, 0].astype(bf16)
            wd_s[...] = wd_ref[0, 0].astype(bf16)

        lo, hi = _unpack_halves(xbuf[slot])
        half = lo.shape[1]
        xb_s[:, :half] = lo.astype(bf16)
        xb_s[:, half:] = hi.astype(bf16)
        xb = xb_s[...]
        gate = _dot(xb, wg_s[...])
        up = _dot(xb, wu_s[...])
        hid = (gate * _sigmoid(gate) * up).astype(bf16)
        o_ref[...] = _pack_halves(_dot(hid, wd_s[...]))

    @pl.when(jnp.logical_not(used))
    def _():
        o_ref[...] = jnp.zeros(o_ref.shape, o_ref.dtype)


def _experts(h, slot_tok, block_e, n_used, w_gate, w_up, w_down, layer):
    DE = w_gate.shape[-1]
    D = w_gate.shape[-2]
    DP = h.shape[1]
    n_blocks = block_e.shape[0]

    def wsel(i, be, nu, tok):
        return (layer, be[jnp.minimum(i, nu[0] - 1)], 0, 0)

    grid_spec = pltpu.PrefetchScalarGridSpec(
        num_scalar_prefetch=3,
        grid=(n_blocks,),
        in_specs=[
            pl.BlockSpec(memory_space=pl.ANY),
            pl.BlockSpec((1, 1, D, DE), wsel),
            pl.BlockSpec((1, 1, D, DE), wsel),
            pl.BlockSpec((1, 1, DE, D), wsel),
        ],
        out_specs=pl.BlockSpec((MOE_ROWS, DP), lambda i, be, nu, tok: (i, 0)),
        scratch_shapes=[pltpu.VMEM((2, MOE_ROWS, DP), jnp.uint32), pltpu.SemaphoreType.DMA((2,)),
                        pltpu.VMEM((MOE_ROWS, D), bf16),
                        pltpu.VMEM((D, DE), bf16), pltpu.VMEM((D, DE), bf16), pltpu.VMEM((DE, D), bf16)],
    )
    return pl.pallas_call(
        functools.partial(_expert_kernel, n_blocks=n_blocks),
        out_shape=jax.ShapeDtypeStruct((n_blocks * MOE_ROWS, DP), jnp.uint32),
        grid_spec=grid_spec,
        compiler_params=_cparams(("arbitrary",)),
        name="moe_experts",
    )(block_e, n_used, slot_tok, h, w_gate, w_up, w_down)


def _combine_kernel(dest_ref, out_hbm, x_ref, route_ref, gate_ref, o_ref, rows, sems, *, tm, n_steps):
    i = pl.program_id(0)

    def issue(step, slot):
        def body(r, carry):
            t = step * tm + r
            for kk in range(TOP_K):
                d = dest_ref[t * TOP_K + kk]
                pltpu.make_async_copy(out_hbm.at[pl.ds(d, 1)], rows.at[slot, kk, pl.ds(r, 1)],
                                      sems.at[slot]).start(priority=kk)
            return carry
        lax.fori_loop(0, tm, body, 0, unroll=4)

    @pl.when(i == 0)
    def _():
        issue(0, 0)

    @pl.when(i + 1 < n_steps)
    def _():
        issue(i + 1, (i + 1) % 2)

    slot = i % 2
    for kk in range(TOP_K):
        pltpu.make_async_copy(out_hbm.at[pl.ds(0, tm)], rows.at[slot, kk], sems.at[slot]).wait()
    route = route_ref[...]
    w0, w1 = route[:, 2:3], route[:, 3:4]
    lo0, hi0 = _unpack_halves(rows[slot, 0])
    lo1, hi1 = _unpack_halves(rows[slot, 1])
    half = lo0.shape[1]
    gate = gate_ref[0]
    o_ref[:, :half] = x_ref[:, :half] + gate[:, :half] * (w0 * lo0 + w1 * lo1)
    o_ref[:, half:] = x_ref[:, half:] + gate[:, half:] * (w0 * hi0 + w1 * hi1)


def _combine(out_buf, dest, x, route, mod, *, seq, tm=512):
    T, D = x.shape
    n_steps = T // tm
    tpb = seq // tm
    grid_spec = pltpu.PrefetchScalarGridSpec(
        num_scalar_prefetch=1,
        grid=(n_steps,),
        in_specs=[
            pl.BlockSpec(memory_space=pl.ANY),
            pl.BlockSpec((tm, D), lambda i, d: (i, 0)),
            pl.BlockSpec((tm, LANES), lambda i, d: (i, 0)),
            pl.BlockSpec((1, 1, D), lambda i, d: (i // tpb, 0, 2)),
        ],
        out_specs=pl.BlockSpec((tm, D), lambda i, d: (i, 0)),
        scratch_shapes=[pltpu.VMEM((2, TOP_K, tm, out_buf.shape[1]), out_buf.dtype), pltpu.SemaphoreType.DMA((2,))],
    )
    return pl.pallas_call(
        functools.partial(_combine_kernel, tm=tm, n_steps=n_steps),
        out_shape=jax.ShapeDtypeStruct((T, D), f32),
        grid_spec=grid_spec,
        compiler_params=_cparams(("arbitrary",)),
        name="moe_combine",
    )(dest, out_buf, x, route, mod)


def _dispatch_plan(ids, n_tokens):
    N = n_tokens * TOP_K
    i32 = jnp.int32
    flat_ids = ids.reshape(-1)
    order = jnp.argsort(flat_ids).astype(i32)
    inv_order = jnp.argsort(order).astype(i32)
    experts = jnp.arange(N_EXPERTS, dtype=i32)
    counts = jnp.sum((flat_ids[:, None] == experts[None, :]).astype(i32), axis=0)
    start = jnp.cumsum(counts) - counts
    padded = ((counts + MOE_ROWS - 1) // MOE_ROWS) * MOE_ROWS
    seg_end = jnp.cumsum(padded).astype(i32)
    pad_start = seg_end - padded
    dest = (pad_start - start)[flat_ids] + inv_order
    n_blocks = -(-N // MOE_ROWS) + N_EXPERTS
    blk_row0 = jnp.arange(n_blocks, dtype=i32) * MOE_ROWS
    block_e = jnp.minimum(jnp.sum((blk_row0[:, None] >= seg_end[None, :]).astype(i32), axis=1), N_EXPERTS - 1)
    n_used = (seg_end[-1:] // MOE_ROWS).astype(i32)
    row = jnp.arange(n_blocks * MOE_ROWS, dtype=i32)
    row_e = jnp.repeat(block_e, MOE_ROWS)
    j = row - pad_start[row_e]
    src = order[jnp.clip(start[row_e] + j, 0, N - 1)] // TOP_K
    slot_tok = jnp.where(j < counts[row_e], src, 0).astype(i32)
    return dest.astype(i32), slot_tok, block_e.astype(i32), n_used


def _moe_layer(x, gain, mod, w_router, b_router, w_gate, w_up, w_down, layer, *, seq):
    T, D = x.shape
    h, route = _route(x, gain, mod, w_router, b_router, seq=seq)
    ids = route[:, :TOP_K].astype(jnp.int32)
    dest, slot_tok, block_e, n_used = _dispatch_plan(ids, T)
    out_buf = _experts(h, slot_tok, block_e, n_used, w_gate, w_up, w_down, layer)
    return _combine(out_buf, dest, x, route, mod, seq=seq)


def _router_params(w_grp, b_grp, w_exp, b_exp):
    D = w_grp.shape[0]
    w = jnp.zeros((D, LANES), f32).at[:, :N_GROUPS].set(w_grp).at[:, EXP_LANE0:EXP_LANE0 + N_EXPERTS].set(w_exp)
    b = jnp.zeros((1, LANES), f32).at[0, :N_GROUPS].set(b_grp).at[0, EXP_LANE0:EXP_LANE0 + N_EXPERTS].set(b_exp)
    return w, b


def _pad_cols(w, n):
    return jnp.zeros((w.shape[0], n), w.dtype).at[:, :w.shape[1]].set(w)


def _pad_rows(w, n):
    return jnp.zeros((n, w.shape[1]), w.dtype).at[:w.shape[0]].set(w)


def kernel(x, c, norm_g, ada_w, ada_b, rel_bias, attn_w_in, attn_w_o, attn_q_gain, attn_k_gain, attn_sinks, rw_mu, rw_w_rkv, rw_w0, rw_w1, rw_w2, rw_a0, rw_a1, rw_a2, rw_g1, rw_g2, rw_k_k, rw_k_a, rw_r_k, rw_lnx_g, rw_lnx_b, rw_w_o, pool_w, pool_scale, moe_w_grp, moe_b_grp, moe_w_exp, moe_b_exp, moe_w_gate, moe_w_up, moe_w_down):
    B, S, D = x.shape
    T = B * S
    xt = x.reshape(T, D)
    mods = _ada_mods(c, ada_w, ada_b)
    bias = _band_bias_masked(rel_bias)
    for layer in range(DEPTH):
        kind, idx = layer % N_MIXERS, layer // N_MIXERS
        gain = norm_g[layer, 0].reshape(1, D)
        mod = mods[2 * layer]
        if kind == 0:
            qkv = _norm_mm(xt, gain, mod, attn_w_in[idx].astype(bf16), seq=S)
            o = _attention(qkv, bias, attn_q_gain[idx], attn_k_gain[idx], attn_sinks[idx], batch=B, seq=S)
            xt = _mm_res(o, attn_w_o[idx].astype(bf16), xt, mod, seq=S)
        elif kind == 1:
            w_all = jnp.concatenate(
                [rw_w_rkv[idx, 0], rw_w_rkv[idx, 1], rw_w_rkv[idx, 2], _pad_cols(rw_w1[idx], LANES),
                 _pad_cols(rw_a1[idx], LANES), _pad_cols(rw_g1[idx], RW_PROJ_TILE - 2 * LANES)], axis=1).astype(bf16)
            proj = _rwkv_proj(xt, gain, mod, rw_mu[idx], w_all, seq=S)
            yg = _wkv(proj, rw_w0[idx], rw_a0[idx],
                      _pad_rows(rw_w2[idx], LANES).astype(bf16), _pad_rows(rw_a2[idx], LANES).astype(bf16),
                      rw_g2[idx].astype(bf16), rw_k_k[idx], rw_k_a[idx], rw_r_k[idx], rw_lnx_g[idx], rw_lnx_b[idx],
                      batch=B, seq=S)
            xt = _mm_res(yg, rw_w_o[idx].astype(bf16), xt, mod, seq=S)
        else:
            xt = _pool_layer(xt, gain, mod, pool_w[idx].astype(bf16), pool_scale[idx], seq=S)
        w_router, b_router = _router_params(moe_w_grp[layer], moe_b_grp[layer], moe_w_exp[layer], moe_b_exp[layer])
        xt = _moe_layer(xt, norm_g[layer, 1].reshape(1, D), mods[2 * layer + 1], w_router, b_router,
                        moe_w_gate, moe_w_up, moe_w_down, layer, seq=S)
    return xt.reshape(B, S, D)
```

```python
import functools
import math

import jax
import jax.numpy as jnp
import numpy as np
from jax import lax
from jax.experimental import pallas as pl
from jax.experimental.pallas import tpu as pltpu

f32 = jnp.float32
bf16 = jnp.bfloat16

D_MODEL = 2048
DEPTH = 4
N_MIXERS = 3
HEAD_DIM = 64
N_HEADS = D_MODEL // HEAD_DIM
N_KV_HEADS = 4
GQA_GROUP = N_HEADS // N_KV_HEADS
WINDOW = 128
Q_DIM = N_HEADS * HEAD_DIM
KV_DIM = N_KV_HEADS * HEAD_DIM
QKV_DIM = Q_DIM + 2 * KV_DIM
N_BUCKETS = 32
MAX_DISTANCE = 128
RWKV_HEAD = 64
GN_EPS = 64e-5
POOL_WINDOWS = (2, 4, 8, 16)
POOL_GROUP = D_MODEL // len(POOL_WINDOWS)
N_GROUPS = 4
EXPERTS_PER_GROUP = 8
N_EXPERTS = N_GROUPS * EXPERTS_PER_GROUP
TOP_K = 2
D_EXPERT = D_MODEL // 4
NORM_EPS = 1e-6

LANES = 128
SUBLANES = 8
VMEM_LIMIT = 56 * 1024 * 1024

MOE_ROWS = 512
WKV_CHUNK = 64
EXP_LANE0 = 32
NEG_BIG = -1e30


def _cparams(sem):
    return pltpu.CompilerParams(dimension_semantics=sem, vmem_limit_bytes=VMEM_LIMIT)


def _ada_norm(xf, gain, shift, scale):
    ms = jnp.mean(xf * xf, axis=-1, keepdims=True)
    return xf * lax.rsqrt(ms + NORM_EPS) * gain * (1.0 + scale) + shift


def _sigmoid(z):
    return 1.0 / (1.0 + jnp.exp(-z))


def _split3(x):
    hi = x.astype(bf16)
    r1 = x - hi.astype(f32)
    mid = r1.astype(bf16)
    lo = (r1 - mid.astype(f32)).astype(bf16)
    return hi, mid, lo


def _pack_halves(x):
    n = x.shape[1] // 2
    lo = pltpu.bitcast(x[:, :n].astype(bf16).astype(f32), jnp.uint32)
    hi = pltpu.bitcast(x[:, n:].astype(bf16).astype(f32), jnp.uint32)
    return (lo >> 16) | hi


def _unpack_halves(u):
    return pltpu.bitcast(u << 16, f32), pltpu.bitcast(u & jnp.uint32(0xFFFF0000), f32)


def _dot(a, b):
    return jnp.dot(a, b, preferred_element_type=f32)


def _dot_nt(a, b):
    return lax.dot_general(a, b, (((1,), (1,)), ((), ())), preferred_element_type=f32)


def _dot_tn(a, b):
    return lax.dot_general(a, b, (((0,), (0,)), ((), ())), preferred_element_type=f32)


def _ada_kernel(c_ref, w_ref, b_ref, o_ref):
    c = c_ref[...]
    ca = (c * _sigmoid(c)).astype(bf16)
    o_ref[0] = _dot(ca, w_ref[0].astype(bf16)) + b_ref[0]


def _ada_mods(c, ada_w, ada_b):
    B, D = c.shape
    n_mod = ada_w.shape[0] * ada_w.shape[1]
    N = ada_w.shape[-1]
    tn = 1024
    c_pad = jnp.zeros((SUBLANES, D), f32).at[:B].set(c)
    out = pl.pallas_call(
        _ada_kernel,
        out_shape=jax.ShapeDtypeStruct((n_mod, SUBLANES, N), f32),
        grid=(n_mod, N // tn),
        in_specs=[
            pl.BlockSpec((SUBLANES, D), lambda m, j: (0, 0)),
            pl.BlockSpec((1, D, tn), lambda m, j: (m, 0, j)),
            pl.BlockSpec((1, 1, tn), lambda m, j: (m, 0, j)),
        ],
        out_specs=pl.BlockSpec((1, SUBLANES, tn), lambda m, j: (m, 0, j)),
        compiler_params=_cparams(("parallel", "parallel")),
        name="ada_mods",
    )(c_pad, ada_w.reshape(n_mod, D, N), ada_b.reshape(n_mod, 1, N))
    return out[:, :B].reshape(n_mod, B, 1, N)


def _lookahead_tile(n_tiles):
    def tile(i, j):
        return jnp.where(jnp.logical_and(i == 0, j == 0), 0, jnp.minimum(i + 1, n_tiles - 1))
    return tile


def _norm_mm_kernel(x_ref, g_ref, sh_ref, sc_ref, w_ref, o_ref, lhs_ref):
    i = pl.program_id(0)
    j = pl.program_id(1)
    last = pl.num_programs(1) - 1
    cur = i % 2

    def normed():
        return _ada_norm(x_ref[...], g_ref[...], sh_ref[0], sc_ref[0]).astype(bf16)

    @pl.when(jnp.logical_and(i == 0, j == 0))
    def _():
        lhs_ref[0] = normed()

    @pl.when(j < last)
    def _():
        o_ref[...] = _dot(lhs_ref[cur], w_ref[...]).astype(o_ref.dtype)

    @pl.when(j == last)
    def _():
        o_ref[...] = _dot(lhs_ref[cur], w_ref[...]).astype(o_ref.dtype)
        lhs_ref[1 - cur] = normed()


def _norm_mm(x, gain, mod, w, *, seq, tm=512, tn=1280):
    T, D = x.shape
    N = w.shape[1]
    tpb = seq // tm
    nxt = _lookahead_tile(T // tm)
    assert N // tn >= 2
    return pl.pallas_call(
        _norm_mm_kernel,
        out_shape=jax.ShapeDtypeStruct((T, N), bf16),
        grid=(T // tm, N // tn),
        in_specs=[
            pl.BlockSpec((tm, D), lambda i, j: (nxt(i, j), 0)),
            pl.BlockSpec((1, D), lambda i, j: (0, 0)),
            pl.BlockSpec((1, 1, D), lambda i, j: (nxt(i, j) // tpb, 0, 0)),
            pl.BlockSpec((1, 1, D), lambda i, j: (nxt(i, j) // tpb, 0, 1)),
            pl.BlockSpec((D, tn), lambda i, j: (0, j)),
        ],
        out_specs=pl.BlockSpec((tm, tn), lambda i, j: (i, j)),
        scratch_shapes=[pltpu.VMEM((2, tm, D), bf16)],
        compiler_params=_cparams(("arbitrary", "arbitrary")),
        name="norm_mm",
    )(x, gain, mod, mod, w)


RW_MIXES = 6
RW_LORA_TILE = 512
RW_PROJ_TILE = 1024
RW_PREP_ROWS = 16


def _rwkv_proj_kernel(x_ref, xh_ref, g_ref, sh_ref, sc_ref, mu_ref, w_ref, o_ref, lhs_ref, *, tiles_per_batch, n_big,
                      chunks_per_mix):
    i = pl.program_id(0)
    j = pl.program_id(1)

    @pl.when(j == 0)
    def _():
        g, sh, sc = g_ref[...], sh_ref[0], sc_ref[0]
        h_last = _ada_norm(xh_ref[...], g, sh, sc)[SUBLANES - 1:SUBLANES]
        h_last = jnp.where(i % tiles_per_batch == 0, 0.0, h_last)
        rows = RW_PREP_ROWS
        row = lax.broadcasted_iota(jnp.int32, (rows, x_ref.shape[1]), 0)

        def prep(c, prev):
            r0 = pl.multiple_of(c * rows, rows)
            h = _ada_norm(x_ref[pl.ds(r0, rows), :], g, sh, sc)
            dh = jnp.where(row == 0, prev, pltpu.roll(h, 1, axis=0)) - h
            for m in range(RW_MIXES):
                lhs_ref[m, pl.ds(r0, rows), :] = (h + dh * mu_ref[m:m + 1, :]).astype(bf16)
            return h[rows - 1:rows]

        lax.fori_loop(0, x_ref.shape[0] // rows, prep, h_last)

    @pl.when(j < n_big)
    def _():
        o_ref[...] = _dot(lhs_ref[j // chunks_per_mix], w_ref[...]).astype(o_ref.dtype)

    @pl.when(j == n_big)
    def _():
        q = LANES
        o_ref[:, :RW_LORA_TILE] = jnp.concatenate(
            [_dot(lhs_ref[3], w_ref[:, :q]), _dot(lhs_ref[4], w_ref[:, q:2 * q]),
             _dot(lhs_ref[5], w_ref[:, 2 * q:RW_LORA_TILE])], axis=1).astype(o_ref.dtype)
        o_ref[:, RW_LORA_TILE:] = jnp.zeros((o_ref.shape[0], o_ref.shape[1] - RW_LORA_TILE), o_ref.dtype)


def _rwkv_proj(x, gain, mod, mu, w_all, *, seq, tm=512, tn=RW_PROJ_TILE):
    T, D = x.shape
    n_big = 3 * D // tn
    tpb = seq // tm
    vec_spec = pl.BlockSpec((1, D), lambda i, j: (0, 0))
    return pl.pallas_call(
        functools.partial(_rwkv_proj_kernel, tiles_per_batch=tpb, n_big=n_big, chunks_per_mix=D // tn),
        out_shape=jax.ShapeDtypeStruct((T, w_all.shape[1]), bf16),
        grid=(T // tm, n_big + 1),
        in_specs=[
            pl.BlockSpec((tm, D), lambda i, j: (i, 0)),
            pl.BlockSpec((SUBLANES, D), lambda i, j: (jnp.maximum(i * (tm // SUBLANES) - 1, 0), 0)),
            vec_spec,
            pl.BlockSpec((1, 1, D), lambda i, j: (i // tpb, 0, 0)),
            pl.BlockSpec((1, 1, D), lambda i, j: (i // tpb, 0, 1)),
            pl.BlockSpec((RW_MIXES, D), lambda i, j: (0, 0)),
            pl.BlockSpec((D, tn), lambda i, j: (0, j)),
        ],
        out_specs=pl.BlockSpec((tm, tn), lambda i, j: (i, j)),
        scratch_shapes=[pltpu.VMEM((RW_MIXES, tm, D), bf16)],
        compiler_params=_cparams(("parallel", "arbitrary")),
        name="rwkv_proj",
    )(x, x, gain, mod, mod, mu, w_all)


def _mm_res_kernel(a_ref, w_ref, x_ref, gate_ref, o_ref):
    o_ref[...] = x_ref[...] + gate_ref[0] * _dot(a_ref[...], w_ref[...])


def _mm_res(a, w, x, mod, *, seq, tm=1024, tn=1024):
    T, K = a.shape
    N = w.shape[1]
    tpb = seq // tm
    gate_blk = 2 * (N // tn)
    return pl.pallas_call(
        _mm_res_kernel,
        out_shape=jax.ShapeDtypeStruct((T, N), f32),
        grid=(T // tm, N // tn),
        in_specs=[
            pl.BlockSpec((tm, K), lambda i, j: (i, 0)),
            pl.BlockSpec((K, tn), lambda i, j: (0, j)),
            pl.BlockSpec((tm, tn), lambda i, j: (i, j)),
            pl.BlockSpec((1, 1, tn), lambda i, j: (i // tpb, 0, gate_blk + j)),
        ],
        out_specs=pl.BlockSpec((tm, tn), lambda i, j: (i, j)),
        compiler_params=_cparams(("parallel", "parallel")),
        name="mm_res",
    )(a, w, x, mod)


def _attn_kernel(sink_ref, q_ref, kp_ref, kc_ref, vp_ref, vc_ref, bias_ref, qg_ref, kg_ref, o_ref):
    n = pl.program_id(1)
    W = WINDOW
    GW = GQA_GROUP * W
    heads_per_tile = LANES // HEAD_DIM
    n_qt = Q_DIM // LANES
    qt_per_kv = n_qt // N_KV_HEADS
    lane = lax.broadcasted_iota(jnp.int32, (1, LANES), 1)
    m0 = lane < HEAD_DIM
    ri = lax.broadcasted_iota(jnp.int32, (LANES, LANES), 0)
    ci = lax.broadcasted_iota(jnp.int32, (LANES, LANES), 1)
    head_ones = jnp.where((ri // HEAD_DIM) == (ci // HEAD_DIM), 1.0, 0.0).astype(bf16)

    def inv_rms(t):
        tt = t * t
        hi = tt.astype(bf16)
        lo = (tt - hi.astype(f32)).astype(bf16)
        ssq = _dot(hi, head_ones) + _dot(lo, head_ones)
        return lax.rsqrt(ssq * (1.0 / HEAD_DIM) + NORM_EPS)

    q_stack = jnp.concatenate([q_ref[:, t * LANES:(t + 1) * LANES] for t in range(n_qt)], axis=0).astype(f32)
    q_stack = q_stack * inv_rms(q_stack) * qg_ref[...]
    k_band = jnp.concatenate([kp_ref[...], kc_ref[...]], axis=0).astype(f32)
    v_band = jnp.concatenate([vp_ref[...], vc_ref[...]], axis=0).astype(f32)
    col = lax.broadcasted_iota(jnp.int32, (GW, 2 * W), 1)
    key_ok = jnp.logical_or(n > 0, col >= W)
    grp = lax.broadcasted_iota(jnp.int32, (GW, 1), 0) // W
    ones_kv = jnp.ones((2 * W, LANES), bf16)
    for kt in range(KV_DIM // LANES):
        k_tile = k_band[:, kt * LANES:(kt + 1) * LANES]
        k_tile = k_tile * inv_rms(k_tile) * kg_ref[...]
        k_roll = pltpu.roll(k_tile, HEAD_DIM, axis=1)
        v_tile = v_band[:, kt * LANES:(kt + 1) * LANES]
        v_roll = pltpu.roll(v_tile, HEAD_DIM, axis=1)
        for side in range(heads_per_tile):
            h = kt * heads_per_tile + side
            first = m0 if side == 0 else jnp.logical_not(m0)
            k_dup = jnp.where(first, k_tile, k_roll).astype(bf16)
            v_dup = jnp.where(first, v_tile, v_roll).astype(bf16)
            lhs = []
            for j in range(qt_per_kv):
                q_t = q_stack[(h * qt_per_kv + j) * W:(h * qt_per_kv + j + 1) * W]
                lhs += [jnp.where(m0, q_t, 0.0), jnp.where(m0, 0.0, q_t)]
            lhs = jnp.concatenate(lhs, axis=0).astype(bf16)
            logits = _dot_nt(lhs, k_dup)
            logits = logits + bias_ref[h].reshape(GW, 2 * W)
            logits = jnp.where(key_ok, logits, NEG_BIG)
            sink = jnp.zeros((GW, 1), f32)
            for g in range(GQA_GROUP):
                sink = jnp.where(grp == g, sink_ref[h * GQA_GROUP + g], sink)
            m = jnp.maximum(jnp.max(logits, axis=-1, keepdims=True), sink)
            p = jnp.exp(logits - m).astype(bf16)
            pv = _dot(p, jnp.concatenate([v_dup, ones_kv], axis=1))
            o_full = pv[:, :LANES] / (pv[:, LANES:] + jnp.exp(sink - m))
            for j in range(qt_per_kv):
                t = h * qt_per_kv + j
                o_ref[:, t * LANES:(t + 1) * LANES] = jnp.where(
                    m0, o_full[2 * j * W:(2 * j + 1) * W], o_full[(2 * j + 1) * W:(2 * j + 2) * W]).astype(o_ref.dtype)


def _band_bias_masked(rel_bias):
    max_exact = N_BUCKETS // 2
    i = np.arange(WINDOW)[:, None]
    j = np.arange(2 * WINDOW)[None, :]
    nn = np.maximum(WINDOW + i - j, 0)
    nf = np.maximum(nn, 1).astype(np.float32)
    large = max_exact + (np.log(nf / max_exact) / math.log(MAX_DISTANCE / max_exact)
                         * (N_BUCKETS - max_exact)).astype(np.int32)
    bucket = np.where(nn < max_exact, nn, np.minimum(large, N_BUCKETS - 1)).reshape(-1)
    onehot = (np.arange(N_BUCKETS)[:, None] == bucket[None, :]).astype(np.float32)
    in_band = ((j > i) & (j <= i + WINDOW)).reshape(1, -1)
    b = jnp.dot(rel_bias.astype(f32).T, onehot, precision=lax.Precision.HIGHEST)
    b = jnp.where(in_band, b, NEG_BIG)
    return b.reshape(N_KV_HEADS, GQA_GROUP, WINDOW, 2 * WINDOW)


def _attention(qkv, bias, q_gain, k_gain, sinks, *, batch, seq):
    T = qkv.shape[0]
    nb = seq // WINDOW
    kcol = Q_DIM // KV_DIM
    vcol = kcol + 1

    def cur(col):
        return lambda b, n, s: (b * nb + n, col)

    def prev(col):
        return lambda b, n, s: (b * nb + jnp.maximum(n - 1, 0), col)

    def tile_gain(g):
        return jnp.tile(g.astype(f32), LANES // HEAD_DIM).reshape(1, LANES)

    grid_spec = pltpu.PrefetchScalarGridSpec(
        num_scalar_prefetch=1,
        grid=(batch, nb),
        in_specs=[
            pl.BlockSpec((WINDOW, Q_DIM), lambda b, n, s: (b * nb + n, 0)),
            pl.BlockSpec((WINDOW, KV_DIM), prev(kcol)),
            pl.BlockSpec((WINDOW, KV_DIM), cur(kcol)),
            pl.BlockSpec((WINDOW, KV_DIM), prev(vcol)),
            pl.BlockSpec((WINDOW, KV_DIM), cur(vcol)),
            pl.BlockSpec((N_KV_HEADS, GQA_GROUP, WINDOW, 2 * WINDOW), lambda b, n, s: (0, 0, 0, 0)),
            pl.BlockSpec((1, LANES), lambda b, n, s: (0, 0)),
            pl.BlockSpec((1, LANES), lambda b, n, s: (0, 0)),
        ],
        out_specs=pl.BlockSpec((WINDOW, Q_DIM), lambda b, n, s: (b * nb + n, 0)),
    )
    return pl.pallas_call(
        _attn_kernel,
        out_shape=jax.ShapeDtypeStruct((T, Q_DIM), bf16),
        grid_spec=grid_spec,
        compiler_params=_cparams(("parallel", "parallel")),
        name="swa_attention",
    )(sinks, qkv, qkv, qkv, qkv, qkv, bias, tile_gain(q_gain * (HEAD_DIM ** -0.5)), tile_gain(k_gain))


def _wkv_kernel(r_ref, k_ref, v_ref, lora_ref, w0_ref, a0_ref, w2_ref, a2_ref, g2_ref,
                kk_ref, ka_ref, rk_ref, lng_ref, lnb_ref, o_ref,
                state_ref, w_s, c_s, a_s, g_s, ar_s, bk_s, vb_s, aab_s, aak_s, abk_s, x_s, p_s, xb_s,
                aav_s, uh_s, uv_s, rs_s, *, n_pairs):
    L = WKV_CHUNK
    P2 = 2 * L
    pairs = range(n_pairs)

    @pl.when(pl.program_id(1) == 0)
    def _():
        state_ref[...] = jnp.zeros(state_ref.shape, f32)

    lw = lora_ref[:, :LANES].astype(f32)
    la = lora_ref[:, LANES:2 * LANES]
    lg = lora_ref[:, 2 * LANES:].astype(f32)
    w_lin = w0_ref[...] + _dot(jnp.tanh(lw).astype(bf16), w2_ref[...])
    neg = -w_lin
    softplus = jnp.maximum(neg, 0.0) + jnp.log(1.0 + jnp.exp(-jnp.abs(neg)))
    logw = -jnp.exp(-softplus - 0.5)
    w_s[...] = logw
    a_s[...] = _sigmoid(a0_ref[...] + _dot(la, a2_ref[...]))
    g_s[...] = _dot(_sigmoid(lg).astype(bf16), g2_ref[...])
    ti = lax.broadcasted_iota(jnp.int32, (L, L), 0)
    tj = lax.broadcasted_iota(jnp.int32, (L, L), 1)
    tri = jnp.where(ti >= tj, 1.0, 0.0).astype(bf16)
    hi, mid, lo = _split3(logw)
    c_s[...] = _dot(tri, hi) + _dot(tri, mid) + _dot(tri, lo)

    lane = lax.broadcasted_iota(jnp.int32, (1, LANES), 1)
    m0 = lane < RWKV_HEAD
    ri = lax.broadcasted_iota(jnp.int32, (P2, P2), 0)
    ci = lax.broadcasted_iota(jnp.int32, (P2, P2), 1)
    same = (ri // L) == (ci // L)
    strict = jnp.logical_and(same, ri > ci)
    incl = jnp.logical_and(same, ri >= ci)
    eye = jnp.where(ri == ci, 1.0, 0.0)

    def seg_sum(x):
        s0 = jnp.sum(jnp.where(m0, x, 0.0), axis=-1, keepdims=True)
        s1 = jnp.sum(jnp.where(m0, 0.0, x), axis=-1, keepdims=True)
        return jnp.where(m0, s0, s1)

    def bd(x):
        return jnp.concatenate([jnp.where(m0, x, 0.0), jnp.where(m0, 0.0, x)], axis=0)

    def cols(p):
        return slice(p * LANES, (p + 1) * LANES)

    def k_mod(p):
        return k_ref[:, cols(p)].astype(f32) * (1.0 + (a_s[:, cols(p)] - 1.0) * ka_ref[p])

    for p in pairs:
        cs = cols(p)
        r = r_ref[:, cs].astype(f32)
        k = k_ref[:, cs].astype(f32)
        a = a_s[:, cs]
        cm = c_s[:, cs]
        kk = k * kk_ref[p]
        kk = kk / jnp.maximum(jnp.sqrt(seg_sum(kk * kk)), 1e-12)
        e_pos = jnp.exp(cm)
        e_neg = jnp.exp(-cm)
        e_exc = jnp.exp(cm - w_s[:, cs])
        ar_s[p, :P2] = bd(-kk * e_exc).astype(bf16)
        ar_s[p, P2:] = bd(r * e_pos).astype(bf16)
        bk_s[p, :P2] = bd(kk * a * e_neg).astype(bf16)
        bk_s[p, P2:] = bd(k_mod(p) * e_neg).astype(bf16)
        vb_s[p] = bd(v_ref[:, cs].astype(f32)).astype(bf16)

    for p in pairs:
        G = _dot_nt(ar_s[p], bk_s[p])
        A_ab = jnp.where(strict, G[:P2, :P2], 0.0)
        aab_s[p] = A_ab.astype(bf16)
        x_s[p] = eye + A_ab
        aak_s[p] = jnp.where(strict, G[:P2, P2:], 0.0).astype(bf16)
        abk_s[p, :, :LANES] = jnp.where(incl, G[P2:, :P2], 0.0).astype(bf16)
        abk_s[p, :, LANES:] = jnp.where(incl, G[P2:, P2:], 0.0).astype(bf16)

    for p in pairs:
        p_s[p] = _dot(aab_s[p], aab_s[p]).astype(bf16)
        aav_s[p, :, :LANES] = ar_s[p, :P2]
        aav_s[p, :, LANES:] = _dot(aak_s[p], vb_s[p]).astype(bf16)
    for _ in range(int(math.log2(L)) - 2):
        for p in pairs:
            pw = p_s[p]
            Z = _dot(jnp.concatenate([x_s[p].astype(bf16), pw], axis=0), pw)
            x_s[p] = x_s[p] + Z[:P2]
            p_s[p] = Z[P2:].astype(bf16)
    for p in pairs:
        X = x_s[p]
        xb_s[p] = (X + _dot(X.astype(bf16), p_s[p])).astype(bf16)
    for p in pairs:
        AU = _dot(xb_s[p], aav_s[p])
        ar_s[p, :P2] = AU[:, :LANES].astype(bf16)
        uh_s[p] = AU[:, LANES:]

    for p in pairs:
        T1 = _dot_nt(ar_s[p], state_ref[p].astype(bf16))
        uv_s[p, :P2] = (T1[:P2] + uh_s[p]).astype(bf16)
        uv_s[p, P2:] = vb_s[p]
        rs_s[p] = T1[P2:]

    for p in pairs:
        cs = cols(p)
        Y = rs_s[p] + _dot(abk_s[p], uv_s[p])
        g_last = jnp.exp(c_s[L - 1:L, cs])
        state_ref[p] = (state_ref[p] + _dot_tn(uv_s[p], bk_s[p])) * g_last
        y = Y[:L] + Y[L:]
        mean = seg_sum(y) * (1.0 / RWKV_HEAD)
        yc = y - mean
        var = seg_sum(yc * yc) * (1.0 / RWKV_HEAD)
        yn = yc * lax.rsqrt(var + GN_EPS) * lng_ref[p] + lnb_ref[p]
        r = r_ref[:, cs].astype(f32)
        v = v_ref[:, cs].astype(f32)
        yn = yn + seg_sum(r * k_mod(p) * rk_ref[p]) * v
        o_ref[:, cs] = (yn * g_s[:, cs]).astype(o_ref.dtype)


def _wkv(proj, w0, a0, w2, a2, g2, k_k, k_a, r_k, lnx_g, lnx_b, *, batch, seq):
    T = proj.shape[0]
    D = w0.shape[0]
    L = WKV_CHUNK
    P2 = 2 * L
    n_pairs = D // LANES
    nc = seq // L

    def row(width, col):
        return pl.BlockSpec((L, width), lambda b, t: (b * nc + t, col))

    def full(shape):
        return pl.BlockSpec(shape, lambda b, t: (0,) * len(shape))

    pair_vec = lambda a: a.reshape(n_pairs, 1, LANES).astype(f32)
    pv_spec = full((n_pairs, 1, LANES))
    wide = pltpu.VMEM((L, D), f32)
    sq = lambda dt: pltpu.VMEM((n_pairs, P2, P2), dt)
    tall = pltpu.VMEM((n_pairs, 2 * P2, LANES), bf16)
    wide2 = pltpu.VMEM((n_pairs, P2, 2 * LANES), bf16)
    return pl.pallas_call(
        functools.partial(_wkv_kernel, n_pairs=n_pairs),
        out_shape=jax.ShapeDtypeStruct((T, D), bf16),
        grid=(batch, nc),
        in_specs=[row(D, 0), row(D, 1), row(D, 2), row(RW_LORA_TILE, 3 * D // RW_LORA_TILE),
                  full((1, D)), full((1, D)), full(w2.shape), full(a2.shape), full(g2.shape),
                  pv_spec, pv_spec, pv_spec, pv_spec, pv_spec],
        out_specs=row(D, 0),
        scratch_shapes=[sq(f32), wide, wide, wide, wide, tall, tall, sq(bf16), sq(bf16), sq(bf16), wide2,
                        sq(f32), sq(bf16), sq(bf16), wide2, sq(f32), tall, sq(f32)],
        compiler_params=_cparams(("parallel", "arbitrary")),
        name="wkv7",
    )(proj, proj, proj, proj, w0.reshape(1, D), a0.reshape(1, D), w2, a2, g2,
      pair_vec(k_k), pair_vec(k_a), pair_vec(r_k), pair_vec(lnx_g), pair_vec(lnx_b))


def _pool_kernel(x_ref, xh_ref, g_ref, sh_ref, sc_ref, gate_ref, w_ref, ps_ref, o_ref, *, tiles_per_batch, tm):
    i = pl.program_id(0)
    halo = 2 * SUBLANES
    g, sh, sc = g_ref[...], sh_ref[0], sc_ref[0]
    x = x_ref[...]
    h = _ada_norm(x, g, sh, sc)
    hh = _ada_norm(xh_ref[...], g, sh, sc)
    hh = jnp.where(i % tiles_per_batch == 0, 0.0, hh)
    ext = jnp.concatenate([hh, h], axis=0)
    pos = (i % tiles_per_batch) * tm + lax.broadcasted_iota(jnp.int32, (tm, 1), 0)
    for gi, w in enumerate(POOL_WINDOWS):
        cs = slice(gi * POOL_GROUP, (gi + 1) * POOL_GROUP)
        s = ext[:, cs]
        d = 1
        while d < w:
            s = s + pltpu.roll(s, d, axis=0)
            d *= 2
        cnt = jnp.minimum(pos + 1, w).astype(f32)
        pooled = s[halo:] / cnt - h[:, cs]
        mixed = _dot(pooled.astype(bf16), w_ref[gi]) * ps_ref[:, cs]
        o_ref[:, cs] = x[:, cs] + gate_ref[0][:, cs] * mixed


def _pool_layer(x, gain, mod, pool_w, pool_scale, *, seq, tm=512):
    T, D = x.shape
    tpb = seq // tm
    halo = 2 * SUBLANES
    vec = pl.BlockSpec((1, D), lambda i: (0, 0))
    return pl.pallas_call(
        functools.partial(_pool_kernel, tiles_per_batch=tpb, tm=tm),
        out_shape=jax.ShapeDtypeStruct((T, D), f32),
        grid=(T // tm,),
        in_specs=[
            pl.BlockSpec((tm, D), lambda i: (i, 0)),
            pl.BlockSpec((halo, D), lambda i: (jnp.maximum(i * (tm // halo) - 1, 0), 0)),
            vec,
            pl.BlockSpec((1, 1, D), lambda i: (i // tpb, 0, 0)),
            pl.BlockSpec((1, 1, D), lambda i: (i // tpb, 0, 1)),
            pl.BlockSpec((1, 1, D), lambda i: (i // tpb, 0, 2)),
            pl.BlockSpec(pool_w.shape, lambda i: (0, 0, 0)),
            vec,
        ],
        out_specs=pl.BlockSpec((tm, D), lambda i: (i, 0)),
        compiler_params=_cparams(("parallel",)),
        name="pool_mixer",
    )(x, x, gain, mod, mod, mod, pool_w, pool_scale.reshape(1, D))


def _route_kernel(x_ref, g_ref, sh_ref, sc_ref, w_ref, b_ref, h_ref, route_ref):
    h = _ada_norm(x_ref[...], g_ref[...], sh_ref[0], sc_ref[0])
    h_ref[...] = _pack_halves(h)
    h_hi, h_mid, _ = _split3(h)
    hw = _dot(h_hi, w_ref[...])
    logits = hw[:, :LANES] + (hw[:, LANES:] + _dot(h_mid, w_ref[:, :LANES])) + b_ref[...]
    lane = lax.broadcasted_iota(jnp.int32, logits.shape, 1)
    big = jnp.int32(LANES)

    def masked_argmax(mask):
        mx = jnp.max(jnp.where(mask, logits, NEG_BIG), axis=-1, keepdims=True)
        idx = jnp.min(jnp.where(jnp.logical_and(mask, logits == mx), lane, big), axis=-1, keepdims=True)
        return mx, idx

    gmask = lane < N_GROUPS
    gmax, gidx = masked_argmax(gmask)
    grp_w = 1.0 / jnp.sum(jnp.where(gmask, jnp.exp(logits - gmax), 0.0), axis=-1, keepdims=True)
    e_lo = EXP_LANE0 + gidx * EXPERTS_PER_GROUP
    emask = jnp.logical_and(lane >= e_lo, lane < e_lo + EXPERTS_PER_GROUP)
    m1, i1 = masked_argmax(emask)
    m2, i2 = masked_argmax(jnp.logical_and(emask, lane != i1))
    e21 = jnp.exp(m2 - m1)
    w1 = grp_w / (1.0 + e21)
    w2 = grp_w * e21 / (1.0 + e21)
    out = jnp.where(lane == 0, (i1 - EXP_LANE0).astype(f32), 0.0)
    out = jnp.where(lane == 1, (i2 - EXP_LANE0).astype(f32), out)
    out = jnp.where(lane == 2, w1, out)
    out = jnp.where(lane == 3, w2, out)
    route_ref[...] = out


def _route(x, gain, mod, w_router, b_router, *, seq, tm=512):
    T, D = x.shape
    tpb = seq // tm
    vec = pl.BlockSpec((1, D), lambda i: (0, 0))
    return pl.pallas_call(
        _route_kernel,
        out_shape=(jax.ShapeDtypeStruct((T, D // 2), jnp.uint32), jax.ShapeDtypeStruct((T, LANES), f32)),
        grid=(T // tm,),
        in_specs=[
            pl.BlockSpec((tm, D), lambda i: (i, 0)),
            vec,
            pl.BlockSpec((1, 1, D), lambda i: (i // tpb, 0, 0)),
            pl.BlockSpec((1, 1, D), lambda i: (i // tpb, 0, 1)),
            pl.BlockSpec((D, 2 * LANES), lambda i: (0, 0)),
            pl.BlockSpec((1, LANES), lambda i: (0, 0)),
        ],
        out_specs=(pl.BlockSpec((tm, D // 2), lambda i: (i, 0)), pl.BlockSpec((tm, LANES), lambda i: (i, 0))),
        compiler_params=_cparams(("parallel",)),
        name="moe_route",
    )(x, gain, mod, mod, w_router, b_router)


def _expert_kernel(be_ref, nu_ref, tok_ref, h_hbm, wg_ref, wu_ref, wd_ref, o_ref, xbuf, sems, xb_s, wg_s, wu_s, wd_s,
                   *, n_blocks):
    del n_blocks
    i = pl.program_id(0)
    slot = i % 2
    n_used = nu_ref[0]
    used = i < n_used

    def row_copy(blk, slot_, r):
        tok = tok_ref[blk * MOE_ROWS + r]
        return pltpu.make_async_copy(h_hbm.at[pl.ds(tok, 1)], xbuf.at[slot_, pl.ds(r, 1)], sems.at[slot_])

    def gather(blk, slot_):
        for r in range(MOE_ROWS):
            row_copy(blk, slot_, r).start(priority=r % 2)

    @pl.when(i == 0)
    def _():
        gather(0, 0)

    @pl.when(i + 1 < n_used)
    def _():
        gather(i + 1, 1 - slot)

    @pl.when(used)
    def _():
        pltpu.make_async_copy(h_hbm.at[pl.ds(0, MOE_ROWS)], xbuf.at[slot], sems.at[slot]).wait()

        @pl.when(jnp.logical_or(i == 0, be_ref[i] != be_ref[jnp.maximum(i - 1, 0)]))
        def _():
            wg_s[...] = wg_ref[0, 0].astype(bf16)
            wu_s[...] = wu_ref[0, 0].astype(bf16)
            wd_s[...] = wd_ref[0, 0].astype(bf16)

        lo, hi = _unpack_halves(xbuf[slot])
        half = lo.shape[1]
        xb_s[:, :half] = lo.astype(bf16)
        xb_s[:, half:] = hi.astype(bf16)
        xb = xb_s[...]
        gate = _dot(xb, wg_s[...])
        up = _dot(xb, wu_s[...])
        hid = (gate * _sigmoid(gate) * up).astype(bf16)
        o_ref[...] = _pack_halves(_dot(hid, wd_s[...]))

    @pl.when(jnp.logical_not(used))
    def _():
        o_ref[...] = jnp.zeros(o_ref.shape, o_ref.dtype)


def _experts(h, slot_tok, block_e, n_used, w_gate, w_up, w_down, layer):
    DE = w_gate.shape[-1]
    D = w_gate.shape[-2]
    DP = h.shape[1]
    n_blocks = block_e.shape[0]

    def wsel(i, be, nu, tok):
        return (layer, be[jnp.minimum(i, nu[0] - 1)], 0, 0)

    grid_spec = pltpu.PrefetchScalarGridSpec(
        num_scalar_prefetch=3,
        grid=(n_blocks,),
        in_specs=[
            pl.BlockSpec(memory_space=pl.ANY),
            pl.BlockSpec((1, 1, D, DE), wsel),
            pl.BlockSpec((1, 1, D, DE), wsel),
            pl.BlockSpec((1, 1, DE, D), wsel),
        ],
        out_specs=pl.BlockSpec((MOE_ROWS, DP), lambda i, be, nu, tok: (i, 0)),
        scratch_shapes=[pltpu.VMEM((2, MOE_ROWS, DP), jnp.uint32), pltpu.SemaphoreType.DMA((2,)),
                        pltpu.VMEM((MOE_ROWS, D), bf16),
                        pltpu.VMEM((D, DE), bf16), pltpu.VMEM((D, DE), bf16), pltpu.VMEM((DE, D), bf16)],
    )
    return pl.pallas_call(
        functools.partial(_expert_kernel, n_blocks=n_blocks),
        out_shape=jax.ShapeDtypeStruct((n_blocks * MOE_ROWS, DP), jnp.uint32),
        grid_spec=grid_spec,
        compiler_params=_cparams(("arbitrary",)),
        name="moe_experts",
    )(block_e, n_used, slot_tok, h, w_gate, w_up, w_down)


def _combine_kernel(dest_ref, out_hbm, x_ref, route_ref, gate_ref, o_ref, rows, sems, *, tm, n_steps):
    i = pl.program_id(0)

    def issue(step, slot):
        def body(r, carry):
            t = step * tm + r
            for kk in range(TOP_K):
                d = dest_ref[t * TOP_K + kk]
                pltpu.make_async_copy(out_hbm.at[pl.ds(d, 1)], rows.at[slot, kk, pl.ds(r, 1)],
                                      sems.at[slot]).start(priority=kk)
            return carry
        lax.fori_loop(0, tm, body, 0, unroll=4)

    @pl.when(i == 0)
    def _():
        issue(0, 0)

    @pl.when(i + 1 < n_steps)
    def _():
        issue(i + 1, (i + 1) % 2)

    slot = i % 2
    for kk in range(TOP_K):
        pltpu.make_async_copy(out_hbm.at[pl.ds(0, tm)], rows.at[slot, kk], sems.at[slot]).wait()
    route = route_ref[...]
    w0, w1 = route[:, 2:3], route[:, 3:4]
    lo0, hi0 = _unpack_halves(rows[slot, 0])
    lo1, hi1 = _unpack_halves(rows[slot, 1])
    half = lo0.shape[1]
    gate = gate_ref[0]
    o_ref[:, :half] = x_ref[:, :half] + gate[:, :half] * (w0 * lo0 + w1 * lo1)
    o_ref[:, half:] = x_ref[:, half:] + gate[:, half:] * (w0 * hi0 + w1 * hi1)


def _combine(out_buf, dest, x, route, mod, *, seq, tm=512):
    T, D = x.shape
    n_steps = T // tm
    tpb = seq // tm
    grid_spec = pltpu.PrefetchScalarGridSpec(
        num_scalar_prefetch=1,
        grid=(n_steps,),
        in_specs=[
            pl.BlockSpec(memory_space=pl.ANY),
            pl.BlockSpec((tm, D), lambda i, d: (i, 0)),
            pl.BlockSpec((tm, LANES), lambda i, d: (i, 0)),
            pl.BlockSpec((1, 1, D), lambda i, d: (i // tpb, 0, 2)),
        ],
        out_specs=pl.BlockSpec((tm, D), lambda i, d: (i, 0)),
        scratch_shapes=[pltpu.VMEM((2, TOP_K, tm, out_buf.shape[1]), out_buf.dtype), pltpu.SemaphoreType.DMA((2,))],
    )
    return pl.pallas_call(
        functools.partial(_combine_kernel, tm=tm, n_steps=n_steps),
        out_shape=jax.ShapeDtypeStruct((T, D), f32),
        grid_spec=grid_spec,
        compiler_params=_cparams(("arbitrary",)),
        name="moe_combine",
    )(dest, out_buf, x, route, mod)


def _dispatch_plan(ids, n_tokens):
    N = n_tokens * TOP_K
    i32 = jnp.int32
    flat_ids = ids.reshape(-1)
    order = jnp.argsort(flat_ids).astype(i32)
    inv_order = jnp.argsort(order).astype(i32)
    experts = jnp.arange(N_EXPERTS, dtype=i32)
    counts = jnp.sum((flat_ids[:, None] == experts[None, :]).astype(i32), axis=0)
    start = jnp.cumsum(counts) - counts
    padded = ((counts + MOE_ROWS - 1) // MOE_ROWS) * MOE_ROWS
    seg_end = jnp.cumsum(padded).astype(i32)
    pad_start = seg_end - padded
    dest = (pad_start - start)[flat_ids] + inv_order
    n_blocks = -(-N // MOE_ROWS) + N_EXPERTS
    blk_row0 = jnp.arange(n_blocks, dtype=i32) * MOE_ROWS
    block_e = jnp.minimum(jnp.sum((blk_row0[:, None] >= seg_end[None, :]).astype(i32), axis=1), N_EXPERTS - 1)
    n_used = (seg_end[-1:] // MOE_ROWS).astype(i32)
    row = jnp.arange(n_blocks * MOE_ROWS, dtype=i32)
    row_e = jnp.repeat(block_e, MOE_ROWS)
    j = row - pad_start[row_e]
    src = order[jnp.clip(start[row_e] + j, 0, N - 1)] // TOP_K
    slot_tok = jnp.where(j < counts[row_e], src, 0).astype(i32)
    return dest.astype(i32), slot_tok, block_e.astype(i32), n_used


def _moe_layer(x, gain, mod, w_router, b_router, w_gate, w_up, w_down, layer, *, seq):
    T, D = x.shape
    h, route = _route(x, gain, mod, w_router, b_router, seq=seq)
    ids = route[:, :TOP_K].astype(jnp.int32)
    dest, slot_tok, block_e, n_used = _dispatch_plan(ids, T)
    out_buf = _experts(h, slot_tok, block_e, n_used, w_gate, w_up, w_down, layer)
    return _combine(out_buf, dest, x, route, mod, seq=seq)


def _router_params(w_grp, b_grp, w_exp, b_exp):
    D = w_grp.shape[0]
    w = jnp.zeros((D, LANES), f32).at[:, :N_GROUPS].set(w_grp).at[:, EXP_LANE0:EXP_LANE0 + N_EXPERTS].set(w_exp)
    b = jnp.zeros((1, LANES), f32).at[0, :N_GROUPS].set(b_grp).at[0, EXP_LANE0:EXP_LANE0 + N_EXPERTS].set(b_exp)
    w_hi = w.astype(bf16)
    w_mid = (w - w_hi.astype(f32)).astype(bf16)
    return jnp.concatenate([w_hi, w_mid], axis=1), b


def _pad_cols(w, n):
    return jnp.zeros((w.shape[0], n), w.dtype).at[:, :w.shape[1]].set(w)


def _pad_rows(w, n):
    return jnp.zeros((n, w.shape[1]), w.dtype).at[:w.shape[0]].set(w)


def kernel(x, c, norm_g, ada_w, ada_b, rel_bias, attn_w_in, attn_w_o, attn_q_gain, attn_k_gain, attn_sinks, rw_mu, rw_w_rkv, rw_w0, rw_w1, rw_w2, rw_a0, rw_a1, rw_a2, rw_g1, rw_g2, rw_k_k, rw_k_a, rw_r_k, rw_lnx_g, rw_lnx_b, rw_w_o, pool_w, pool_scale, moe_w_grp, moe_b_grp, moe_w_exp, moe_b_exp, moe_w_gate, moe_w_up, moe_w_down):
    B, S, D = x.shape
    T = B * S
    xt = x.reshape(T, D)
    mods = _ada_mods(c, ada_w, ada_b)
    bias = _band_bias_masked(rel_bias)
    for layer in range(DEPTH):
        kind, idx = layer % N_MIXERS, layer // N_MIXERS
        gain = norm_g[layer, 0].reshape(1, D)
        mod = mods[2 * layer]
        if kind == 0:
            qkv = _norm_mm(xt, gain, mod, attn_w_in[idx].astype(bf16), seq=S)
            o = _attention(qkv, bias, attn_q_gain[idx], attn_k_gain[idx], attn_sinks[idx], batch=B, seq=S)
            xt = _mm_res(o, attn_w_o[idx].astype(bf16), xt, mod, seq=S)
        elif kind == 1:
            w_all = jnp.concatenate(
                [rw_w_rkv[idx, 0], rw_w_rkv[idx, 1], rw_w_rkv[idx, 2], _pad_cols(rw_w1[idx], LANES),
                 _pad_cols(rw_a1[idx], LANES), _pad_cols(rw_g1[idx], RW_PROJ_TILE - 2 * LANES)], axis=1).astype(bf16)
            proj = _rwkv_proj(xt, gain, mod, rw_mu[idx], w_all, seq=S)
            yg = _wkv(proj, rw_w0[idx], rw_a0[idx],
                      _pad_rows(rw_w2[idx], LANES).astype(bf16), _pad_rows(rw_a2[idx], LANES).astype(bf16),
                      rw_g2[idx].astype(bf16), rw_k_k[idx], rw_k_a[idx], rw_r_k[idx], rw_lnx_g[idx], rw_lnx_b[idx],
                      batch=B, seq=S)
            xt = _mm_res(yg, rw_w_o[idx].astype(bf16), xt, mod, seq=S)
        else:
            xt = _pool_layer(xt, gain, mod, pool_w[idx].astype(bf16), pool_scale[idx], seq=S)
        w_router, b_router = _router_params(moe_w_grp[layer], moe_b_grp[layer], moe_w_exp[layer], moe_b_exp[layer])
        xt = _moe_layer(xt, norm_g[layer, 1].reshape(1, D), mods[2 * layer + 1], w_router, b_router,
                        moe_w_gate, moe_w_up, moe_w_down, layer, seq=S)
    return xt.reshape(B, S, D)
```

```python
import functools
import math

import jax
import jax.numpy as jnp
import numpy as np
from jax import lax
from jax.experimental import pallas as pl
from jax.experimental.pallas import tpu as pltpu

f32 = jnp.float32
bf16 = jnp.bfloat16

D_MODEL = 2048
DEPTH = 4
N_MIXERS = 3
HEAD_DIM = 64
N_HEADS = D_MODEL // HEAD_DIM
N_KV_HEADS = 4
GQA_GROUP = N_HEADS // N_KV_HEADS
WINDOW = 128
Q_DIM = N_HEADS * HEAD_DIM
KV_DIM = N_KV_HEADS * HEAD_DIM
QKV_DIM = Q_DIM + 2 * KV_DIM
N_BUCKETS = 32
MAX_DISTANCE = 128
RWKV_HEAD = 64
GN_EPS = 64e-5
POOL_WINDOWS = (2, 4, 8, 16)
POOL_GROUP = D_MODEL // len(POOL_WINDOWS)
N_GROUPS = 4
EXPERTS_PER_GROUP = 8
N_EXPERTS = N_GROUPS * EXPERTS_PER_GROUP
TOP_K = 2
D_EXPERT = D_MODEL // 4
NORM_EPS = 1e-6

LANES = 128
SUBLANES = 8
VMEM_LIMIT = 56 * 1024 * 1024

MOE_ROWS = 512
WKV_CHUNK = 64
EXP_LANE0 = 32
NEG_BIG = -1e30


def _cparams(sem):
    return pltpu.CompilerParams(dimension_semantics=sem, vmem_limit_bytes=VMEM_LIMIT)


def _ada_norm(xf, gain, shift, scale):
    ms = jnp.mean(xf * xf, axis=-1, keepdims=True)
    return xf * lax.rsqrt(ms + NORM_EPS) * gain * (1.0 + scale) + shift


def _sigmoid(z):
    return 1.0 / (1.0 + jnp.exp(-z))


def _split3(x):
    hi = x.astype(bf16)
    r1 = x - hi.astype(f32)
    mid = r1.astype(bf16)
    lo = (r1 - mid.astype(f32)).astype(bf16)
    return hi, mid, lo


def _pack_halves(x):
    n = x.shape[1] // 2
    lo = pltpu.bitcast(x[:, :n].astype(bf16).astype(f32), jnp.uint32)
    hi = pltpu.bitcast(x[:, n:].astype(bf16).astype(f32), jnp.uint32)
    return (lo >> 16) | hi


def _unpack_halves(u):
    return pltpu.bitcast(u << 16, f32), pltpu.bitcast(u & jnp.uint32(0xFFFF0000), f32)


def _dot(a, b):
    return jnp.dot(a, b, preferred_element_type=f32)


def _dot_nt(a, b):
    return lax.dot_general(a, b, (((1,), (1,)), ((), ())), preferred_element_type=f32)


def _dot_tn(a, b):
    return lax.dot_general(a, b, (((0,), (0,)), ((), ())), preferred_element_type=f32)


def _ada_kernel(c_ref, w_ref, b_ref, o_ref):
    c = c_ref[...]
    ca = (c * _sigmoid(c)).astype(bf16)
    o_ref[0] = _dot(ca, w_ref[0].astype(bf16)) + b_ref[0]


def _ada_mods(c, ada_w, ada_b):
    B, D = c.shape
    n_mod = ada_w.shape[0] * ada_w.shape[1]
    N = ada_w.shape[-1]
    tn = 1024
    c_pad = jnp.zeros((SUBLANES, D), f32).at[:B].set(c)
    out = pl.pallas_call(
        _ada_kernel,
        out_shape=jax.ShapeDtypeStruct((n_mod, SUBLANES, N), f32),
        grid=(n_mod, N // tn),
        in_specs=[
            pl.BlockSpec((SUBLANES, D), lambda m, j: (0, 0)),
            pl.BlockSpec((1, D, tn), lambda m, j: (m, 0, j)),
            pl.BlockSpec((1, 1, tn), lambda m, j: (m, 0, j)),
        ],
        out_specs=pl.BlockSpec((1, SUBLANES, tn), lambda m, j: (m, 0, j)),
        compiler_params=_cparams(("parallel", "parallel")),
        name="ada_mods",
    )(c_pad, ada_w.reshape(n_mod, D, N), ada_b.reshape(n_mod, 1, N))
    return out[:, :B].reshape(n_mod, B, 1, N)


def _lookahead_tile(n_tiles):
    def tile(i, j):
        return jnp.where(jnp.logical_and(i == 0, j == 0), 0, jnp.minimum(i + 1, n_tiles - 1))
    return tile


def _norm_mm_kernel(x_ref, g_ref, sh_ref, sc_ref, w_ref, o_ref, lhs_ref):
    i = pl.program_id(0)
    j = pl.program_id(1)
    last = pl.num_programs(1) - 1
    cur = i % 2

    def normed():
        return _ada_norm(x_ref[...], g_ref[...], sh_ref[0], sc_ref[0]).astype(bf16)

    @pl.when(jnp.logical_and(i == 0, j == 0))
    def _():
        lhs_ref[0] = normed()

    @pl.when(j < last)
    def _():
        o_ref[...] = _dot(lhs_ref[cur], w_ref[...]).astype(o_ref.dtype)

    @pl.when(j == last)
    def _():
        o_ref[...] = _dot(lhs_ref[cur], w_ref[...]).astype(o_ref.dtype)
        lhs_ref[1 - cur] = normed()


def _norm_mm(x, gain, mod, w, *, seq, tm=512, tn=1280):
    T, D = x.shape
    N = w.shape[1]
    tpb = seq // tm
    nxt = _lookahead_tile(T // tm)
    assert N // tn >= 2
    return pl.pallas_call(
        _norm_mm_kernel,
        out_shape=jax.ShapeDtypeStruct((T, N), bf16),
        grid=(T // tm, N // tn),
        in_specs=[
            pl.BlockSpec((tm, D), lambda i, j: (nxt(i, j), 0)),
            pl.BlockSpec((1, D), lambda i, j: (0, 0)),
            pl.BlockSpec((1, 1, D), lambda i, j: (nxt(i, j) // tpb, 0, 0)),
            pl.BlockSpec((1, 1, D), lambda i, j: (nxt(i, j) // tpb, 0, 1)),
            pl.BlockSpec((D, tn), lambda i, j: (0, j)),
        ],
        out_specs=pl.BlockSpec((tm, tn), lambda i, j: (i, j)),
        scratch_shapes=[pltpu.VMEM((2, tm, D), bf16)],
        compiler_params=_cparams(("arbitrary", "arbitrary")),
        name="norm_mm",
    )(x, gain, mod, mod, w)


RW_MIXES = 6
RW_LORA_TILE = 512
RW_PROJ_TILE = 1024
RW_PREP_ROWS = 16


def _rwkv_proj_kernel(x_ref, xh_ref, g_ref, sh_ref, sc_ref, mu_ref, w_ref, o_ref, lhs_ref, *, tiles_per_batch, n_big,
                      chunks_per_mix):
    i = pl.program_id(0)
    j = pl.program_id(1)

    @pl.when(j == 0)
    def _():
        g, sh, sc = g_ref[...], sh_ref[0], sc_ref[0]
        h_last = _ada_norm(xh_ref[...], g, sh, sc)[SUBLANES - 1:SUBLANES]
        h_last = jnp.where(i % tiles_per_batch == 0, 0.0, h_last)
        rows = RW_PREP_ROWS
        row = lax.broadcasted_iota(jnp.int32, (rows, x_ref.shape[1]), 0)

        def prep(c, prev):
            r0 = pl.multiple_of(c * rows, rows)
            h = _ada_norm(x_ref[pl.ds(r0, rows), :], g, sh, sc)
            dh = jnp.where(row == 0, prev, pltpu.roll(h, 1, axis=0)) - h
            for m in range(RW_MIXES):
                lhs_ref[m, pl.ds(r0, rows), :] = (h + dh * mu_ref[m:m + 1, :]).astype(bf16)
            return h[rows - 1:rows]

        lax.fori_loop(0, x_ref.shape[0] // rows, prep, h_last)

    @pl.when(j < n_big)
    def _():
        o_ref[...] = _dot(lhs_ref[j // chunks_per_mix], w_ref[...]).astype(o_ref.dtype)

    @pl.when(j == n_big)
    def _():
        q = LANES
        o_ref[:, :RW_LORA_TILE] = jnp.concatenate(
            [_dot(lhs_ref[3], w_ref[:, :q]), _dot(lhs_ref[4], w_ref[:, q:2 * q]),
             _dot(lhs_ref[5], w_ref[:, 2 * q:RW_LORA_TILE])], axis=1).astype(o_ref.dtype)
        o_ref[:, RW_LORA_TILE:] = jnp.zeros((o_ref.shape[0], o_ref.shape[1] - RW_LORA_TILE), o_ref.dtype)


def _rwkv_proj(x, gain, mod, mu, w_all, *, seq, tm=512, tn=RW_PROJ_TILE):
    T, D = x.shape
    n_big = 3 * D // tn
    tpb = seq // tm
    vec_spec = pl.BlockSpec((1, D), lambda i, j: (0, 0))
    return pl.pallas_call(
        functools.partial(_rwkv_proj_kernel, tiles_per_batch=tpb, n_big=n_big, chunks_per_mix=D // tn),
        out_shape=jax.ShapeDtypeStruct((T, w_all.shape[1]), bf16),
        grid=(T // tm, n_big + 1),
        in_specs=[
            pl.BlockSpec((tm, D), lambda i, j: (i, 0)),
            pl.BlockSpec((SUBLANES, D), lambda i, j: (jnp.maximum(i * (tm // SUBLANES) - 1, 0), 0)),
            vec_spec,
            pl.BlockSpec((1, 1, D), lambda i, j: (i // tpb, 0, 0)),
            pl.BlockSpec((1, 1, D), lambda i, j: (i // tpb, 0, 1)),
            pl.BlockSpec((RW_MIXES, D), lambda i, j: (0, 0)),
            pl.BlockSpec((D, tn), lambda i, j: (0, j)),
        ],
        out_specs=pl.BlockSpec((tm, tn), lambda i, j: (i, j)),
        scratch_shapes=[pltpu.VMEM((RW_MIXES, tm, D), bf16)],
        compiler_params=_cparams(("parallel", "arbitrary")),
        name="rwkv_proj",
    )(x, x, gain, mod, mod, mu, w_all)


def _mm_res_kernel(a_ref, w_ref, x_ref, gate_ref, o_ref):
    o_ref[...] = x_ref[...] + gate_ref[0] * _dot(a_ref[...], w_ref[...])


def _mm_res(a, w, x, mod, *, seq, tm=1024, tn=1024):
    T, K = a.shape
    N = w.shape[1]
    tpb = seq // tm
    gate_blk = 2 * (N // tn)
    return pl.pallas_call(
        _mm_res_kernel,
        out_shape=jax.ShapeDtypeStruct((T, N), f32),
        grid=(T // tm, N // tn),
        in_specs=[
            pl.BlockSpec((tm, K), lambda i, j: (i, 0)),
            pl.BlockSpec((K, tn), lambda i, j: (0, j)),
            pl.BlockSpec((tm, tn), lambda i, j: (i, j)),
            pl.BlockSpec((1, 1, tn), lambda i, j: (i // tpb, 0, gate_blk + j)),
        ],
        out_specs=pl.BlockSpec((tm, tn), lambda i, j: (i, j)),
        compiler_params=_cparams(("parallel", "parallel")),
        name="mm_res",
    )(a, w, x, mod)


def _attn_kernel(sink_ref, q_ref, kp_ref, kc_ref, vp_ref, vc_ref, bias_ref, qg_ref, kg_ref, o_ref):
    n = pl.program_id(1)
    W = WINDOW
    GW = GQA_GROUP * W
    heads_per_tile = LANES // HEAD_DIM
    n_qt = Q_DIM // LANES
    qt_per_kv = n_qt // N_KV_HEADS
    lane = lax.broadcasted_iota(jnp.int32, (1, LANES), 1)
    m0 = lane < HEAD_DIM
    ri = lax.broadcasted_iota(jnp.int32, (LANES, LANES), 0)
    ci = lax.broadcasted_iota(jnp.int32, (LANES, LANES), 1)
    head_ones = jnp.where((ri // HEAD_DIM) == (ci // HEAD_DIM), 1.0, 0.0).astype(bf16)

    def inv_rms(t):
        tt = t * t
        hi = tt.astype(bf16)
        lo = (tt - hi.astype(f32)).astype(bf16)
        ssq = _dot(hi, head_ones) + _dot(lo, head_ones)
        return lax.rsqrt(ssq * (1.0 / HEAD_DIM) + NORM_EPS)

    q_stack = jnp.concatenate([q_ref[:, t * LANES:(t + 1) * LANES] for t in range(n_qt)], axis=0).astype(f32)
    q_stack = q_stack * inv_rms(q_stack) * qg_ref[...]
    k_band = jnp.concatenate([kp_ref[...], kc_ref[...]], axis=0).astype(f32)
    v_band = jnp.concatenate([vp_ref[...], vc_ref[...]], axis=0).astype(f32)
    col = lax.broadcasted_iota(jnp.int32, (GW, 2 * W), 1)
    key_ok = jnp.logical_or(n > 0, col >= W)
    grp = lax.broadcasted_iota(jnp.int32, (GW, 1), 0) // W
    ones_kv = jnp.ones((2 * W, LANES), bf16)
    for kt in range(KV_DIM // LANES):
        k_tile = k_band[:, kt * LANES:(kt + 1) * LANES]
        k_tile = k_tile * inv_rms(k_tile) * kg_ref[...]
        k_roll = pltpu.roll(k_tile, HEAD_DIM, axis=1)
        v_tile = v_band[:, kt * LANES:(kt + 1) * LANES]
        v_roll = pltpu.roll(v_tile, HEAD_DIM, axis=1)
        for side in range(heads_per_tile):
            h = kt * heads_per_tile + side
            first = m0 if side == 0 else jnp.logical_not(m0)
            k_dup = jnp.where(first, k_tile, k_roll).astype(bf16)
            v_dup = jnp.where(first, v_tile, v_roll).astype(bf16)
            lhs = []
            for j in range(qt_per_kv):
                q_t = q_stack[(h * qt_per_kv + j) * W:(h * qt_per_kv + j + 1) * W]
                lhs += [jnp.where(m0, q_t, 0.0), jnp.where(m0, 0.0, q_t)]
            lhs = jnp.concatenate(lhs, axis=0).astype(bf16)
            logits = _dot_nt(lhs, k_dup)
            logits = logits + bias_ref[h].reshape(GW, 2 * W)
            logits = jnp.where(key_ok, logits, NEG_BIG)
            sink = jnp.zeros((GW, 1), f32)
            for g in range(GQA_GROUP):
                sink = jnp.where(grp == g, sink_ref[h * GQA_GROUP + g], sink)
            m = jnp.maximum(jnp.max(logits, axis=-1, keepdims=True), sink)
            p = jnp.exp(logits - m).astype(bf16)
            pv = _dot(p, jnp.concatenate([v_dup, ones_kv], axis=1))
            o_full = pv[:, :LANES] / (pv[:, LANES:] + jnp.exp(sink - m))
            for j in range(qt_per_kv):
                t = h * qt_per_kv + j
                o_ref[:, t * LANES:(t + 1) * LANES] = jnp.where(
                    m0, o_full[2 * j * W:(2 * j + 1) * W], o_full[(2 * j + 1) * W:(2 * j + 2) * W]).astype(o_ref.dtype)


def _band_bias_masked(rel_bias):
    max_exact = N_BUCKETS // 2
    i = np.arange(WINDOW)[:, None]
    j = np.arange(2 * WINDOW)[None, :]
    nn = np.maximum(WINDOW + i - j, 0)
    nf = np.maximum(nn, 1).astype(np.float32)
    large = max_exact + (np.log(nf / max_exact) / math.log(MAX_DISTANCE / max_exact)
                         * (N_BUCKETS - max_exact)).astype(np.int32)
    bucket = np.where(nn < max_exact, nn, np.minimum(large, N_BUCKETS - 1)).reshape(-1)
    onehot = (np.arange(N_BUCKETS)[:, None] == bucket[None, :]).astype(np.float32)
    in_band = ((j > i) & (j <= i + WINDOW)).reshape(1, -1)
    b = jnp.dot(rel_bias.astype(f32).T, onehot, precision=lax.Precision.HIGHEST)
    b = jnp.where(in_band, b, NEG_BIG)
    return b.reshape(N_KV_HEADS, GQA_GROUP, WINDOW, 2 * WINDOW)


def _attention(qkv, bias, q_gain, k_gain, sinks, *, batch, seq):
    T = qkv.shape[0]
    nb = seq // WINDOW
    kcol = Q_DIM // KV_DIM
    vcol = kcol + 1

    def cur(col):
        return lambda b, n, s: (b * nb + n, col)

    def prev(col):
        return lambda b, n, s: (b * nb + jnp.maximum(n - 1, 0), col)

    def tile_gain(g):
        return jnp.tile(g.astype(f32), LANES // HEAD_DIM).reshape(1, LANES)

    grid_spec = pltpu.PrefetchScalarGridSpec(
        num_scalar_prefetch=1,
        grid=(batch, nb),
        in_specs=[
            pl.BlockSpec((WINDOW, Q_DIM), lambda b, n, s: (b * nb + n, 0)),
            pl.BlockSpec((WINDOW, KV_DIM), prev(kcol)),
            pl.BlockSpec((WINDOW, KV_DIM), cur(kcol)),
            pl.BlockSpec((WINDOW, KV_DIM), prev(vcol)),
            pl.BlockSpec((WINDOW, KV_DIM), cur(vcol)),
            pl.BlockSpec((N_KV_HEADS, GQA_GROUP, WINDOW, 2 * WINDOW), lambda b, n, s: (0, 0, 0, 0)),
            pl.BlockSpec((1, LANES), lambda b, n, s: (0, 0)),
            pl.BlockSpec((1, LANES), lambda b, n, s: (0, 0)),
        ],
        out_specs=pl.BlockSpec((WINDOW, Q_DIM), lambda b, n, s: (b * nb + n, 0)),
    )
    return pl.pallas_call(
        _attn_kernel,
        out_shape=jax.ShapeDtypeStruct((T, Q_DIM), bf16),
        grid_spec=grid_spec,
        compiler_params=_cparams(("parallel", "parallel")),
        name="swa_attention",
    )(sinks, qkv, qkv, qkv, qkv, qkv, bias, tile_gain(q_gain * (HEAD_DIM ** -0.5)), tile_gain(k_gain))


def _wkv_kernel(r_ref, k_ref, v_ref, lora_ref, w0_ref, a0_ref, w2_ref, a2_ref, g2_ref,
                kk_ref, ka_ref, rk_ref, lng_ref, lnb_ref, o_ref,
                state_ref, w_s, c_s, a_s, g_s, ar_s, bk_s, vb_s, aab_s, aak_s, abk_s, x_s, p_s, xb_s,
                aav_s, uh_s, uv_s, rs_s, *, n_pairs):
    L = WKV_CHUNK
    P2 = 2 * L
    pairs = range(n_pairs)

    @pl.when(pl.program_id(1) == 0)
    def _():
        state_ref[...] = jnp.zeros(state_ref.shape, f32)

    lw = lora_ref[:, :LANES].astype(f32)
    la = lora_ref[:, LANES:2 * LANES]
    lg = lora_ref[:, 2 * LANES:].astype(f32)
    w_lin = w0_ref[...] + _dot(jnp.tanh(lw).astype(bf16), w2_ref[...])
    neg = -w_lin
    softplus = jnp.maximum(neg, 0.0) + jnp.log(1.0 + jnp.exp(-jnp.abs(neg)))
    logw = -jnp.exp(-softplus - 0.5)
    w_s[...] = logw
    a_s[...] = _sigmoid(a0_ref[...] + _dot(la, a2_ref[...]))
    g_s[...] = _dot(_sigmoid(lg).astype(bf16), g2_ref[...])
    ti = lax.broadcasted_iota(jnp.int32, (L, L), 0)
    tj = lax.broadcasted_iota(jnp.int32, (L, L), 1)
    tri = jnp.where(ti >= tj, 1.0, 0.0).astype(bf16)
    hi, mid, lo = _split3(logw)
    c_s[...] = _dot(tri, hi) + _dot(tri, mid) + _dot(tri, lo)

    lane = lax.broadcasted_iota(jnp.int32, (1, LANES), 1)
    m0 = lane < RWKV_HEAD
    ri = lax.broadcasted_iota(jnp.int32, (P2, P2), 0)
    ci = lax.broadcasted_iota(jnp.int32, (P2, P2), 1)
    same = (ri // L) == (ci // L)
    strict = jnp.logical_and(same, ri > ci)
    incl = jnp.logical_and(same, ri >= ci)
    eye = jnp.where(ri == ci, 1.0, 0.0)

    def seg_sum(x):
        s0 = jnp.sum(jnp.where(m0, x, 0.0), axis=-1, keepdims=True)
        s1 = jnp.sum(jnp.where(m0, 0.0, x), axis=-1, keepdims=True)
        return jnp.where(m0, s0, s1)

    def bd(x):
        return jnp.concatenate([jnp.where(m0, x, 0.0), jnp.where(m0, 0.0, x)], axis=0)

    def cols(p):
        return slice(p * LANES, (p + 1) * LANES)

    def k_mod(p):
        return k_ref[:, cols(p)].astype(f32) * (1.0 + (a_s[:, cols(p)] - 1.0) * ka_ref[p])

    for p in pairs:
        cs = cols(p)
        r = r_ref[:, cs].astype(f32)
        k = k_ref[:, cs].astype(f32)
        a = a_s[:, cs]
        cm = c_s[:, cs]
        kk = k * kk_ref[p]
        kk = kk / jnp.maximum(jnp.sqrt(seg_sum(kk * kk)), 1e-12)
        e_pos = jnp.exp(cm)
        e_neg = jnp.exp(-cm)
        e_exc = jnp.exp(cm - w_s[:, cs])
        ar_s[p, :P2] = bd(-kk * e_exc).astype(bf16)
        ar_s[p, P2:] = bd(r * e_pos).astype(bf16)
        bk_s[p, :P2] = bd(kk * a * e_neg).astype(bf16)
        bk_s[p, P2:] = bd(k_mod(p) * e_neg).astype(bf16)
        vb_s[p] = bd(v_ref[:, cs].astype(f32)).astype(bf16)

    for p in pairs:
        G = _dot_nt(ar_s[p], bk_s[p])
        A_ab = jnp.where(strict, G[:P2, :P2], 0.0)
        aab_s[p] = A_ab.astype(bf16)
        x_s[p] = eye + A_ab
        aak_s[p] = jnp.where(strict, G[:P2, P2:], 0.0).astype(bf16)
        abk_s[p, :, :LANES] = jnp.where(incl, G[P2:, :P2], 0.0).astype(bf16)
        abk_s[p, :, LANES:] = jnp.where(incl, G[P2:, P2:], 0.0).astype(bf16)

    for p in pairs:
        p_s[p] = _dot(aab_s[p], aab_s[p]).astype(bf16)
        aav_s[p, :, :LANES] = ar_s[p, :P2]
        aav_s[p, :, LANES:] = _dot(aak_s[p], vb_s[p]).astype(bf16)
    for _ in range(int(math.log2(L)) - 2):
        for p in pairs:
            pw = p_s[p]
            Z = _dot(jnp.concatenate([x_s[p].astype(bf16), pw], axis=0), pw)
            x_s[p] = x_s[p] + Z[:P2]
            p_s[p] = Z[P2:].astype(bf16)
    for p in pairs:
        X = x_s[p]
        xb_s[p] = (X + _dot(X.astype(bf16), p_s[p])).astype(bf16)
    for p in pairs:
        AU = _dot(xb_s[p], aav_s[p])
        ar_s[p, :P2] = AU[:, :LANES].astype(bf16)
        uh_s[p] = AU[:, LANES:]

    for p in pairs:
        T1 = _dot_nt(ar_s[p], state_ref[p].astype(bf16))
        uv_s[p, :P2] = (T1[:P2] + uh_s[p]).astype(bf16)
        uv_s[p, P2:] = vb_s[p]
        rs_s[p] = T1[P2:]

    for p in pairs:
        cs = cols(p)
        Y = rs_s[p] + _dot(abk_s[p], uv_s[p])
        g_last = jnp.exp(c_s[L - 1:L, cs])
        state_ref[p] = (state_ref[p] + _dot_tn(uv_s[p], bk_s[p])) * g_last
        y = Y[:L] + Y[L:]
        mean = seg_sum(y) * (1.0 / RWKV_HEAD)
        yc = y - mean
        var = seg_sum(yc * yc) * (1.0 / RWKV_HEAD)
        yn = yc * lax.rsqrt(var + GN_EPS) * lng_ref[p] + lnb_ref[p]
        r = r_ref[:, cs].astype(f32)
        v = v_ref[:, cs].astype(f32)
        yn = yn + seg_sum(r * k_mod(p) * rk_ref[p]) * v
        o_ref[:, cs] = (yn * g_s[:, cs]).astype(o_ref.dtype)


def _wkv(proj, w0, a0, w2, a2, g2, k_k, k_a, r_k, lnx_g, lnx_b, *, batch, seq):
    T = proj.shape[0]
    D = w0.shape[0]
    L = WKV_CHUNK
    P2 = 2 * L
    n_pairs = D // LANES
    nc = seq // L

    def row(width, col):
        return pl.BlockSpec((L, width), lambda b, t: (b * nc + t, col))

    def full(shape):
        return pl.BlockSpec(shape, lambda b, t: (0,) * len(shape))

    pair_vec = lambda a: a.reshape(n_pairs, 1, LANES).astype(f32)
    pv_spec = full((n_pairs, 1, LANES))
    wide = pltpu.VMEM((L, D), f32)
    sq = lambda dt: pltpu.VMEM((n_pairs, P2, P2), dt)
    tall = pltpu.VMEM((n_pairs, 2 * P2, LANES), bf16)
    wide2 = pltpu.VMEM((n_pairs, P2, 2 * LANES), bf16)
    return pl.pallas_call(
        functools.partial(_wkv_kernel, n_pairs=n_pairs),
        out_shape=jax.ShapeDtypeStruct((T, D), bf16),
        grid=(batch, nc),
        in_specs=[row(D, 0), row(D, 1), row(D, 2), row(RW_LORA_TILE, 3 * D // RW_LORA_TILE),
                  full((1, D)), full((1, D)), full(w2.shape), full(a2.shape), full(g2.shape),
                  pv_spec, pv_spec, pv_spec, pv_spec, pv_spec],
        out_specs=row(D, 0),
        scratch_shapes=[sq(f32), wide, wide, wide, wide, tall, tall, sq(bf16), sq(bf16), sq(bf16), wide2,
                        sq(f32), sq(bf16), sq(bf16), wide2, sq(f32), tall, sq(f32)],
        compiler_params=_cparams(("parallel", "arbitrary")),
        name="wkv7",
    )(proj, proj, proj, proj, w0.reshape(1, D), a0.reshape(1, D), w2, a2, g2,
      pair_vec(k_k), pair_vec(k_a), pair_vec(r_k), pair_vec(lnx_g), pair_vec(lnx_b))


def _pool_kernel(x_ref, xh_ref, g_ref, sh_ref, sc_ref, gate_ref, w_ref, ps_ref, o_ref, *, tiles_per_batch, tm):
    i = pl.program_id(0)
    halo = 2 * SUBLANES
    g, sh, sc = g_ref[...], sh_ref[0], sc_ref[0]
    x = x_ref[...]
    h = _ada_norm(x, g, sh, sc)
    hh = _ada_norm(xh_ref[...], g, sh, sc)
    hh = jnp.where(i % tiles_per_batch == 0, 0.0, hh)
    ext = jnp.concatenate([hh, h], axis=0)
    pos = (i % tiles_per_batch) * tm + lax.broadcasted_iota(jnp.int32, (tm, 1), 0)
    for gi, w in enumerate(POOL_WINDOWS):
        cs = slice(gi * POOL_GROUP, (gi + 1) * POOL_GROUP)
        s = ext[:, cs]
        d = 1
        while d < w:
            s = s + pltpu.roll(s, d, axis=0)
            d *= 2
        cnt = jnp.minimum(pos + 1, w).astype(f32)
        pooled = s[halo:] / cnt - h[:, cs]
        mixed = _dot(pooled.astype(bf16), w_ref[gi]) * ps_ref[:, cs]
        o_ref[:, cs] = x[:, cs] + gate_ref[0][:, cs] * mixed


def _pool_layer(x, gain, mod, pool_w, pool_scale, *, seq, tm=512):
    T, D = x.shape
    tpb = seq // tm
    halo = 2 * SUBLANES
    vec = pl.BlockSpec((1, D), lambda i: (0, 0))
    return pl.pallas_call(
        functools.partial(_pool_kernel, tiles_per_batch=tpb, tm=tm),
        out_shape=jax.ShapeDtypeStruct((T, D), f32),
        grid=(T // tm,),
        in_specs=[
            pl.BlockSpec((tm, D), lambda i: (i, 0)),
            pl.BlockSpec((halo, D), lambda i: (jnp.maximum(i * (tm // halo) - 1, 0), 0)),
            vec,
            pl.BlockSpec((1, 1, D), lambda i: (i // tpb, 0, 0)),
            pl.BlockSpec((1, 1, D), lambda i: (i // tpb, 0, 1)),
            pl.BlockSpec((1, 1, D), lambda i: (i // tpb, 0, 2)),
            pl.BlockSpec(pool_w.shape, lambda i: (0, 0, 0)),
            vec,
        ],
        out_specs=pl.BlockSpec((tm, D), lambda i: (i, 0)),
        compiler_params=_cparams(("parallel",)),
        name="pool_mixer",
    )(x, x, gain, mod, mod, mod, pool_w, pool_scale.reshape(1, D))


def _route_kernel(x_ref, g_ref, sh_ref, sc_ref, w_ref, b_ref, h_ref, route_ref, cnt_ref):
    @pl.when(pl.program_id(0) == 0)
    def _():
        cnt_ref[...] = jnp.zeros(cnt_ref.shape, f32)

    h = _ada_norm(x_ref[...], g_ref[...], sh_ref[0], sc_ref[0])
    h_ref[...] = _pack_halves(h)
    h_hi, h_mid, _ = _split3(h)
    hw = _dot(h_hi, w_ref[...])
    logits = hw[:, :LANES] + (hw[:, LANES:] + _dot(h_mid, w_ref[:, :LANES])) + b_ref[...]
    lane = lax.broadcasted_iota(jnp.int32, logits.shape, 1)
    big = jnp.int32(LANES)

    def masked_argmax(mask):
        mx = jnp.max(jnp.where(mask, logits, NEG_BIG), axis=-1, keepdims=True)
        idx = jnp.min(jnp.where(jnp.logical_and(mask, logits == mx), lane, big), axis=-1, keepdims=True)
        return mx, idx

    gmask = lane < N_GROUPS
    gmax, gidx = masked_argmax(gmask)
    grp_w = 1.0 / jnp.sum(jnp.where(gmask, jnp.exp(logits - gmax), 0.0), axis=-1, keepdims=True)
    e_lo = EXP_LANE0 + gidx * EXPERTS_PER_GROUP
    emask = jnp.logical_and(lane >= e_lo, lane < e_lo + EXPERTS_PER_GROUP)
    m1, i1 = masked_argmax(emask)
    m2, i2 = masked_argmax(jnp.logical_and(emask, lane != i1))
    e21 = jnp.exp(m2 - m1)
    w1 = grp_w / (1.0 + e21)
    w2 = grp_w * e21 / (1.0 + e21)
    out = jnp.where(lane == 0, (i1 - EXP_LANE0).astype(f32), 0.0)
    out = jnp.where(lane == 1, (i2 - EXP_LANE0).astype(f32), out)
    out = jnp.where(lane == 2, w1, out)
    out = jnp.where(lane == 3, w2, out)
    tm = logits.shape[0]
    pick1 = lane == i1
    pick2 = lane == i2
    both = jnp.where(jnp.logical_or(pick1, pick2), 1.0, 0.0)
    ti = lax.broadcasted_iota(jnp.int32, (tm, tm), 0)
    tj = lax.broadcasted_iota(jnp.int32, (tm, tm), 1)
    before = jnp.where(ti > tj, 1.0, 0.0).astype(bf16)
    earlier = _dot(before, both.astype(bf16)) + cnt_ref[...]
    rank1 = jnp.sum(jnp.where(pick1, earlier, 0.0), axis=-1, keepdims=True)
    rank2 = jnp.sum(jnp.where(pick2, earlier, 0.0), axis=-1, keepdims=True)
    out = jnp.where(lane == 4, rank1, out)
    out = jnp.where(lane == 5, rank2, out)
    route_ref[...] = out
    cnt_ref[...] = cnt_ref[...] + jnp.sum(both, axis=0, keepdims=True)


def _route(x, gain, mod, w_router, b_router, *, seq, tm=512):
    T, D = x.shape
    tpb = seq // tm
    vec = pl.BlockSpec((1, D), lambda i: (0, 0))
    return pl.pallas_call(
        _route_kernel,
        out_shape=(jax.ShapeDtypeStruct((T, D // 2), jnp.uint32), jax.ShapeDtypeStruct((T, LANES), f32),
                   jax.ShapeDtypeStruct((1, LANES), f32)),
        grid=(T // tm,),
        in_specs=[
            pl.BlockSpec((tm, D), lambda i: (i, 0)),
            vec,
            pl.BlockSpec((1, 1, D), lambda i: (i // tpb, 0, 0)),
            pl.BlockSpec((1, 1, D), lambda i: (i // tpb, 0, 1)),
            pl.BlockSpec((D, 2 * LANES), lambda i: (0, 0)),
            pl.BlockSpec((1, LANES), lambda i: (0, 0)),
        ],
        out_specs=(pl.BlockSpec((tm, D // 2), lambda i: (i, 0)), pl.BlockSpec((tm, LANES), lambda i: (i, 0)),
                   pl.BlockSpec((1, LANES), lambda i: (0, 0))),
        compiler_params=_cparams(("arbitrary",)),
        name="moe_route",
    )(x, gain, mod, mod, w_router, b_router)


def _expert_kernel(be_ref, nu_ref, tok_ref, h_hbm, wg_ref, wu_ref, wd_ref, o_ref, xbuf, sems, xb_s, wg_s, wu_s, wd_s,
                   *, n_blocks):
    del n_blocks
    i = pl.program_id(0)
    slot = i % 2
    n_used = nu_ref[0]
    used = i < n_used

    def row_copy(blk, slot_, r):
        tok = tok_ref[blk * MOE_ROWS + r]
        return pltpu.make_async_copy(h_hbm.at[pl.ds(tok, 1)], xbuf.at[slot_, pl.ds(r, 1)], sems.at[slot_])

    def gather(blk, slot_):
        for r in range(MOE_ROWS):
            row_copy(blk, slot_, r).start(priority=r % 2)

    @pl.when(i == 0)
    def _():
        gather(0, 0)

    @pl.when(i + 1 < n_used)
    def _():
        gather(i + 1, 1 - slot)

    @pl.when(used)
    def _():
        pltpu.make_async_copy(h_hbm.at[pl.ds(0, MOE_ROWS)], xbuf.at[slot], sems.at[slot]).wait()

        @pl.when(jnp.logical_or(i == 0, be_ref[i] != be_ref[jnp.maximum(i - 1, 0)]))
        def _():
            wg_s[...] = wg_ref[0, 0].astype(bf16)
            wu_s[...] = wu_ref[0, 0].astype(bf16)
            wd_s[...] = wd_ref[0, 0].astype(bf16)

        lo, hi = _unpack_halves(xbuf[slot])
        half = lo.shape[1]
        xb_s[:, :half] = lo.astype(bf16)
        xb_s[:, half:] = hi.astype(bf16)
        xb = xb_s[...]
        gate = _dot(xb, wg_s[...])
        up = _dot(xb, wu_s[...])
        hid = (gate * _sigmoid(gate) * up).astype(bf16)
        o_ref[...] = _pack_halves(_dot(hid, wd_s[...]))

    @pl.when(jnp.logical_not(used))
    def _():
        o_ref[...] = jnp.zeros(o_ref.shape, o_ref.dtype)


def _experts(h, slot_tok, block_e, n_used, w_gate, w_up, w_down, layer):
    DE = w_gate.shape[-1]
    D = w_gate.shape[-2]
    DP = h.shape[1]
    n_blocks = block_e.shape[0]

    def wsel(i, be, nu, tok):
        return (layer, be[jnp.minimum(i, nu[0] - 1)], 0, 0)

    grid_spec = pltpu.PrefetchScalarGridSpec(
        num_scalar_prefetch=3,
        grid=(n_blocks,),
        in_specs=[
            pl.BlockSpec(memory_space=pl.ANY),
            pl.BlockSpec((1, 1, D, DE), wsel),
            pl.BlockSpec((1, 1, D, DE), wsel),
            pl.BlockSpec((1, 1, DE, D), wsel),
        ],
        out_specs=pl.BlockSpec((MOE_ROWS, DP), lambda i, be, nu, tok: (i, 0)),
        scratch_shapes=[pltpu.VMEM((2, MOE_ROWS, DP), jnp.uint32), pltpu.SemaphoreType.DMA((2,)),
                        pltpu.VMEM((MOE_ROWS, D), bf16),
                        pltpu.VMEM((D, DE), bf16), pltpu.VMEM((D, DE), bf16), pltpu.VMEM((DE, D), bf16)],
    )
    return pl.pallas_call(
        functools.partial(_expert_kernel, n_blocks=n_blocks),
        out_shape=jax.ShapeDtypeStruct((n_blocks * MOE_ROWS, DP), jnp.uint32),
        grid_spec=grid_spec,
        compiler_params=_cparams(("arbitrary",)),
        name="moe_experts",
    )(block_e, n_used, slot_tok, h, w_gate, w_up, w_down)


def _combine_kernel(dest_ref, out_hbm, x_ref, route_ref, gate_ref, o_ref, rows, sems, *, tm, n_steps):
    i = pl.program_id(0)

    def issue(step, slot):
        def body(r, carry):
            t = step * tm + r
            for kk in range(TOP_K):
                d = dest_ref[t * TOP_K + kk]
                pltpu.make_async_copy(out_hbm.at[pl.ds(d, 1)], rows.at[slot, kk, pl.ds(r, 1)],
                                      sems.at[slot]).start(priority=kk)
            return carry
        lax.fori_loop(0, tm, body, 0, unroll=4)

    @pl.when(i == 0)
    def _():
        issue(0, 0)

    @pl.when(i + 1 < n_steps)
    def _():
        issue(i + 1, (i + 1) % 2)

    slot = i % 2
    for kk in range(TOP_K):
        pltpu.make_async_copy(out_hbm.at[pl.ds(0, tm)], rows.at[slot, kk], sems.at[slot]).wait()
    route = route_ref[...]
    w0, w1 = route[:, 2:3], route[:, 3:4]
    lo0, hi0 = _unpack_halves(rows[slot, 0])
    lo1, hi1 = _unpack_halves(rows[slot, 1])
    half = lo0.shape[1]
    gate = gate_ref[0]
    o_ref[:, :half] = x_ref[:, :half] + gate[:, :half] * (w0 * lo0 + w1 * lo1)
    o_ref[:, half:] = x_ref[:, half:] + gate[:, half:] * (w0 * hi0 + w1 * hi1)


def _combine(out_buf, dest, x, route, mod, *, seq, tm=512):
    T, D = x.shape
    n_steps = T // tm
    tpb = seq // tm
    grid_spec = pltpu.PrefetchScalarGridSpec(
        num_scalar_prefetch=1,
        grid=(n_steps,),
        in_specs=[
            pl.BlockSpec(memory_space=pl.ANY),
            pl.BlockSpec((tm, D), lambda i, d: (i, 0)),
            pl.BlockSpec((tm, LANES), lambda i, d: (i, 0)),
            pl.BlockSpec((1, 1, D), lambda i, d: (i // tpb, 0, 2)),
        ],
        out_specs=pl.BlockSpec((tm, D), lambda i, d: (i, 0)),
        scratch_shapes=[pltpu.VMEM((2, TOP_K, tm, out_buf.shape[1]), out_buf.dtype), pltpu.SemaphoreType.DMA((2,))],
    )
    return pl.pallas_call(
        functools.partial(_combine_kernel, tm=tm, n_steps=n_steps),
        out_shape=jax.ShapeDtypeStruct((T, D), f32),
        grid_spec=grid_spec,
        compiler_params=_cparams(("arbitrary",)),
        name="moe_combine",
    )(dest, out_buf, x, route, mod)


def _dispatch_plan(ids, ranks, counts, n_tokens):
    N = n_tokens * TOP_K
    i32 = jnp.int32
    flat_ids = ids.reshape(-1)
    order = jnp.argsort(flat_ids).astype(i32)
    start = jnp.cumsum(counts) - counts
    padded = ((counts + MOE_ROWS - 1) // MOE_ROWS) * MOE_ROWS
    seg_end = jnp.cumsum(padded).astype(i32)
    pad_start = seg_end - padded
    dest = pad_start[flat_ids] + ranks.reshape(-1)
    n_blocks = -(-N // MOE_ROWS) + N_EXPERTS
    blk_row0 = jnp.arange(n_blocks, dtype=i32) * MOE_ROWS
    block_e = jnp.minimum(jnp.sum((blk_row0[:, None] >= seg_end[None, :]).astype(i32), axis=1), N_EXPERTS - 1)
    n_used = (seg_end[-1:] // MOE_ROWS).astype(i32)
    row = jnp.arange(n_blocks * MOE_ROWS, dtype=i32)
    row_e = jnp.repeat(block_e, MOE_ROWS)
    j = row - pad_start[row_e]
    src = order[jnp.clip(start[row_e] + j, 0, N - 1)] // TOP_K
    slot_tok = jnp.where(j < counts[row_e], src, 0).astype(i32)
    return dest.astype(i32), slot_tok, block_e.astype(i32), n_used


def _moe_layer(x, gain, mod, w_router, b_router, w_gate, w_up, w_down, layer, *, seq):
    T, D = x.shape
    h, route, lane_counts = _route(x, gain, mod, w_router, b_router, seq=seq)
    ids_ranks = route[:, :2 * TOP_K + 2].astype(jnp.int32)
    counts = lane_counts[0, EXP_LANE0:EXP_LANE0 + N_EXPERTS].astype(jnp.int32)
    dest, slot_tok, block_e, n_used = _dispatch_plan(ids_ranks[:, :TOP_K], ids_ranks[:, TOP_K + 2:], counts, T)
    out_buf = _experts(h, slot_tok, block_e, n_used, w_gate, w_up, w_down, layer)
    return _combine(out_buf, dest, x, route, mod, seq=seq)


def _router_params(w_grp, b_grp, w_exp, b_exp):
    D = w_grp.shape[0]
    w = jnp.zeros((D, LANES), f32).at[:, :N_GROUPS].set(w_grp).at[:, EXP_LANE0:EXP_LANE0 + N_EXPERTS].set(w_exp)
    b = jnp.zeros((1, LANES), f32).at[0, :N_GROUPS].set(b_grp).at[0, EXP_LANE0:EXP_LANE0 + N_EXPERTS].set(b_exp)
    w_hi = w.astype(bf16)
    w_mid = (w - w_hi.astype(f32)).astype(bf16)
    return jnp.concatenate([w_hi, w_mid], axis=1), b


def _pad_cols(w, n):
    return jnp.zeros((w.shape[0], n), w.dtype).at[:, :w.shape[1]].set(w)


def _pad_rows(w, n):
    return jnp.zeros((n, w.shape[1]), w.dtype).at[:w.shape[0]].set(w)


def kernel(x, c, norm_g, ada_w, ada_b, rel_bias, attn_w_in, attn_w_o, attn_q_gain, attn_k_gain, attn_sinks, rw_mu, rw_w_rkv, rw_w0, rw_w1, rw_w2, rw_a0, rw_a1, rw_a2, rw_g1, rw_g2, rw_k_k, rw_k_a, rw_r_k, rw_lnx_g, rw_lnx_b, rw_w_o, pool_w, pool_scale, moe_w_grp, moe_b_grp, moe_w_exp, moe_b_exp, moe_w_gate, moe_w_up, moe_w_down):
    B, S, D = x.shape
    T = B * S
    xt = x.reshape(T, D)
    mods = _ada_mods(c, ada_w, ada_b)
    bias = _band_bias_masked(rel_bias)
    for layer in range(DEPTH):
        kind, idx = layer % N_MIXERS, layer // N_MIXERS
        gain = norm_g[layer, 0].reshape(1, D)
        mod = mods[2 * layer]
        if kind == 0:
            qkv = _norm_mm(xt, gain, mod, attn_w_in[idx].astype(bf16), seq=S)
            o = _attention(qkv, bias, attn_q_gain[idx], attn_k_gain[idx], attn_sinks[idx], batch=B, seq=S)
            xt = _mm_res(o, attn_w_o[idx].astype(bf16), xt, mod, seq=S)
        elif kind == 1:
            w_all = jnp.concatenate(
                [rw_w_rkv[idx, 0], rw_w_rkv[idx, 1], rw_w_rkv[idx, 2], _pad_cols(rw_w1[idx], LANES),
                 _pad_cols(rw_a1[idx], LANES), _pad_cols(rw_g1[idx], RW_PROJ_TILE - 2 * LANES)], axis=1).astype(bf16)
            proj = _rwkv_proj(xt, gain, mod, rw_mu[idx], w_all, seq=S)
            yg = _wkv(proj, rw_w0[idx], rw_a0[idx],
                      _pad_rows(rw_w2[idx], LANES).astype(bf16), _pad_rows(rw_a2[idx], LANES).astype(bf16),
                      rw_g2[idx].astype(bf16), rw_k_k[idx], rw_k_a[idx], rw_r_k[idx], rw_lnx_g[idx], rw_lnx_b[idx],
                      batch=B, seq=S)
            xt = _mm_res(yg, rw_w_o[idx].astype(bf16), xt, mod, seq=S)
        else:
            xt = _pool_layer(xt, gain, mod, pool_w[idx].astype(bf16), pool_scale[idx], seq=S)
        w_router, b_router = _router_params(moe_w_grp[layer], moe_b_grp[layer], moe_w_exp[layer], moe_b_exp[layer])
        xt = _moe_layer(xt, norm_g[layer, 1].reshape(1, D), mods[2 * layer + 1], w_router, b_router,
                        moe_w_gate, moe_w_up, moe_w_down, layer, seq=S)
    return xt.reshape(B, S, D)
```

```python
import functools
import math

import jax
import jax.numpy as jnp
import numpy as np
from jax import lax
from jax.experimental import pallas as pl
from jax.experimental.pallas import tpu as pltpu

f32 = jnp.float32
bf16 = jnp.bfloat16

D_MODEL = 2048
DEPTH = 4
N_MIXERS = 3
HEAD_DIM = 64
N_HEADS = D_MODEL // HEAD_DIM
N_KV_HEADS = 4
GQA_GROUP = N_HEADS // N_KV_HEADS
WINDOW = 128
Q_DIM = N_HEADS * HEAD_DIM
KV_DIM = N_KV_HEADS * HEAD_DIM
QKV_DIM = Q_DIM + 2 * KV_DIM
N_BUCKETS = 32
MAX_DISTANCE = 128
RWKV_HEAD = 64
GN_EPS = 64e-5
POOL_WINDOWS = (2, 4, 8, 16)
POOL_GROUP = D_MODEL // len(POOL_WINDOWS)
N_GROUPS = 4
EXPERTS_PER_GROUP = 8
N_EXPERTS = N_GROUPS * EXPERTS_PER_GROUP
TOP_K = 2
D_EXPERT = D_MODEL // 4
NORM_EPS = 1e-6

LANES = 128
SUBLANES = 8
VMEM_LIMIT = 56 * 1024 * 1024

MOE_ROWS = 512
WKV_CHUNK = 64
WKV_STEP_CHUNKS = 2
EXP_LANE0 = 32
NEG_BIG = -1e30


def _cparams(sem):
    return pltpu.CompilerParams(dimension_semantics=sem, vmem_limit_bytes=VMEM_LIMIT)


def _ada_norm(xf, gain, shift, scale):
    ms = jnp.mean(xf * xf, axis=-1, keepdims=True)
    return xf * lax.rsqrt(ms + NORM_EPS) * gain * (1.0 + scale) + shift


def _sigmoid(z):
    return 1.0 / (1.0 + jnp.exp(-z))


def _split3(x):
    hi = x.astype(bf16)
    r1 = x - hi.astype(f32)
    mid = r1.astype(bf16)
    lo = (r1 - mid.astype(f32)).astype(bf16)
    return hi, mid, lo


def _pack_halves(x):
    n = x.shape[1] // 2
    lo = pltpu.bitcast(x[:, :n].astype(bf16).astype(f32), jnp.uint32)
    hi = pltpu.bitcast(x[:, n:].astype(bf16).astype(f32), jnp.uint32)
    return (lo >> 16) | hi


def _unpack_halves(u):
    return pltpu.bitcast(u << 16, f32), pltpu.bitcast(u & jnp.uint32(0xFFFF0000), f32)


def _dot(a, b):
    return jnp.dot(a, b, preferred_element_type=f32)


def _dot_nt(a, b):
    return lax.dot_general(a, b, (((1,), (1,)), ((), ())), preferred_element_type=f32)


def _dot_tn(a, b):
    return lax.dot_general(a, b, (((0,), (0,)), ((), ())), preferred_element_type=f32)


def _ada_kernel(c_ref, w_ref, b_ref, o_ref):
    c = c_ref[...]
    ca = (c * _sigmoid(c)).astype(bf16)
    o_ref[0] = _dot(ca, w_ref[0].astype(bf16)) + b_ref[0]


def _ada_mods(c, ada_w, ada_b):
    B, D = c.shape
    n_mod = ada_w.shape[0] * ada_w.shape[1]
    N = ada_w.shape[-1]
    tn = 1024
    c_pad = jnp.zeros((SUBLANES, D), f32).at[:B].set(c)
    out = pl.pallas_call(
        _ada_kernel,
        out_shape=jax.ShapeDtypeStruct((n_mod, SUBLANES, N), f32),
        grid=(n_mod, N // tn),
        in_specs=[
            pl.BlockSpec((SUBLANES, D), lambda m, j: (0, 0)),
            pl.BlockSpec((1, D, tn), lambda m, j: (m, 0, j)),
            pl.BlockSpec((1, 1, tn), lambda m, j: (m, 0, j)),
        ],
        out_specs=pl.BlockSpec((1, SUBLANES, tn), lambda m, j: (m, 0, j)),
        compiler_params=_cparams(("parallel", "parallel")),
        name="ada_mods",
    )(c_pad, ada_w.reshape(n_mod, D, N), ada_b.reshape(n_mod, 1, N))
    return out[:, :B].reshape(n_mod, B, 1, N)


def _lookahead_tile(n_tiles):
    def tile(i, j):
        return jnp.where(jnp.logical_and(i == 0, j == 0), 0, jnp.minimum(i + 1, n_tiles - 1))
    return tile


def _norm_mm_kernel(x_ref, g_ref, sh_ref, sc_ref, w_ref, o_ref, lhs_ref):
    i = pl.program_id(0)
    j = pl.program_id(1)
    last = pl.num_programs(1) - 1
    cur = i % 2

    def normed():
        return _ada_norm(x_ref[...], g_ref[...], sh_ref[0], sc_ref[0]).astype(bf16)

    @pl.when(jnp.logical_and(i == 0, j == 0))
    def _():
        lhs_ref[0] = normed()

    @pl.when(j < last)
    def _():
        o_ref[...] = _dot(lhs_ref[cur], w_ref[...]).astype(o_ref.dtype)

    @pl.when(j == last)
    def _():
        o_ref[...] = _dot(lhs_ref[cur], w_ref[...]).astype(o_ref.dtype)
        lhs_ref[1 - cur] = normed()


def _norm_mm(x, gain, mod, w, *, seq, tm=512, tn=1280):
    T, D = x.shape
    N = w.shape[1]
    tpb = seq // tm
    nxt = _lookahead_tile(T // tm)
    assert N // tn >= 2
    return pl.pallas_call(
        _norm_mm_kernel,
        out_shape=jax.ShapeDtypeStruct((T, N), bf16),
        grid=(T // tm, N // tn),
        in_specs=[
            pl.BlockSpec((tm, D), lambda i, j: (nxt(i, j), 0)),
            pl.BlockSpec((1, D), lambda i, j: (0, 0)),
            pl.BlockSpec((1, 1, D), lambda i, j: (nxt(i, j) // tpb, 0, 0)),
            pl.BlockSpec((1, 1, D), lambda i, j: (nxt(i, j) // tpb, 0, 1)),
            pl.BlockSpec((D, tn), lambda i, j: (0, j)),
        ],
        out_specs=pl.BlockSpec((tm, tn), lambda i, j: (i, j)),
        scratch_shapes=[pltpu.VMEM((2, tm, D), bf16)],
        compiler_params=_cparams(("arbitrary", "arbitrary")),
        name="norm_mm",
    )(x, gain, mod, mod, w)


RW_MIXES = 6
RW_LORA_TILE = 512
RW_PROJ_TILE = 1024
RW_PREP_ROWS = 16


def _rwkv_proj_kernel(x_ref, xh_ref, g_ref, sh_ref, sc_ref, mu_ref, w_ref, o_ref, lhs_ref, *, tiles_per_batch, n_big,
                      chunks_per_mix):
    i = pl.program_id(0)
    j = pl.program_id(1)

    @pl.when(j == 0)
    def _():
        g, sh, sc = g_ref[...], sh_ref[0], sc_ref[0]
        h_last = _ada_norm(xh_ref[...], g, sh, sc)[SUBLANES - 1:SUBLANES]
        h_last = jnp.where(i % tiles_per_batch == 0, 0.0, h_last)
        rows = RW_PREP_ROWS
        row = lax.broadcasted_iota(jnp.int32, (rows, x_ref.shape[1]), 0)

        def prep(c, prev):
            r0 = pl.multiple_of(c * rows, rows)
            h = _ada_norm(x_ref[pl.ds(r0, rows), :], g, sh, sc)
            dh = jnp.where(row == 0, prev, pltpu.roll(h, 1, axis=0)) - h
            for m in range(RW_MIXES):
                lhs_ref[m, pl.ds(r0, rows), :] = (h + dh * mu_ref[m:m + 1, :]).astype(bf16)
            return h[rows - 1:rows]

        lax.fori_loop(0, x_ref.shape[0] // rows, prep, h_last)

    @pl.when(j < n_big)
    def _():
        o_ref[...] = _dot(lhs_ref[j // chunks_per_mix], w_ref[...]).astype(o_ref.dtype)

    @pl.when(j == n_big)
    def _():
        q = LANES
        o_ref[:, :RW_LORA_TILE] = jnp.concatenate(
            [_dot(lhs_ref[3], w_ref[:, :q]), _dot(lhs_ref[4], w_ref[:, q:2 * q]),
             _dot(lhs_ref[5], w_ref[:, 2 * q:RW_LORA_TILE])], axis=1).astype(o_ref.dtype)
        o_ref[:, RW_LORA_TILE:] = jnp.zeros((o_ref.shape[0], o_ref.shape[1] - RW_LORA_TILE), o_ref.dtype)


def _rwkv_proj(x, gain, mod, mu, w_all, *, seq, tm=512, tn=RW_PROJ_TILE):
    T, D = x.shape
    n_big = 3 * D // tn
    tpb = seq // tm
    vec_spec = pl.BlockSpec((1, D), lambda i, j: (0, 0))
    return pl.pallas_call(
        functools.partial(_rwkv_proj_kernel, tiles_per_batch=tpb, n_big=n_big, chunks_per_mix=D // tn),
        out_shape=jax.ShapeDtypeStruct((T, w_all.shape[1]), bf16),
        grid=(T // tm, n_big + 1),
        in_specs=[
            pl.BlockSpec((tm, D), lambda i, j: (i, 0)),
            pl.BlockSpec((SUBLANES, D), lambda i, j: (jnp.maximum(i * (tm // SUBLANES) - 1, 0), 0)),
            vec_spec,
            pl.BlockSpec((1, 1, D), lambda i, j: (i // tpb, 0, 0)),
            pl.BlockSpec((1, 1, D), lambda i, j: (i // tpb, 0, 1)),
            pl.BlockSpec((RW_MIXES, D), lambda i, j: (0, 0)),
            pl.BlockSpec((D, tn), lambda i, j: (0, j)),
        ],
        out_specs=pl.BlockSpec((tm, tn), lambda i, j: (i, j)),
        scratch_shapes=[pltpu.VMEM((RW_MIXES, tm, D), bf16)],
        compiler_params=_cparams(("parallel", "arbitrary")),
        name="rwkv_proj",
    )(x, x, gain, mod, mod, mu, w_all)


def _mm_res_kernel(a_ref, w_ref, x_ref, gate_ref, o_ref):
    o_ref[...] = x_ref[...] + gate_ref[0] * _dot(a_ref[...], w_ref[...])


def _mm_res(a, w, x, mod, *, seq, tm=1024, tn=1024):
    T, K = a.shape
    N = w.shape[1]
    tpb = seq // tm
    gate_blk = 2 * (N // tn)
    return pl.pallas_call(
        _mm_res_kernel,
        out_shape=jax.ShapeDtypeStruct((T, N), f32),
        grid=(T // tm, N // tn),
        in_specs=[
            pl.BlockSpec((tm, K), lambda i, j: (i, 0)),
            pl.BlockSpec((K, tn), lambda i, j: (0, j)),
            pl.BlockSpec((tm, tn), lambda i, j: (i, j)),
            pl.BlockSpec((1, 1, tn), lambda i, j: (i // tpb, 0, gate_blk + j)),
        ],
        out_specs=pl.BlockSpec((tm, tn), lambda i, j: (i, j)),
        compiler_params=_cparams(("parallel", "parallel")),
        name="mm_res",
    )(a, w, x, mod)


def _attn_kernel(sink_ref, q_ref, kp_ref, kc_ref, vp_ref, vc_ref, bias_ref, qg_ref, kg_ref, o_ref):
    n = pl.program_id(1)
    W = WINDOW
    GW = GQA_GROUP * W
    heads_per_tile = LANES // HEAD_DIM
    n_qt = Q_DIM // LANES
    qt_per_kv = n_qt // N_KV_HEADS
    lane = lax.broadcasted_iota(jnp.int32, (1, LANES), 1)
    m0 = lane < HEAD_DIM
    ri = lax.broadcasted_iota(jnp.int32, (LANES, LANES), 0)
    ci = lax.broadcasted_iota(jnp.int32, (LANES, LANES), 1)
    head_ones = jnp.where((ri // HEAD_DIM) == (ci // HEAD_DIM), 1.0, 0.0).astype(bf16)

    def inv_rms(t):
        tt = t * t
        hi = tt.astype(bf16)
        lo = (tt - hi.astype(f32)).astype(bf16)
        ssq = _dot(hi, head_ones) + _dot(lo, head_ones)
        return lax.rsqrt(ssq * (1.0 / HEAD_DIM) + NORM_EPS)

    q_stack = jnp.concatenate([q_ref[:, t * LANES:(t + 1) * LANES] for t in range(n_qt)], axis=0).astype(f32)
    q_stack = q_stack * inv_rms(q_stack) * qg_ref[...]
    k_band = jnp.concatenate([kp_ref[...], kc_ref[...]], axis=0).astype(f32)
    v_band = jnp.concatenate([vp_ref[...], vc_ref[...]], axis=0).astype(f32)
    col = lax.broadcasted_iota(jnp.int32, (GW, 2 * W), 1)
    key_ok = jnp.logical_or(n > 0, col >= W)
    grp = lax.broadcasted_iota(jnp.int32, (GW, 1), 0) // W
    ones_kv = jnp.ones((2 * W, LANES), bf16)
    for kt in range(KV_DIM // LANES):
        k_tile = k_band[:, kt * LANES:(kt + 1) * LANES]
        k_tile = k_tile * inv_rms(k_tile) * kg_ref[...]
        k_roll = pltpu.roll(k_tile, HEAD_DIM, axis=1)
        v_tile = v_band[:, kt * LANES:(kt + 1) * LANES]
        v_roll = pltpu.roll(v_tile, HEAD_DIM, axis=1)
        for side in range(heads_per_tile):
            h = kt * heads_per_tile + side
            first = m0 if side == 0 else jnp.logical_not(m0)
            k_dup = jnp.where(first, k_tile, k_roll).astype(bf16)
            v_dup = jnp.where(first, v_tile, v_roll).astype(bf16)
            lhs = []
            for j in range(qt_per_kv):
                q_t = q_stack[(h * qt_per_kv + j) * W:(h * qt_per_kv + j + 1) * W]
                lhs += [jnp.where(m0, q_t, 0.0), jnp.where(m0, 0.0, q_t)]
            lhs = jnp.concatenate(lhs, axis=0).astype(bf16)
            logits = _dot_nt(lhs, k_dup)
            logits = logits + bias_ref[h].reshape(GW, 2 * W)
            logits = jnp.where(key_ok, logits, NEG_BIG)
            sink = jnp.zeros((GW, 1), f32)
            for g in range(GQA_GROUP):
                sink = jnp.where(grp == g, sink_ref[h * GQA_GROUP + g], sink)
            m = jnp.maximum(jnp.max(logits, axis=-1, keepdims=True), sink)
            p = jnp.exp(logits - m).astype(bf16)
            pv = _dot(p, jnp.concatenate([v_dup, ones_kv], axis=1))
            o_full = pv[:, :LANES] / (pv[:, LANES:] + jnp.exp(sink - m))
            for j in range(qt_per_kv):
                t = h * qt_per_kv + j
                o_ref[:, t * LANES:(t + 1) * LANES] = jnp.where(
                    m0, o_full[2 * j * W:(2 * j + 1) * W], o_full[(2 * j + 1) * W:(2 * j + 2) * W]).astype(o_ref.dtype)


def _band_bias_masked(rel_bias):
    max_exact = N_BUCKETS // 2
    i = np.arange(WINDOW)[:, None]
    j = np.arange(2 * WINDOW)[None, :]
    nn = np.maximum(WINDOW + i - j, 0)
    nf = np.maximum(nn, 1).astype(np.float32)
    large = max_exact + (np.log(nf / max_exact) / math.log(MAX_DISTANCE / max_exact)
                         * (N_BUCKETS - max_exact)).astype(np.int32)
    bucket = np.where(nn < max_exact, nn, np.minimum(large, N_BUCKETS - 1)).reshape(-1)
    onehot = (np.arange(N_BUCKETS)[:, None] == bucket[None, :]).astype(np.float32)
    in_band = ((j > i) & (j <= i + WINDOW)).reshape(1, -1)
    b = jnp.dot(rel_bias.astype(f32).T, onehot, precision=lax.Precision.HIGHEST)
    b = jnp.where(in_band, b, NEG_BIG)
    return b.reshape(N_KV_HEADS, GQA_GROUP, WINDOW, 2 * WINDOW)


def _attention(qkv, bias, q_gain, k_gain, sinks, *, batch, seq):
    T = qkv.shape[0]
    nb = seq // WINDOW
    kcol = Q_DIM // KV_DIM
    vcol = kcol + 1

    def cur(col):
        return lambda b, n, s: (b * nb + n, col)

    def prev(col):
        return lambda b, n, s: (b * nb + jnp.maximum(n - 1, 0), col)

    def tile_gain(g):
        return jnp.tile(g.astype(f32), LANES // HEAD_DIM).reshape(1, LANES)

    grid_spec = pltpu.PrefetchScalarGridSpec(
        num_scalar_prefetch=1,
        grid=(batch, nb),
        in_specs=[
            pl.BlockSpec((WINDOW, Q_DIM), lambda b, n, s: (b * nb + n, 0)),
            pl.BlockSpec((WINDOW, KV_DIM), prev(kcol)),
            pl.BlockSpec((WINDOW, KV_DIM), cur(kcol)),
            pl.BlockSpec((WINDOW, KV_DIM), prev(vcol)),
            pl.BlockSpec((WINDOW, KV_DIM), cur(vcol)),
            pl.BlockSpec((N_KV_HEADS, GQA_GROUP, WINDOW, 2 * WINDOW), lambda b, n, s: (0, 0, 0, 0)),
            pl.BlockSpec((1, LANES), lambda b, n, s: (0, 0)),
            pl.BlockSpec((1, LANES), lambda b, n, s: (0, 0)),
        ],
        out_specs=pl.BlockSpec((WINDOW, Q_DIM), lambda b, n, s: (b * nb + n, 0)),
    )
    return pl.pallas_call(
        _attn_kernel,
        out_shape=jax.ShapeDtypeStruct((T, Q_DIM), bf16),
        grid_spec=grid_spec,
        compiler_params=_cparams(("parallel", "parallel")),
        name="swa_attention",
    )(sinks, qkv, qkv, qkv, qkv, qkv, bias, tile_gain(q_gain * (HEAD_DIM ** -0.5)), tile_gain(k_gain))


def _wkv_kernel(r_ref, k_ref, v_ref, lora_ref, w0_ref, a0_ref, w2_ref, a2_ref, g2_ref,
                kk_ref, ka_ref, rk_ref, lng_ref, lnb_ref, o_ref,
                state_ref, w_s, c_s, a_s, g_s, ar_s, bk_s, vb_s, aab_s, aak_s, abk_s, x_s, p_s, xb_s,
                aav_s, uh_s, uv_s, rs_s, *, n_pairs):
    L = WKV_CHUNK
    P2 = 2 * L
    pairs = range(n_pairs)

    @pl.when(pl.program_id(1) == 0)
    def _():
        state_ref[...] = jnp.zeros(state_ref.shape, f32)

    lw = lora_ref[:, :LANES].astype(f32)
    la = lora_ref[:, LANES:2 * LANES]
    lg = lora_ref[:, 2 * LANES:].astype(f32)
    w_lin = w0_ref[...] + _dot(jnp.tanh(lw).astype(bf16), w2_ref[...])
    neg = -w_lin
    softplus = jnp.maximum(neg, 0.0) + jnp.log(1.0 + jnp.exp(-jnp.abs(neg)))
    logw = -jnp.exp(-softplus - 0.5)
    w_s[...] = logw
    a_s[...] = _sigmoid(a0_ref[...] + _dot(la, a2_ref[...]))
    g_s[...] = _dot(_sigmoid(lg).astype(bf16), g2_ref[...])
    ti = lax.broadcasted_iota(jnp.int32, (L, L), 0)
    tj = lax.broadcasted_iota(jnp.int32, (L, L), 1)
    tri = jnp.where(ti >= tj, 1.0, 0.0).astype(bf16)
    n_sub = w_s.shape[0] // L
    for c in range(n_sub):
        hi, mid, lo = _split3(logw[c * L:(c + 1) * L])
        c_s[c * L:(c + 1) * L] = _dot(tri, hi) + _dot(tri, mid) + _dot(tri, lo)

    lane = lax.broadcasted_iota(jnp.int32, (1, LANES), 1)
    m0 = lane < RWKV_HEAD
    ri = lax.broadcasted_iota(jnp.int32, (P2, P2), 0)
    ci = lax.broadcasted_iota(jnp.int32, (P2, P2), 1)
    same = (ri // L) == (ci // L)
    strict = jnp.logical_and(same, ri > ci)
    incl = jnp.logical_and(same, ri >= ci)
    eye = jnp.where(ri == ci, 1.0, 0.0)

    def seg_sum(x):
        s0 = jnp.sum(jnp.where(m0, x, 0.0), axis=-1, keepdims=True)
        s1 = jnp.sum(jnp.where(m0, 0.0, x), axis=-1, keepdims=True)
        return jnp.where(m0, s0, s1)

    def bd(x):
        return jnp.concatenate([jnp.where(m0, x, 0.0), jnp.where(m0, 0.0, x)], axis=0)

    def cols(p):
        return slice(p * LANES, (p + 1) * LANES)

    def k_mod(p, rows):
        return k_ref[rows, cols(p)].astype(f32) * (1.0 + (a_s[rows, cols(p)] - 1.0) * ka_ref[p])

    for c in range(n_sub):
        rows = slice(c * L, (c + 1) * L)
        for p in pairs:
            cs = cols(p)
            r = r_ref[rows, cs].astype(f32)
            k = k_ref[rows, cs].astype(f32)
            a = a_s[rows, cs]
            cm = c_s[rows, cs]
            kk = k * kk_ref[p]
            kk = kk / jnp.maximum(jnp.sqrt(seg_sum(kk * kk)), 1e-12)
            e_pos = jnp.exp(cm)
            e_neg = jnp.exp(-cm)
            e_exc = jnp.exp(cm - w_s[rows, cs])
            ar_s[p, :P2] = bd(-kk * e_exc).astype(bf16)
            ar_s[p, P2:] = bd(r * e_pos).astype(bf16)
            bk_s[p, :P2] = bd(kk * a * e_neg).astype(bf16)
            bk_s[p, P2:] = bd(k_mod(p, rows) * e_neg).astype(bf16)
            vb_s[p] = bd(v_ref[rows, cs].astype(f32)).astype(bf16)

        for p in pairs:
            G = _dot_nt(ar_s[p], bk_s[p])
            A_ab = jnp.where(strict, G[:P2, :P2], 0.0)
            aab_s[p] = A_ab.astype(bf16)
            x_s[p] = eye + A_ab
            aak_s[p] = jnp.where(strict, G[:P2, P2:], 0.0).astype(bf16)
            abk_s[p, :, :LANES] = jnp.where(incl, G[P2:, :P2], 0.0).astype(bf16)
            abk_s[p, :, LANES:] = jnp.where(incl, G[P2:, P2:], 0.0).astype(bf16)

        for p in pairs:
            p_s[p] = _dot(aab_s[p], aab_s[p]).astype(bf16)
            aav_s[p, :, :LANES] = ar_s[p, :P2]
            aav_s[p, :, LANES:] = _dot(aak_s[p], vb_s[p]).astype(bf16)
        for _ in range(int(math.log2(L)) - 2):
            for p in pairs:
                pw = p_s[p]
                Z = _dot(jnp.concatenate([x_s[p].astype(bf16), pw], axis=0), pw)
                x_s[p] = x_s[p] + Z[:P2]
                p_s[p] = Z[P2:].astype(bf16)
        for p in pairs:
            X = x_s[p]
            xb_s[p] = (X + _dot(X.astype(bf16), p_s[p])).astype(bf16)
        for p in pairs:
            AU = _dot(xb_s[p], aav_s[p])
            ar_s[p, :P2] = AU[:, :LANES].astype(bf16)
            uh_s[p] = AU[:, LANES:]

        for p in pairs:
            T1 = _dot_nt(ar_s[p], state_ref[p].astype(bf16))
            uv_s[p, :P2] = (T1[:P2] + uh_s[p]).astype(bf16)
            uv_s[p, P2:] = vb_s[p]
            rs_s[p] = T1[P2:]

        for p in pairs:
            cs = cols(p)
            Y = rs_s[p] + _dot(abk_s[p], uv_s[p])
            g_last = jnp.exp(c_s[c * L + L - 1:c * L + L, cs])
            state_ref[p] = (state_ref[p] + _dot_tn(uv_s[p], bk_s[p])) * g_last
            y = Y[:L] + Y[L:]
            mean = seg_sum(y) * (1.0 / RWKV_HEAD)
            yc = y - mean
            var = seg_sum(yc * yc) * (1.0 / RWKV_HEAD)
            yn = yc * lax.rsqrt(var + GN_EPS) * lng_ref[p] + lnb_ref[p]
            r = r_ref[rows, cs].astype(f32)
            v = v_ref[rows, cs].astype(f32)
            yn = yn + seg_sum(r * k_mod(p, rows) * rk_ref[p]) * v
            o_ref[rows, cs] = (yn * g_s[rows, cs]).astype(o_ref.dtype)


def _wkv(proj, w0, a0, w2, a2, g2, k_k, k_a, r_k, lnx_g, lnx_b, *, batch, seq):
    T = proj.shape[0]
    D = w0.shape[0]
    L = WKV_CHUNK
    P2 = 2 * L
    n_pairs = D // LANES
    LB = L * WKV_STEP_CHUNKS
    nc = seq // LB

    def row(width, col):
        return pl.BlockSpec((LB, width), lambda b, t: (b * nc + t, col))

    def full(shape):
        return pl.BlockSpec(shape, lambda b, t: (0,) * len(shape))

    pair_vec = lambda a: a.reshape(n_pairs, 1, LANES).astype(f32)
    pv_spec = full((n_pairs, 1, LANES))
    wide = pltpu.VMEM((LB, D), f32)
    sq = lambda dt: pltpu.VMEM((n_pairs, P2, P2), dt)
    tall = pltpu.VMEM((n_pairs, 2 * P2, LANES), bf16)
    wide2 = pltpu.VMEM((n_pairs, P2, 2 * LANES), bf16)
    return pl.pallas_call(
        functools.partial(_wkv_kernel, n_pairs=n_pairs),
        out_shape=jax.ShapeDtypeStruct((T, D), bf16),
        grid=(batch, nc),
        in_specs=[row(D, 0), row(D, 1), row(D, 2), row(RW_LORA_TILE, 3 * D // RW_LORA_TILE),
                  full((1, D)), full((1, D)), full(w2.shape), full(a2.shape), full(g2.shape),
                  pv_spec, pv_spec, pv_spec, pv_spec, pv_spec],
        out_specs=row(D, 0),
        scratch_shapes=[sq(f32), wide, wide, wide, wide, tall, tall, sq(bf16), sq(bf16), sq(bf16), wide2,
                        sq(f32), sq(bf16), sq(bf16), wide2, sq(f32), tall, sq(f32)],
        compiler_params=_cparams(("parallel", "arbitrary")),
        name="wkv7",
    )(proj, proj, proj, proj, w0.reshape(1, D), a0.reshape(1, D), w2, a2, g2,
      pair_vec(k_k), pair_vec(k_a), pair_vec(r_k), pair_vec(lnx_g), pair_vec(lnx_b))


def _pool_kernel(x_ref, xh_ref, g_ref, sh_ref, sc_ref, gate_ref, w_ref, ps_ref, o_ref, *, tiles_per_batch, tm):
    i = pl.program_id(0)
    halo = 2 * SUBLANES
    g, sh, sc = g_ref[...], sh_ref[0], sc_ref[0]
    x = x_ref[...]
    h = _ada_norm(x, g, sh, sc)
    hh = _ada_norm(xh_ref[...], g, sh, sc)
    hh = jnp.where(i % tiles_per_batch == 0, 0.0, hh)
    ext = jnp.concatenate([hh, h], axis=0)
    pos = (i % tiles_per_batch) * tm + lax.broadcasted_iota(jnp.int32, (tm, 1), 0)
    for gi, w in enumerate(POOL_WINDOWS):
        cs = slice(gi * POOL_GROUP, (gi + 1) * POOL_GROUP)
        s = ext[:, cs]
        d = 1
        while d < w:
            s = s + pltpu.roll(s, d, axis=0)
            d *= 2
        cnt = jnp.minimum(pos + 1, w).astype(f32)
        pooled = s[halo:] / cnt - h[:, cs]
        mixed = _dot(pooled.astype(bf16), w_ref[gi]) * ps_ref[:, cs]
        o_ref[:, cs] = x[:, cs] + gate_ref[0][:, cs] * mixed


def _pool_layer(x, gain, mod, pool_w, pool_scale, *, seq, tm=512):
    T, D = x.shape
    tpb = seq // tm
    halo = 2 * SUBLANES
    vec = pl.BlockSpec((1, D), lambda i: (0, 0))
    return pl.pallas_call(
        functools.partial(_pool_kernel, tiles_per_batch=tpb, tm=tm),
        out_shape=jax.ShapeDtypeStruct((T, D), f32),
        grid=(T // tm,),
        in_specs=[
            pl.BlockSpec((tm, D), lambda i: (i, 0)),
            pl.BlockSpec((halo, D), lambda i: (jnp.maximum(i * (tm // halo) - 1, 0), 0)),
            vec,
            pl.BlockSpec((1, 1, D), lambda i: (i // tpb, 0, 0)),
            pl.BlockSpec((1, 1, D), lambda i: (i // tpb, 0, 1)),
            pl.BlockSpec((1, 1, D), lambda i: (i // tpb, 0, 2)),
            pl.BlockSpec(pool_w.shape, lambda i: (0, 0, 0)),
            vec,
        ],
        out_specs=pl.BlockSpec((tm, D), lambda i: (i, 0)),
        compiler_params=_cparams(("parallel",)),
        name="pool_mixer",
    )(x, x, gain, mod, mod, mod, pool_w, pool_scale.reshape(1, D))


def _route_kernel(x_ref, g_ref, sh_ref, sc_ref, w_ref, b_ref, h_ref, route_ref):
    h = _ada_norm(x_ref[...], g_ref[...], sh_ref[0], sc_ref[0])
    h_ref[...] = _pack_halves(h)
    h_hi, h_mid, _ = _split3(h)
    hw = _dot(h_hi, w_ref[...])
    logits = hw[:, :LANES] + (hw[:, LANES:] + _dot(h_mid, w_ref[:, :LANES])) + b_ref[...]
    lane = lax.broadcasted_iota(jnp.int32, logits.shape, 1)
    big = jnp.int32(LANES)

    def masked_argmax(mask):
        mx = jnp.max(jnp.where(mask, logits, NEG_BIG), axis=-1, keepdims=True)
        idx = jnp.min(jnp.where(jnp.logical_and(mask, logits == mx), lane, big), axis=-1, keepdims=True)
        return mx, idx

    gmask = lane < N_GROUPS
    gmax, gidx = masked_argmax(gmask)
    grp_w = 1.0 / jnp.sum(jnp.where(gmask, jnp.exp(logits - gmax), 0.0), axis=-1, keepdims=True)
    e_lo = EXP_LANE0 + gidx * EXPERTS_PER_GROUP
    emask = jnp.logical_and(lane >= e_lo, lane < e_lo + EXPERTS_PER_GROUP)
    m1, i1 = masked_argmax(emask)
    m2, i2 = masked_argmax(jnp.logical_and(emask, lane != i1))
    e21 = jnp.exp(m2 - m1)
    w1 = grp_w / (1.0 + e21)
    w2 = grp_w * e21 / (1.0 + e21)
    out = jnp.where(lane == 0, (i1 - EXP_LANE0).astype(f32), 0.0)
    out = jnp.where(lane == 1, (i2 - EXP_LANE0).astype(f32), out)
    out = jnp.where(lane == 2, w1, out)
    out = jnp.where(lane == 3, w2, out)
    route_ref[...] = out


def _route(x, gain, mod, w_router, b_router, *, seq, tm=512):
    T, D = x.shape
    tpb = seq // tm
    vec = pl.BlockSpec((1, D), lambda i: (0, 0))
    return pl.pallas_call(
        _route_kernel,
        out_shape=(jax.ShapeDtypeStruct((T, D // 2), jnp.uint32), jax.ShapeDtypeStruct((T, LANES), f32)),
        grid=(T // tm,),
        in_specs=[
            pl.BlockSpec((tm, D), lambda i: (i, 0)),
            vec,
            pl.BlockSpec((1, 1, D), lambda i: (i // tpb, 0, 0)),
            pl.BlockSpec((1, 1, D), lambda i: (i // tpb, 0, 1)),
            pl.BlockSpec((D, 2 * LANES), lambda i: (0, 0)),
            pl.BlockSpec((1, LANES), lambda i: (0, 0)),
        ],
        out_specs=(pl.BlockSpec((tm, D // 2), lambda i: (i, 0)), pl.BlockSpec((tm, LANES), lambda i: (i, 0))),
        compiler_params=_cparams(("parallel",)),
        name="moe_route",
    )(x, gain, mod, mod, w_router, b_router)


def _expert_kernel(be_ref, nu_ref, tok_ref, h_hbm, wg_ref, wu_ref, wd_ref, o_ref, xbuf, sems, xb_s, wg_s, wu_s, wd_s,
                   *, n_blocks):
    del n_blocks
    i = pl.program_id(0)
    slot = i % 2
    n_used = nu_ref[0]
    used = i < n_used

    def row_copy(blk, slot_, r):
        tok = tok_ref[blk * MOE_ROWS + r]
        return pltpu.make_async_copy(h_hbm.at[pl.ds(tok, 1)], xbuf.at[slot_, pl.ds(r, 1)], sems.at[slot_])

    def gather(blk, slot_):
        for r in range(MOE_ROWS):
            row_copy(blk, slot_, r).start(priority=r % 2)

    @pl.when(i == 0)
    def _():
        gather(0, 0)

    @pl.when(i + 1 < n_used)
    def _():
        gather(i + 1, 1 - slot)

    @pl.when(used)
    def _():
        pltpu.make_async_copy(h_hbm.at[pl.ds(0, MOE_ROWS)], xbuf.at[slot], sems.at[slot]).wait()

        @pl.when(jnp.logical_or(i == 0, be_ref[i] != be_ref[jnp.maximum(i - 1, 0)]))
        def _():
            wg_s[...] = wg_ref[0, 0].astype(bf16)
            wu_s[...] = wu_ref[0, 0].astype(bf16)
            wd_s[...] = wd_ref[0, 0].astype(bf16)

        lo, hi = _unpack_halves(xbuf[slot])
        half = lo.shape[1]
        xb_s[:, :half] = lo.astype(bf16)
        xb_s[:, half:] = hi.astype(bf16)
        xb = xb_s[...]
        gate = _dot(xb, wg_s[...])
        up = _dot(xb, wu_s[...])
        hid = (gate * _sigmoid(gate) * up).astype(bf16)
        o_ref[...] = _pack_halves(_dot(hid, wd_s[...]))

    @pl.when(jnp.logical_not(used))
    def _():
        o_ref[...] = jnp.zeros(o_ref.shape, o_ref.dtype)


def _experts(h, slot_tok, block_e, n_used, w_gate, w_up, w_down, layer):
    DE = w_gate.shape[-1]
    D = w_gate.shape[-2]
    DP = h.shape[1]
    n_blocks = block_e.shape[0]

    def wsel(i, be, nu, tok):
        return (layer, be[jnp.minimum(i, nu[0] - 1)], 0, 0)

    grid_spec = pltpu.PrefetchScalarGridSpec(
        num_scalar_prefetch=3,
        grid=(n_blocks,),
        in_specs=[
            pl.BlockSpec(memory_space=pl.ANY),
            pl.BlockSpec((1, 1, D, DE), wsel),
            pl.BlockSpec((1, 1, D, DE), wsel),
            pl.BlockSpec((1, 1, DE, D), wsel),
        ],
        out_specs=pl.BlockSpec((MOE_ROWS, DP), lambda i, be, nu, tok: (i, 0)),
        scratch_shapes=[pltpu.VMEM((2, MOE_ROWS, DP), jnp.uint32), pltpu.SemaphoreType.DMA((2,)),
                        pltpu.VMEM((MOE_ROWS, D), bf16),
                        pltpu.VMEM((D, DE), bf16), pltpu.VMEM((D, DE), bf16), pltpu.VMEM((DE, D), bf16)],
    )
    return pl.pallas_call(
        functools.partial(_expert_kernel, n_blocks=n_blocks),
        out_shape=jax.ShapeDtypeStruct((n_blocks * MOE_ROWS, DP), jnp.uint32),
        grid_spec=grid_spec,
        compiler_params=_cparams(("arbitrary",)),
        name="moe_experts",
    )(block_e, n_used, slot_tok, h, w_gate, w_up, w_down)


def _combine_kernel(dest_ref, out_hbm, x_ref, route_ref, gate_ref, o_ref, rows, sems, *, tm, n_steps):
    i = pl.program_id(0)

    def issue(step, slot):
        def body(r, carry):
            t = step * tm + r
            for kk in range(TOP_K):
                d = dest_ref[t * TOP_K + kk]
                pltpu.make_async_copy(out_hbm.at[pl.ds(d, 1)], rows.at[slot, kk, pl.ds(r, 1)],
                                      sems.at[slot]).start(priority=kk)
            return carry
        lax.fori_loop(0, tm, body, 0, unroll=4)

    @pl.when(i == 0)
    def _():
        issue(0, 0)

    @pl.when(i + 1 < n_steps)
    def _():
        issue(i + 1, (i + 1) % 2)

    slot = i % 2
    for kk in range(TOP_K):
        pltpu.make_async_copy(out_hbm.at[pl.ds(0, tm)], rows.at[slot, kk], sems.at[slot]).wait()
    route = route_ref[...]
    w0, w1 = route[:, 2:3], route[:, 3:4]
    lo0, hi0 = _unpack_halves(rows[slot, 0])
    lo1, hi1 = _unpack_halves(rows[slot, 1])
    half = lo0.shape[1]
    gate = gate_ref[0]
    o_ref[:, :half] = x_ref[:, :half] + gate[:, :half] * (w0 * lo0 + w1 * lo1)
    o_ref[:, half:] = x_ref[:, half:] + gate[:, half:] * (w0 * hi0 + w1 * hi1)


def _combine(out_buf, dest, x, route, mod, *, seq, tm=512):
    T, D = x.shape
    n_steps = T // tm
    tpb = seq // tm
    grid_spec = pltpu.PrefetchScalarGridSpec(
        num_scalar_prefetch=1,
        grid=(n_steps,),
        in_specs=[
            pl.BlockSpec(memory_space=pl.ANY),
            pl.BlockSpec((tm, D), lambda i, d: (i, 0)),
            pl.BlockSpec((tm, LANES), lambda i, d: (i, 0)),
            pl.BlockSpec((1, 1, D), lambda i, d: (i // tpb, 0, 2)),
        ],
        out_specs=pl.BlockSpec((tm, D), lambda i, d: (i, 0)),
        scratch_shapes=[pltpu.VMEM((2, TOP_K, tm, out_buf.shape[1]), out_buf.dtype), pltpu.SemaphoreType.DMA((2,))],
    )
    return pl.pallas_call(
        functools.partial(_combine_kernel, tm=tm, n_steps=n_steps),
        out_shape=jax.ShapeDtypeStruct((T, D), f32),
        grid_spec=grid_spec,
        compiler_params=_cparams(("arbitrary",)),
        name="moe_combine",
    )(dest, out_buf, x, route, mod)


def _dispatch_plan(ids, n_tokens):
    N = n_tokens * TOP_K
    i32 = jnp.int32
    flat_ids = ids.reshape(-1)
    order = jnp.argsort(flat_ids).astype(i32)
    inv_order = jnp.argsort(order).astype(i32)
    experts = jnp.arange(N_EXPERTS, dtype=i32)
    counts = jnp.sum((flat_ids[:, None] == experts[None, :]).astype(i32), axis=0)
    start = jnp.cumsum(counts) - counts
    padded = ((counts + MOE_ROWS - 1) // MOE_ROWS) * MOE_ROWS
    seg_end = jnp.cumsum(padded).astype(i32)
    pad_start = seg_end - padded
    dest = (pad_start - start)[flat_ids] + inv_order
    n_blocks = -(-N // MOE_ROWS) + N_EXPERTS
    blk_row0 = jnp.arange(n_blocks, dtype=i32) * MOE_ROWS
    block_e = jnp.minimum(jnp.sum((blk_row0[:, None] >= seg_end[None, :]).astype(i32), axis=1), N_EXPERTS - 1)
    n_used = (seg_end[-1:] // MOE_ROWS).astype(i32)
    row = jnp.arange(n_blocks * MOE_ROWS, dtype=i32)
    row_e = jnp.repeat(block_e, MOE_ROWS)
    j = row - pad_start[row_e]
    src = order[jnp.clip(start[row_e] + j, 0, N - 1)] // TOP_K
    slot_tok = jnp.where(j < counts[row_e], src, 0).astype(i32)
    return dest.astype(i32), slot_tok, block_e.astype(i32), n_used


def _moe_layer(x, gain, mod, w_router, b_router, w_gate, w_up, w_down, layer, *, seq):
    T, D = x.shape
    h, route = _route(x, gain, mod, w_router, b_router, seq=seq)
    ids = route[:, :TOP_K].astype(jnp.int32)
    dest, slot_tok, block_e, n_used = _dispatch_plan(ids, T)
    out_buf = _experts(h, slot_tok, block_e, n_used, w_gate, w_up, w_down, layer)
    return _combine(out_buf, dest, x, route, mod, seq=seq)


def _router_params(w_grp, b_grp, w_exp, b_exp):
    D = w_grp.shape[0]
    w = jnp.zeros((D, LANES), f32).at[:, :N_GROUPS].set(w_grp).at[:, EXP_LANE0:EXP_LANE0 + N_EXPERTS].set(w_exp)
    b = jnp.zeros((1, LANES), f32).at[0, :N_GROUPS].set(b_grp).at[0, EXP_LANE0:EXP_LANE0 + N_EXPERTS].set(b_exp)
    w_hi = w.astype(bf16)
    w_mid = (w - w_hi.astype(f32)).astype(bf16)
    return jnp.concatenate([w_hi, w_mid], axis=1), b


def _pad_cols(w, n):
    return jnp.zeros((w.shape[0], n), w.dtype).at[:, :w.shape[1]].set(w)


def _pad_rows(w, n):
    return jnp.zeros((n, w.shape[1]), w.dtype).at[:w.shape[0]].set(w)


def kernel(x, c, norm_g, ada_w, ada_b, rel_bias, attn_w_in, attn_w_o, attn_q_gain, attn_k_gain, attn_sinks, rw_mu, rw_w_rkv, rw_w0, rw_w1, rw_w2, rw_a0, rw_a1, rw_a2, rw_g1, rw_g2, rw_k_k, rw_k_a, rw_r_k, rw_lnx_g, rw_lnx_b, rw_w_o, pool_w, pool_scale, moe_w_grp, moe_b_grp, moe_w_exp, moe_b_exp, moe_w_gate, moe_w_up, moe_w_down):
    B, S, D = x.shape
    T = B * S
    xt = x.reshape(T, D)
    mods = _ada_mods(c, ada_w, ada_b)
    bias = _band_bias_masked(rel_bias)
    for layer in range(DEPTH):
        kind, idx = layer % N_MIXERS, layer // N_MIXERS
        gain = norm_g[layer, 0].reshape(1, D)
        mod = mods[2 * layer]
        if kind == 0:
            qkv = _norm_mm(xt, gain, mod, attn_w_in[idx].astype(bf16), seq=S)
            o = _attention(qkv, bias, attn_q_gain[idx], attn_k_gain[idx], attn_sinks[idx], batch=B, seq=S)
            xt = _mm_res(o, attn_w_o[idx].astype(bf16), xt, mod, seq=S)
        elif kind == 1:
            w_all = jnp.concatenate(
                [rw_w_rkv[idx, 0], rw_w_rkv[idx, 1], rw_w_rkv[idx, 2], _pad_cols(rw_w1[idx], LANES),
                 _pad_cols(rw_a1[idx], LANES), _pad_cols(rw_g1[idx], RW_PROJ_TILE - 2 * LANES)], axis=1).astype(bf16)
            proj = _rwkv_proj(xt, gain, mod, rw_mu[idx], w_all, seq=S)
            yg = _wkv(proj, rw_w0[idx], rw_a0[idx],
                      _pad_rows(rw_w2[idx], LANES).astype(bf16), _pad_rows(rw_a2[idx], LANES).astype(bf16),
                      rw_g2[idx].astype(bf16), rw_k_k[idx], rw_k_a[idx], rw_r_k[idx], rw_lnx_g[idx], rw_lnx_b[idx],
                      batch=B, seq=S)
            xt = _mm_res(yg, rw_w_o[idx].astype(bf16), xt, mod, seq=S)
        else:
            xt = _pool_layer(xt, gain, mod, pool_w[idx].astype(bf16), pool_scale[idx], seq=S)
        w_router, b_router = _router_params(moe_w_grp[layer], moe_b_grp[layer], moe_w_exp[layer], moe_b_exp[layer])
        xt = _moe_layer(xt, norm_g[layer, 1].reshape(1, D), mods[2 * layer + 1], w_router, b_router,
                        moe_w_gate, moe_w_up, moe_w_down, layer, seq=S)
    return xt.reshape(B, S, D)
```
